```python
import jax, jax.numpy as jnp
from jax import lax
import numpy as np

D_MODEL = 2048
BATCH = 2
SEQ = 4096
DEPTH = 1

CTX_LEN = 256
GRID_W = 64
N_ADALN = 6
EPS = 1e-6
N_HEADS = 8
KV_HEADS = 2
GROUP = N_HEADS // KV_HEADS
HEAD_DIM = 128
AXIS_DIM = HEAD_DIM // 2
ROPE_THETA = 10000.0
Q_BLOCK = 128
ATT_WIDTH = N_HEADS * HEAD_DIM
ATT_SCALE = HEAD_DIM ** -0.5
M_HEADS = 4
M_DK = 128
M_DV = 256
M_CHUNK = 128
M_WIDTH = M_HEADS * M_DV
FORGET_BIAS = 3.0
MIX_WIDTH = ATT_WIDTH + M_WIDTH
IN_WIDTHS = (ATT_WIDTH, KV_HEADS * HEAD_DIM, KV_HEADS * HEAD_DIM,
             M_HEADS * M_DK, M_HEADS * M_DK, M_WIDTH, M_WIDTH, 2 * M_HEADS, 2 * M_HEADS)
IN_COLS = sum(IN_WIDTHS)
N_EXPERTS = 32
TOP_K = 4
D_FF = D_MODEL
SWIGLU_LIMIT = 7.0
SWIGLU_ALPHA = 1.702
MOE_BLOCK = 128

kernel_name = 'hymba_style_gqa_mlstm_moe_dit_layer'


def rmsnorm(x, g):
    xf = x.astype(jnp.float32)
    y = xf * lax.rsqrt(jnp.mean(xf * xf, axis=-1, keepdims=True) + EPS)
    return (y * g).astype(x.dtype)


def modulate(h, shift, scale):
    return h * (1 + scale) + shift


def split_columns(p):
    points = np.cumsum(IN_WIDTHS)[:-1].tolist()
    return jnp.split(p, points, axis=-1)


def axial_rope_tables(rows):
    row = jnp.repeat(jnp.arange(rows, dtype=jnp.float32), GRID_W)
    col = jnp.tile(jnp.arange(GRID_W, dtype=jnp.float32), rows)
    inv = ROPE_THETA ** (-jnp.arange(0, AXIS_DIM, 2, dtype=jnp.float32) / AXIS_DIM)
    ang = jnp.concatenate([row[:, None] * inv, col[:, None] * inv], axis=-1)
    return jnp.cos(ang), jnp.sin(ang)


def apply_rope(x, cos, sin):
    xf = x.astype(jnp.float32).reshape(x.shape[:-1] + (HEAD_DIM // 2, 2))
    x1, x2 = xf[..., 0], xf[..., 1]
    cs, sn = cos[:, None, :], sin[:, None, :]
    out = jnp.stack([x1 * cs - x2 * sn, x1 * sn + x2 * cs], axis=-1)
    return out.reshape(x.shape).astype(x.dtype)


def attention_qkv(aq, ak, av, g_q, g_k):
    B, T = aq.shape[0], aq.shape[1]
    q = rmsnorm(aq.reshape(B, T, N_HEADS, HEAD_DIM), g_q)
    k = rmsnorm(ak.reshape(B, T, KV_HEADS, HEAD_DIM), g_k)
    v = av.reshape(B, T, KV_HEADS, HEAD_DIM)
    return q, k, v


def dense_attention(q, k, v):
    s = jnp.einsum('bqkgd,bskd->bkgqs', q, k).astype(jnp.float32) * ATT_SCALE
    p = jax.nn.softmax(s, axis=-1).astype(v.dtype)
    return jnp.einsum('bkgqs,bskd->bqkgd', p, v)


def latent_attention(q, k_all, v_all):
    B, N = q.shape[0], q.shape[1]
    nb = N // Q_BLOCK
    qb = q.reshape(B, nb, Q_BLOCK, KV_HEADS, GROUP, HEAD_DIM).swapaxes(0, 1)
    o = lax.map(lambda qblk: dense_attention(qblk, k_all, v_all), qb)
    return o.swapaxes(0, 1).reshape(B, N, ATT_WIDTH)


def mlstm_heads(mq, mk, mv, mi, mf):
    B, T = mq.shape[0], mq.shape[1]
    q = mq.astype(jnp.float32).reshape(B, T, M_HEADS, M_DK).transpose(0, 2, 1, 3)
    k = mk.astype(jnp.float32).reshape(B, T, M_HEADS, M_DK).transpose(0, 2, 1, 3) * (M_DK ** -0.5)
    v = mv.astype(jnp.float32).reshape(B, T, M_HEADS, M_DV).transpose(0, 2, 1, 3)
    i_pre = mi.astype(jnp.float32).reshape(B, T, 2, M_HEADS).transpose(2, 0, 3, 1)
    logf = jax.nn.log_sigmoid(mf.astype(jnp.float32).reshape(B, T, 2, M_HEADS).transpose(2, 0, 3, 1))
    return q, k, v, i_pre, logf


def mlstm_chunkwise(q, k, v, i_pre, logf, state):
    B, H, T = q.shape[0], q.shape[1], q.shape[2]
    nc = T // M_CHUNK

    def to_chunks(a):
        return jnp.moveaxis(a.reshape((B, H, nc, M_CHUNK) + a.shape[3:]), 2, 0)

    tril = jnp.tril(jnp.ones((M_CHUNK, M_CHUNK), dtype=bool))

    def step(carry, inp):
        C, n, m = carry
        qc, kc, vc, ic, fc = inp
        b = jnp.cumsum(fc, axis=-1)
        d = jnp.where(tril, b[..., :, None] - b[..., None, :] + ic[..., None, :], -jnp.inf)
        inter = b + m[..., None]
        m_t = jnp.maximum(inter, jnp.max(d, axis=-1))
        w_intra = jnp.exp(d - m_t[..., None])
        w_inter = jnp.exp(inter - m_t)
        s = jnp.einsum('bhtd,bhsd->bhts', qc, kc) * w_intra
        num = jnp.einsum('bhts,bhsv->bhtv', s, vc) + w_inter[..., None] * jnp.einsum('bhvd,bhtd->bhtv', C, qc)
        den = jnp.sum(s, axis=-1) + w_inter * jnp.einsum('bhd,bhtd->bht', n, qc)
        h = num / jnp.maximum(jnp.abs(den), jnp.exp(-m_t))[..., None]
        b_last = b[..., -1]
        g = b_last[..., None] - b + ic
        m_new = jnp.maximum(b_last + m, jnp.max(g, axis=-1))
        w_state = jnp.exp(g - m_new[..., None])
        decay = jnp.exp(b_last + m - m_new)
        C = decay[..., None, None] * C + jnp.einsum('bhs,bhsv,bhsd->bhvd', w_state, vc, kc)
        n = decay[..., None] * n + jnp.einsum('bhs,bhsd->bhd', w_state, kc)
        return (C, n, m_new), h

    state, h = lax.scan(step, state, (to_chunks(q), to_chunks(k), to_chunks(v), to_chunks(i_pre), to_chunks(logf)))
    return state, jnp.moveaxis(h, 0, 2).reshape(B, H, T, M_DV)


def flip_t(a):
    return jnp.flip(a, axis=2)


def mlstm_output(h, o_pre, g_mlstm):
    B, H, T = h.shape[0], h.shape[1], h.shape[2]
    hn = h * lax.rsqrt(jnp.mean(h * h, axis=-1, keepdims=True) + EPS)
    hn = hn.transpose(0, 2, 1, 3).reshape(B, T, M_WIDTH) * g_mlstm
    return (jax.nn.sigmoid(o_pre.astype(jnp.float32)) * hn).astype(o_pre.dtype)


def mlstm_bidirectional(lat, cx, g_mlstm, with_ctx):
    mq_l, mk_l, mv_l, mo_l, mi_l, mf_l = lat
    mq_c, mk_c, mv_c, mo_c, mi_c, mf_c = cx
    ql, kl, vl, il, fl = mlstm_heads(mq_l, mk_l, mv_l, mi_l, mf_l)
    qc, kc, vc, ic, fc = mlstm_heads(mq_c, mk_c, mv_c, mi_c, mf_c)
    B = ql.shape[0]
    zero = (jnp.zeros((B, M_HEADS, M_DV, M_DK), jnp.float32),
            jnp.zeros((B, M_HEADS, M_DK), jnp.float32),
            jnp.zeros((B, M_HEADS), jnp.float32))
    st_f, hc_f = mlstm_chunkwise(qc, kc, vc, ic[0], fc[0], zero)
    st_b, hc_b = mlstm_chunkwise(flip_t(qc), flip_t(kc), flip_t(vc), flip_t(ic[1]), flip_t(fc[1]), zero)
    _, hl_f = mlstm_chunkwise(ql, kl, vl, il[0], fl[0], st_f)
    _, hl_b = mlstm_chunkwise(flip_t(ql), flip_t(kl), flip_t(vl), flip_t(il[1]), flip_t(fl[1]), st_b)
    out_lat = mlstm_output(hl_f + flip_t(hl_b), mo_l, g_mlstm)
    out_ctx = mlstm_output(hc_f + flip_t(hc_b), mo_c, g_mlstm) if with_ctx else None
    return out_lat, out_ctx


def mixer_group(h_lat, h_ctx, w_in, b_in, g_q, g_k, g_mlstm, cos, sin, with_ctx):
    lat = split_columns(h_lat @ w_in + b_in)
    cx = split_columns(h_ctx @ w_in + b_in)
    q_l, k_l, v_l = attention_qkv(lat[0], lat[1], lat[2], g_q, g_k)
    q_l = apply_rope(q_l, cos, sin)
    k_l = apply_rope(k_l, cos, sin)
    q_c, k_c, v_c = attention_qkv(cx[0], cx[1], cx[2], g_q, g_k)
    a_lat = latent_attention(q_l, jnp.concatenate([k_l, k_c], axis=1), jnp.concatenate([v_l, v_c], axis=1))
    m_lat, m_ctx = mlstm_bidirectional(lat[3:], cx[3:], g_mlstm, with_ctx)
    y_lat = jnp.concatenate([a_lat, m_lat], axis=-1)
    y_ctx = None
    if with_ctx:
        B, C = q_c.shape[0], q_c.shape[1]
        a_ctx = dense_attention(q_c.reshape(B, C, KV_HEADS, GROUP, HEAD_DIM), k_c, v_c).reshape(B, C, ATT_WIDTH)
        y_ctx = jnp.concatenate([a_ctx, m_ctx], axis=-1)
    return y_lat, y_ctx


def moe(h, w_router, b_router, w1, b1, w2, b2):
    T, D = h.shape
    logits = (h @ w_router).astype(jnp.float32) + b_router
    top_val, top_idx = lax.top_k(logits, TOP_K)
    weights = jax.nn.softmax(top_val, axis=-1)
    A = T * TOP_K
    e_flat = top_idx.reshape(A)
    tok_flat = jnp.repeat(jnp.arange(T, dtype=jnp.int32), TOP_K)
    w_flat = weights.reshape(A)
    order = jnp.argsort(e_flat)
    e_sorted, tok_sorted, w_sorted = e_flat[order], tok_flat[order], w_flat[order]
    counts = jnp.zeros((N_EXPERTS,), jnp.int32).at[e_flat].add(1)
    padded = (counts + MOE_BLOCK - 1) // MOE_BLOCK * MOE_BLOCK
    start = jnp.cumsum(counts) - counts
    pstart = jnp.cumsum(padded) - padded
    pend = pstart + padded
    dest = pstart[e_sorted] + jnp.arange(A, dtype=jnp.int32) - start[e_sorted]
    nblk = -(-A // MOE_BLOCK) + N_EXPERTS
    P = nblk * MOE_BLOCK
    row_tok = jnp.full((P,), T, jnp.int32).at[dest].set(tok_sorted)
    row_w = jnp.zeros((P,), jnp.float32).at[dest].set(w_sorted)
    blk_start = jnp.arange(nblk, dtype=jnp.int32) * MOE_BLOCK
    blk_expert = jnp.minimum(jnp.sum(pend[None, :] <= blk_start[:, None], axis=1), N_EXPERTS - 1)
    h_pad = jnp.concatenate([h, jnp.zeros((1, D), h.dtype)], axis=0)

    def block(args):
        e, toks, wts = args
        xb = h_pad[toks]
        gu = xb @ w1[e] + b1[e]
        gate, up = gu[:, :D_FF], gu[:, D_FF:]
        gate = jnp.minimum(gate, SWIGLU_LIMIT)
        up = jnp.clip(up, -SWIGLU_LIMIT, SWIGLU_LIMIT)
        glu = gate * jax.nn.sigmoid(SWIGLU_ALPHA * gate)
        y = ((up + 1) * glu) @ w2[e] + b2[e]
        return y * wts[:, None].astype(y.dtype)

    y = lax.map(block, (blk_expert, row_tok.reshape(nblk, MOE_BLOCK), row_w.reshape(nblk, MOE_BLOCK)))
    out = jnp.zeros((T + 1, D), h.dtype).at[row_tok].add(y.reshape(P, D).astype(h.dtype))
    return out[:T]


def setup_inputs(seed: int = 0) -> dict:
    key = jax.random.key(seed)
    ks = jax.random.split(key, 22)
    nrm = jax.random.normal
    f32 = jnp.float32
    b_in = 0.02 * nrm(ks[8], (DEPTH, IN_COLS), f32)
    b_in = b_in.at[:, IN_COLS - 2 * M_HEADS:].add(FORGET_BIAS)
    return {
        'x': nrm(ks[0], (BATCH, SEQ, D_MODEL), f32),
        'c': nrm(ks[1], (BATCH, D_MODEL), f32),
        'ctx': nrm(ks[2], (BATCH, CTX_LEN, D_MODEL), f32),
        'c_ctx': nrm(ks[3], (D_MODEL,), f32),
        'w_mod': nrm(ks[4], (DEPTH, D_MODEL, N_ADALN * D_MODEL), f32) * (0.5 * D_MODEL ** -0.5),
        'b_mod': 0.02 * nrm(ks[5], (DEPTH, N_ADALN * D_MODEL), f32),
        'g_norm1': 1.0 + 0.05 * nrm(ks[6], (DEPTH, D_MODEL), f32),
        'w_in': nrm(ks[7], (DEPTH, D_MODEL, IN_COLS), f32) * (D_MODEL ** -0.5),
        'b_in': b_in,
        'g_q': 1.0 + 0.05 * nrm(ks[9], (DEPTH, HEAD_DIM), f32),
        'g_k': 1.0 + 0.05 * nrm(ks[10], (DEPTH, HEAD_DIM), f32),
        'g_mlstm': 1.0 + 0.05 * nrm(ks[11], (DEPTH, M_WIDTH), f32),
        'w_out': nrm(ks[12], (DEPTH, MIX_WIDTH, D_MODEL), f32) * (MIX_WIDTH ** -0.5),
        'g_norm2': 1.0 + 0.05 * nrm(ks[13], (DEPTH, D_MODEL), f32),
        'w_router': nrm(ks[14], (DEPTH, D_MODEL, N_EXPERTS), f32) * (D_MODEL ** -0.5),
        'b_router': 0.01 * nrm(ks[15], (DEPTH, N_EXPERTS), f32),
        'w1': nrm(ks[16], (DEPTH, N_EXPERTS, D_MODEL, 2 * D_FF), f32) * (D_MODEL ** -0.5),
        'b1': 0.02 * nrm(ks[17], (DEPTH, N_EXPERTS, 2 * D_FF), f32),
        'w2': nrm(ks[18], (DEPTH, N_EXPERTS, D_FF, D_MODEL), f32) * (D_FF ** -0.5),
        'b2': 0.02 * nrm(ks[19], (DEPTH, N_EXPERTS, D_MODEL), f32),
        'g_final': 1.0 + 0.05 * nrm(ks[20], (D_MODEL,), f32),
    }


def reference(x, c, ctx, c_ctx, w_mod, b_mod, g_norm1, w_in, b_in, g_q, g_k, g_mlstm, w_out,
              g_norm2, w_router, b_router, w1, b1, w2, b2, g_final):
    B, N, D = x.shape[0], x.shape[1], x.shape[2]
    rows = N // GRID_W
    cos, sin = axial_rope_tables(rows)
    for l in range(DEPTH):
        with_ctx = l + 1 < DEPTH
        mod = jax.nn.silu(c) @ w_mod[l] + b_mod[l]
        mod_c = jax.nn.silu(c_ctx) @ w_mod[l] + b_mod[l]
        sh1, sc1, gt1, sh2, sc2, gt2 = jnp.split(mod[:, None, :], N_ADALN, axis=-1)
        sh1c, sc1c, gt1c, sh2c, sc2c, gt2c = jnp.split(mod_c, N_ADALN, axis=-1)
        h_lat = modulate(rmsnorm(x, g_norm1[l]), sh1, sc1)
        h_ctx = modulate(rmsnorm(ctx, g_norm1[l]), sh1c, sc1c)
        y_lat, y_ctx = mixer_group(h_lat, h_ctx, w_in[l], b_in[l], g_q[l], g_k[l], g_mlstm[l], cos, sin, with_ctx)
        x = x + gt1 * (y_lat @ w_out[l])
        h2 = modulate(rmsnorm(x, g_norm2[l]), sh2, sc2)
        x = x + gt2 * moe(h2.reshape(B * N, D), w_router[l], b_router[l], w1[l], b1[l], w2[l], b2[l]).reshape(B, N, D)
        if with_ctx:
            C = ctx.shape[1]
            ctx = ctx + gt1c * (y_ctx @ w_out[l])
            h2c = modulate(rmsnorm(ctx, g_norm2[l]), sh2c, sc2c)
            ctx = ctx + gt2c * moe(h2c.reshape(B * C, D), w_router[l], b_router[l], w1[l], b1[l], w2[l], b2[l]).reshape(B, C, D)
    return rmsnorm(x, g_final)
```

```python
import jax
import jax.numpy as jnp
from jax import lax
from jax.experimental import pallas as pl
from jax.experimental.pallas import tpu as pltpu

F32 = jnp.float32
BF16 = jnp.bfloat16
HIGHEST = lax.Precision.HIGHEST

D_MODEL = 2048
BATCH = 2
SEQ = 4096
CTX_LEN = 256
GRID_W = 64
N_ADALN = 6
EPS = 1e-6
N_HEADS = 8
KV_HEADS = 2
GROUP = N_HEADS // KV_HEADS
HEAD_DIM = 128
AXIS_DIM = HEAD_DIM // 2
ROPE_THETA = 10000.0
ATT_WIDTH = N_HEADS * HEAD_DIM
ATT_SCALE = HEAD_DIM ** -0.5
M_HEADS = 4
M_DK = 128
M_DV = 256
M_CHUNK = 128
M_WIDTH = M_HEADS * M_DV
N_EXPERTS = 32
TOP_K = 4
D_FF = D_MODEL
SWIGLU_LIMIT = 7.0
SWIGLU_ALPHA = 1.702

T_LAT = BATCH * SEQ
T_CTX = BATCH * CTX_LEN
T_ALL = T_LAT + T_CTX
KV_W = KV_HEADS * HEAD_DIM
MQK_W = M_HEADS * M_DK
LANE = 128

C_Q = 0
C_K = C_Q + ATT_WIDTH
C_V = C_K + KV_W
C_MQ = C_V + KV_W
C_MK = C_MQ + MQK_W
C_MV = C_MK + MQK_W
C_MO = C_MV + M_WIDTH
C_G = C_MO + M_WIDTH
IN_COLS = C_G + 4 * M_HEADS
IN_COLS_PAD = C_G + LANE

VMEM_LIMIT = 56 * 1024 * 1024


def _params(n_axes):
    return pltpu.CompilerParams(
        dimension_semantics=("arbitrary",) * n_axes, vmem_limit_bytes=VMEM_LIMIT)


def _rms(x):
    return x * lax.rsqrt(jnp.mean(x * x, axis=-1, keepdims=True) + EPS)


MOD_ROWS = 8
MOD_TN = 1024


def _mod_kernel(c_ref, w_ref, b_ref, o_ref):
    c = c_ref[...]
    s = c / (1.0 + jnp.exp(-c))
    o_ref[...] = jnp.dot(s, w_ref[...], preferred_element_type=F32) + b_ref[...]


def _modulation(c_rows, w_mod, b_mod):
    n = w_mod.shape[1]
    return pl.pallas_call(
        _mod_kernel,
        grid=(n // MOD_TN,),
        in_specs=[
            pl.BlockSpec((MOD_ROWS, D_MODEL), lambda j: (0, 0)),
            pl.BlockSpec((D_MODEL, MOD_TN), lambda j: (0, j)),
            pl.BlockSpec((1, MOD_TN), lambda j: (0, j)),
        ],
        out_specs=pl.BlockSpec((MOD_ROWS, MOD_TN), lambda j: (0, j)),
        out_shape=jax.ShapeDtypeStruct((MOD_ROWS, n), F32),
        compiler_params=_params(1),
        name="adaln_mod",
    )(c_rows, w_mod, b_mod)


TM1 = 256
N_LAT_TILES1 = T_LAT // TM1
N_TILES1 = T_ALL // TM1
TILES_PER_SEQ1 = SEQ // TM1
PCH = 512


def _inproj_kernel(x_ref, ctx_ref, sh_ref, sc_ref, g1_ref, w_ref, b_ref, gq_ref, gk_ref,
                   cos_ref, sin_ref,
                   q_ref, k_ref, v_ref, mq_ref, mkt_ref, mv_ref, mo_ref, g_ref, h_scr):
    i = pl.program_id(0)
    row = jnp.where(i < N_LAT_TILES1, i // TILES_PER_SEQ1, BATCH)
    sc = sc_ref[pl.ds(row, 1), :]
    sh = sh_ref[pl.ds(row, 1), :]

    def norm_mod(xv):
        return (_rms(xv) * g1_ref[...]) * (1.0 + sc) + sh

    @pl.when(i < N_LAT_TILES1)
    def _():
        h_scr[...] = norm_mod(x_ref[...]).astype(BF16)

    @pl.when(i >= N_LAT_TILES1)
    def _():
        h_scr[...] = norm_mod(ctx_ref[...]).astype(BF16)

    h = h_scr[...]

    def proj(c0, c1):
        return jnp.dot(h, w_ref[:, c0:c1], preferred_element_type=F32) + b_ref[:, c0:c1]

    cos_f = cos_ref[...]
    sin_s = sin_ref[...]
    lane = lax.broadcasted_iota(jnp.int32, (TM1, HEAD_DIM), 1)
    even = (lane % 2) == 0

    def head_norm_rope(a, g):
        y = _rms(a) * g
        nxt = pltpu.roll(y, HEAD_DIM - 1, 1)
        prv = pltpu.roll(y, 1, 1)
        return y * cos_f + jnp.where(even, nxt, prv) * sin_s

    for c0 in range(C_Q, C_K, PCH):
        acc = proj(c0, c0 + PCH)
        for hh in range(PCH // HEAD_DIM):
            a = acc[:, hh * HEAD_DIM:(hh + 1) * HEAD_DIM]
            r = head_norm_rope(a, gq_ref[...]) * ATT_SCALE
            q_ref[:, c0 + hh * HEAD_DIM:c0 + (hh + 1) * HEAD_DIM] = r.astype(BF16)
    acc = proj(C_K, C_MQ)
    for hh in range(KV_HEADS):
        a = acc[:, hh * HEAD_DIM:(hh + 1) * HEAD_DIM]
        k_ref[:, hh * HEAD_DIM:(hh + 1) * HEAD_DIM] = head_norm_rope(a, gk_ref[...]).astype(BF16)
    v_ref[...] = acc[:, KV_W:].astype(BF16)
    mq_ref[...] = proj(C_MQ, C_MK).astype(BF16)
    mk = proj(C_MK, C_MV) * (M_DK ** -0.5)
    mkt_ref[...] = mk.T.astype(BF16)
    for c0 in range(C_MV, C_MO, PCH):
        mv_ref[:, c0 - C_MV:c0 - C_MV + PCH] = proj(c0, c0 + PCH).astype(BF16)
    for c0 in range(C_MO, C_G, PCH):
        mo_ref[:, c0 - C_MO:c0 - C_MO + PCH] = proj(c0, c0 + PCH)
    g_ref[...] = proj(C_G, IN_COLS_PAD)


def _in_projection(x2, ctx2, sh1, sc1, g1, w_in_p, b_in_p, g_q, g_k, cos_t, sin_t):
    lat_idx = lambda i: (jnp.minimum(i, N_LAT_TILES1 - 1), 0)
    ctx_idx = lambda i: (jnp.maximum(i - N_LAT_TILES1, 0), 0)
    rope_idx = lambda i: (jnp.where(i < N_LAT_TILES1, i % TILES_PER_SEQ1, TILES_PER_SEQ1), 0)
    full = lambda shape: pl.BlockSpec(shape, lambda i: (0, 0))
    row_blk = lambda w: pl.BlockSpec((TM1, w), lambda i: (i, 0))
    out_shapes = [
        jax.ShapeDtypeStruct((T_ALL, ATT_WIDTH), BF16),
        jax.ShapeDtypeStruct((T_ALL, KV_W), BF16),
        jax.ShapeDtypeStruct((T_ALL, KV_W), BF16),
        jax.ShapeDtypeStruct((T_ALL, MQK_W), BF16),
        jax.ShapeDtypeStruct((MQK_W, T_ALL), BF16),
        jax.ShapeDtypeStruct((T_ALL, M_WIDTH), BF16),
        jax.ShapeDtypeStruct((T_ALL, M_WIDTH), F32),
        jax.ShapeDtypeStruct((T_ALL, LANE), F32),
    ]
    out_specs = [
        row_blk(ATT_WIDTH), row_blk(KV_W), row_blk(KV_W), row_blk(MQK_W),
        pl.BlockSpec((MQK_W, TM1), lambda i: (0, i)),
        row_blk(M_WIDTH), row_blk(M_WIDTH), row_blk(LANE),
    ]
    return pl.pallas_call(
        _inproj_kernel,
        grid=(N_TILES1,),
        in_specs=[
            pl.BlockSpec((TM1, D_MODEL), lat_idx),
            pl.BlockSpec((TM1, D_MODEL), ctx_idx),
            full((MOD_ROWS, D_MODEL)), full((MOD_ROWS, D_MODEL)), full((1, D_MODEL)),
            full((D_MODEL, IN_COLS_PAD)), full((1, IN_COLS_PAD)),
            full((1, HEAD_DIM)), full((1, HEAD_DIM)),
            pl.BlockSpec((TM1, HEAD_DIM), rope_idx),
            pl.BlockSpec((TM1, HEAD_DIM), rope_idx),
        ],
        out_specs=out_specs,
        out_shape=out_shapes,
        scratch_shapes=[pltpu.VMEM((TM1, D_MODEL), BF16)],
        compiler_params=_params(1),
        name="in_proj",
    )(x2, ctx2, sh1, sc1, g1, w_in_p, b_in_p, g_q, g_k, cos_t, sin_t)


TQ = 256
CK = 512
GQ_W = GROUP * HEAD_DIM


def _attn_kernel(q_ref, kl_ref, vl_ref, kc_ref, vc_ref, o_ref):
    chunks = [(kl_ref, vl_ref, c * CK, CK) for c in range(SEQ // CK)]
    chunks.append((kc_ref, vc_ref, 0, CTX_LEN))
    for g in range(GROUP):
        qh = q_ref[:, g * HEAD_DIM:(g + 1) * HEAD_DIM]
        m = jnp.full((TQ, 1), -jnp.inf, F32)
        l = jnp.zeros((TQ, 1), F32)
        acc = jnp.zeros((TQ, HEAD_DIM), F32)
        for kr, vr, st, sz in chunks:
            s = lax.dot_general(qh, kr[st:st + sz, :], (((1,), (1,)), ((), ())),
                                preferred_element_type=F32)
            m_new = jnp.maximum(m, jnp.max(s, axis=-1, keepdims=True))
            p = jnp.exp(s - m_new)
            alpha = jnp.exp(m - m_new)
            l = alpha * l + jnp.sum(p, axis=-1, keepdims=True)
            acc = alpha * acc + jnp.dot(p.astype(BF16), vr[st:st + sz, :],
                                        preferred_element_type=F32)
            m = m_new
        o_ref[:, g * HEAD_DIM:(g + 1) * HEAD_DIM] = (acc / l).astype(BF16)


def _attention(q, k, v):
    nq = SEQ // TQ
    ctx_blk0 = T_LAT // CTX_LEN
    lat_kv = pl.BlockSpec((SEQ, HEAD_DIM), lambda b, h, i: (b, h))
    ctx_kv = pl.BlockSpec((CTX_LEN, HEAD_DIM), lambda b, h, i: (ctx_blk0 + b, h))
    q_blk = pl.BlockSpec((TQ, GQ_W), lambda b, h, i: (b * nq + i, h))
    return pl.pallas_call(
        _attn_kernel,
        grid=(BATCH, KV_HEADS, nq),
        in_specs=[q_blk, lat_kv, lat_kv, ctx_kv, ctx_kv],
        out_specs=q_blk,
        out_shape=jax.ShapeDtypeStruct((T_LAT, ATT_WIDTH), BF16),
        compiler_params=_params(3),
        name="gqa_attention",
    )(q, k, v, k, v)


N_CHUNKS = SEQ // M_CHUNK
CTX_CHUNKS = CTX_LEN // M_CHUNK
N_CHAINS = 2 * M_HEADS


def _log_sigmoid(x):
    return jnp.minimum(x, 0.0) - jnp.log(1.0 + jnp.exp(-jnp.abs(x)))


def _mlstm_kernel(qf_ref, qb_ref, qc_ref, kf_ref, kb_ref, kc_ref, vf_ref, vb_ref, vc_ref,
                  gf_ref, gb_ref, gc_ref, hf_ref, hb_ref, c_scr, n_scr, m_scr):
    j = pl.program_id(1)
    ri = lax.broadcasted_iota(jnp.int32, (M_CHUNK, M_CHUNK), 0)
    ci = lax.broadcasted_iota(jnp.int32, (M_CHUNK, M_CHUNK), 1)
    lower = ci <= ri
    upper = ci >= ri
    tri_l = lower.astype(F32)
    tri_u = upper.astype(F32)

    def gate_tables(g, rev):
        gt = g.T
        mcol, mrow = (tri_u, tri_l) if rev else (tri_l, tri_u)
        bcol = jnp.dot(mcol, _log_sigmoid(g), precision=HIGHEST, preferred_element_type=F32)
        brow = jnp.dot(_log_sigmoid(gt), mrow, precision=HIGHEST, preferred_element_type=F32)
        return gt, bcol, brow

    def chain_step(chain, q, kt, v, tabs, rev, h_out):
        gt, bcol, brow = tabs
        icol = chain
        fcol = N_CHAINS + chain
        last = 0 if rev else M_CHUNK - 1
        b_col = bcol[:, fcol:fcol + 1]
        b_row = brow[fcol:fcol + 1, :]
        i_row = gt[icol:icol + 1, :]
        tot = brow[fcol:fcol + 1, last:last + 1]
        m_old = m_scr[chain][:1, :1]
        c_old = c_scr[chain]
        n_old = n_scr[chain]
        if h_out is not None:
            d = jnp.where(upper if rev else lower, b_col - b_row + i_row, -jnp.inf)
            inter = b_col + m_old
            m_t = jnp.maximum(inter, jnp.max(d, axis=-1, keepdims=True))
            w_intra = jnp.exp(d - m_t)
            w_inter = jnp.exp(inter - m_t)
            s = jnp.dot(q, kt, preferred_element_type=F32) * w_intra
            qc = jnp.dot(q, c_old.astype(BF16), preferred_element_type=F32)
            qn = jnp.dot(q, n_old.astype(BF16), preferred_element_type=F32)[:, :1]
            num = jnp.dot(s.astype(BF16), v, preferred_element_type=F32) + w_inter * qc
            den = jnp.sum(s, axis=-1, keepdims=True) + w_inter * qn
            h_out(num / jnp.maximum(jnp.abs(den), jnp.exp(-m_t)))
        g_row = tot - b_row + i_row
        m_new = jnp.maximum(tot + m_old, jnp.max(g_row, axis=-1, keepdims=True))
        w_state = jnp.exp(g_row - m_new)
        decay = jnp.exp(tot + m_old - m_new)
        kw = kt.astype(F32) * w_state
        c_scr[chain] = decay * c_old + jnp.dot(kw.astype(BF16), v, preferred_element_type=F32)
        n_new = decay * n_old[:, :1] + jnp.sum(kw, axis=-1, keepdims=True)
        n_scr[chain] = jnp.broadcast_to(n_new, (M_DK, LANE))
        m_scr[chain] = jnp.broadcast_to(m_new, (8, LANE))

    def run_chunk(q_ref, kt_ref, v_ref, g_ref, r0, rev, h_ref):
        tabs = gate_tables(g_ref[r0:r0 + M_CHUNK, :], rev)
        for hd in range(M_HEADS):
            chain = (M_HEADS if rev else 0) + hd
            q = q_ref[r0:r0 + M_CHUNK, hd * M_DK:(hd + 1) * M_DK]
            kt = kt_ref[hd * M_DK:(hd + 1) * M_DK, r0:r0 + M_CHUNK]
            v = v_ref[r0:r0 + M_CHUNK, hd * M_DV:(hd + 1) * M_DV]
            if h_ref is None:
                h_out = None
            else:
                def h_out(hv, hd=hd):
                    h_ref[:, hd * M_DV:(hd + 1) * M_DV] = hv
            chain_step(chain, q, kt, v, tabs, rev, h_out)

    @pl.when(j == 0)
    def _():
        c_scr[...] = jnp.zeros_like(c_scr)
        n_scr[...] = jnp.zeros_like(n_scr)
        m_scr[...] = jnp.zeros_like(m_scr)
        for cc in range(CTX_CHUNKS):
            run_chunk(qc_ref, kc_ref, vc_ref, gc_ref, cc * M_CHUNK, False, None)
            run_chunk(qc_ref, kc_ref, vc_ref, gc_ref, (CTX_CHUNKS - 1 - cc) * M_CHUNK, True, None)

    run_chunk(qf_ref, kf_ref, vf_ref, gf_ref, 0, False, hf_ref)
    run_chunk(qb_ref, kb_ref, vb_ref, gb_ref, 0, True, hb_ref)


def _mlstm(mq, mkt, mv, gates):
    ctx_blk0 = T_LAT // CTX_LEN
    fwd = lambda b, j: b * N_CHUNKS + j
    bwd = lambda b, j: b * N_CHUNKS + N_CHUNKS - 1 - j
    ctx = lambda b, j: ctx_blk0 + b

    def rows(width, tile, idx):
        return pl.BlockSpec((tile, width), lambda b, j: (idx(b, j), 0))

    def cols(tile, idx):
        return pl.BlockSpec((MQK_W, tile), lambda b, j: (0, idx(b, j)))

    h_shape = jax.ShapeDtypeStruct((T_LAT, M_WIDTH), F32)
    return pl.pallas_call(
        _mlstm_kernel,
        grid=(BATCH, N_CHUNKS),
        in_specs=[
            rows(MQK_W, M_CHUNK, fwd), rows(MQK_W, M_CHUNK, bwd), rows(MQK_W, CTX_LEN, ctx),
            cols(M_CHUNK, fwd), cols(M_CHUNK, bwd), cols(CTX_LEN, ctx),
            rows(M_WIDTH, M_CHUNK, fwd), rows(M_WIDTH, M_CHUNK, bwd), rows(M_WIDTH, CTX_LEN, ctx),
            rows(LANE, M_CHUNK, fwd), rows(LANE, M_CHUNK, bwd), rows(LANE, CTX_LEN, ctx),
        ],
        out_specs=[rows(M_WIDTH, M_CHUNK, fwd), rows(M_WIDTH, M_CHUNK, bwd)],
        out_shape=[h_shape, h_shape],
        scratch_shapes=[
            pltpu.VMEM((N_CHAINS, M_DK, M_DV), F32),
            pltpu.VMEM((N_CHAINS, M_DK, LANE), F32),
            pltpu.VMEM((N_CHAINS, 8, LANE), F32),
        ],
        compiler_params=_params(2),
        name="mlstm_scan",
    )(mq, mq, mq, mkt, mkt, mkt, mv, mv, mv, gates, gates, gates)


TM4 = 256
TILES_PER_SEQ4 = SEQ // TM4
NEG_BIG = -1e30


def _outproj_kernel(a_ref, hf_ref, hb_ref, mo_ref, x_ref, w_ref, gm_ref, gt1_ref, sh2_ref,
                    sc2_ref, g2_ref, wr_ref, br_ref,
                    x1_ref, h2_ref, ti_ref, tw_ref):
    i = pl.program_id(0)
    row = i // TILES_PER_SEQ4
    acc = jnp.dot(a_ref[...], w_ref[0:ATT_WIDTH, :], preferred_element_type=F32)
    for hd in range(M_HEADS):
        sl = slice(hd * M_DV, (hd + 1) * M_DV)
        hn = _rms(hf_ref[:, sl] + hb_ref[:, sl]) * gm_ref[:, sl]
        mo = mo_ref[:, sl]
        ym = hn / (1.0 + jnp.exp(-mo))
        acc += jnp.dot(ym.astype(BF16), w_ref[ATT_WIDTH + hd * M_DV:ATT_WIDTH + (hd + 1) * M_DV, :],
                       preferred_element_type=F32)
    x1 = x_ref[...] + gt1_ref[pl.ds(row, 1), :] * acc
    x1_ref[...] = x1
    h2 = (_rms(x1) * g2_ref[...]) * (1.0 + sc2_ref[pl.ds(row, 1), :]) + sh2_ref[pl.ds(row, 1), :]
    h2_ref[...] = h2
    logits = jnp.dot(h2, wr_ref[...], precision=HIGHEST, preferred_element_type=F32) + br_ref[...]
    lane = lax.broadcasted_iota(jnp.int32, (TM4, LANE), 1)
    vals, idxs = [], []
    for _ in range(TOP_K):
        mx = jnp.max(logits, axis=-1, keepdims=True)
        ik = jnp.min(jnp.where(logits == mx, lane, LANE), axis=-1, keepdims=True)
        vals.append(mx)
        idxs.append(ik)
        logits = jnp.where(lane == ik, -jnp.inf, logits)
    es = [jnp.exp(vk - vals[0]) for vk in vals]
    tot = es[0] + es[1] + es[2] + es[3]
    ti = jnp.zeros((TM4, LANE), jnp.int32)
    tw = jnp.zeros((TM4, LANE), F32)
    for kk in range(TOP_K):
        ti = jnp.where(lane == kk, idxs[kk], ti)
        tw = jnp.where(lane == kk, es[kk] / tot, tw)
    ti_ref[...] = ti
    tw_ref[...] = tw


def _out_projection(a_lat, hf, hb, mo, x2, w_out_b, g_mlstm, gt1, sh2, sc2, g2, wr_p, br_p):
    full = lambda shape: pl.BlockSpec(shape, lambda i: (0, 0))
    row_blk = lambda w: pl.BlockSpec((TM4, w), lambda i: (i, 0))
    return pl.pallas_call(
        _outproj_kernel,
        grid=(T_LAT // TM4,),
        in_specs=[
            row_blk(ATT_WIDTH), row_blk(M_WIDTH), row_blk(M_WIDTH), row_blk(M_WIDTH),
            row_blk(D_MODEL), full((ATT_WIDTH + M_WIDTH, D_MODEL)), full((1, M_WIDTH)),
            full((MOD_ROWS, D_MODEL)), full((MOD_ROWS, D_MODEL)), full((MOD_ROWS, D_MODEL)),
            full((1, D_MODEL)), full((D_MODEL, LANE)), full((1, LANE)),
        ],
        out_specs=[row_blk(D_MODEL), row_blk(D_MODEL), row_blk(LANE), row_blk(LANE)],
        out_shape=[
            jax.ShapeDtypeStruct((T_LAT, D_MODEL), F32),
            jax.ShapeDtypeStruct((T_LAT, D_MODEL), F32),
            jax.ShapeDtypeStruct((T_LAT, LANE), jnp.int32),
            jax.ShapeDtypeStruct((T_LAT, LANE), F32),
        ],
        compiler_params=_params(1),
        name="out_proj_router",
    )(a_lat, hf, hb, mo, x2, w_out_b, g_mlstm, gt1, sh2, sc2, g2, wr_p, br_p)


MB = 128
N_ASSIGN = T_LAT * TOP_K
N_BLOCKS = N_ASSIGN // MB + N_EXPERTS
P_ROWS = N_BLOCKS * MB
SB = 10
W_MAX = N_EXPERTS + N_BLOCKS // SB
TF = 256
NF = D_FF // TF
XROWS = SB * MB


def _moe_kernel(we_ref, wb_ref, wn_ref, tok_ref, dst_ref,
                h2_hbm, w1g_ref, w1u_ref, w2_ref, b1g_ref, b1u_ref, b2_ref,
                y_hbm, xbuf, acc, gsem, ssem):
    w = pl.program_id(0)
    f = pl.program_id(1)
    nsub = wn_ref[w]
    r0 = wb_ref[w] * MB

    @pl.when(jnp.logical_and(f == 0, nsub > 0))
    def _gather():
        def issue(r, carry):
            tok = tok_ref[r0 + r]
            pltpu.make_async_copy(h2_hbm.at[pl.ds(tok, 1)], xbuf.at[pl.ds(r, 1)], gsem).start()
            return carry
        lax.fori_loop(0, nsub * MB, issue, 0)

        def wait_row(r, carry):
            pltpu.make_async_copy(h2_hbm.at[pl.ds(0, 1)], xbuf.at[pl.ds(r, 1)], gsem).wait()
            return carry
        lax.fori_loop(0, nsub * MB, wait_row, 0)

    @pl.when(nsub > 0)
    def _compute():
        def sub(s, carry):
            rs = pl.ds(pl.multiple_of(s * MB, MB), MB)
            xb = xbuf[rs, :]
            gate = jnp.dot(xb, w1g_ref[...], preferred_element_type=F32) + b1g_ref[...]
            up = jnp.dot(xb, w1u_ref[...], preferred_element_type=F32) + b1u_ref[...]
            gate = jnp.minimum(gate, SWIGLU_LIMIT)
            up = jnp.clip(up, -SWIGLU_LIMIT, SWIGLU_LIMIT)
            glu = gate / (1.0 + jnp.exp(-SWIGLU_ALPHA * gate))
            contrib = jnp.dot((up + 1.0) * glu, w2_ref[...], preferred_element_type=F32)

            @pl.when(f == 0)
            def _():
                acc[rs, :] = contrib + b2_ref[...]

            @pl.when(f > 0)
            def _():
                acc[rs, :] += contrib
            return carry
        lax.fori_loop(0, nsub, sub, 0)

    @pl.when(jnp.logical_and(f == NF - 1, nsub > 0))
    def _scatter():
        def issue(r, carry):
            dst = dst_ref[r0 + r]

            @pl.when(dst < N_ASSIGN)
            def _():
                pltpu.make_async_copy(acc.at[pl.ds(r, 1)], y_hbm.at[pl.ds(dst, 1)], ssem).start()
            return carry
        lax.fori_loop(0, nsub * MB, issue, 0)

        def wait_row(r, carry):
            dst = dst_ref[r0 + r]

            @pl.when(dst < N_ASSIGN)
            def _():
                pltpu.make_async_copy(acc.at[pl.ds(r, 1)], y_hbm.at[pl.ds(dst, 1)], ssem).wait()
            return carry
        lax.fori_loop(0, nsub * MB, wait_row, 0)


def _moe(h2, w1, b1, w2, b2, work_e, work_blk, work_n, row_tok, row_dst):
    b1r = b1.reshape(N_EXPERTS, 1, 2 * D_FF)
    b2r = b2.reshape(N_EXPERTS, 1, D_MODEL)
    grid_spec = pltpu.PrefetchScalarGridSpec(
        num_scalar_prefetch=5,
        grid=(W_MAX, NF),
        in_specs=[
            pl.BlockSpec(memory_space=pl.ANY),
            pl.BlockSpec((None, D_MODEL, TF), lambda w, f, we, *_: (we[w], 0, f)),
            pl.BlockSpec((None, D_MODEL, TF), lambda w, f, we, *_: (we[w], 0, NF + f)),
            pl.BlockSpec((None, TF, D_MODEL), lambda w, f, we, *_: (we[w], f, 0)),
            pl.BlockSpec((None, 1, TF), lambda w, f, we, *_: (we[w], 0, f)),
            pl.BlockSpec((None, 1, TF), lambda w, f, we, *_: (we[w], 0, NF + f)),
            pl.BlockSpec((None, 1, D_MODEL), lambda w, f, we, *_: (we[w], 0, 0)),
        ],
        out_specs=pl.BlockSpec(memory_space=pl.ANY),
        scratch_shapes=[
            pltpu.VMEM((XROWS, D_MODEL), F32),
            pltpu.VMEM((XROWS, D_MODEL), F32),
            pltpu.SemaphoreType.DMA(()),
            pltpu.SemaphoreType.DMA(()),
        ],
    )
    return pl.pallas_call(
        _moe_kernel,
        grid_spec=grid_spec,
        out_shape=jax.ShapeDtypeStruct((N_ASSIGN, D_MODEL), F32),
        compiler_params=_params(2),
        name="moe_experts",
    )(work_e, work_blk, work_n, row_tok, row_dst, h2, w1, w1, w2, b1r, b1r, b2r)


def _routing_tables(top_idx):
    e_flat = top_idx.reshape(N_ASSIGN)
    onehot = (e_flat[:, None] == jnp.arange(N_EXPERTS, dtype=jnp.int32)[None, :]).astype(jnp.int32)
    csum = jnp.cumsum(onehot, axis=0)
    rank = jnp.sum(csum * onehot, axis=1) - 1
    counts = csum[-1]
    nblk = (counts + MB - 1) // MB
    blk_start = jnp.cumsum(nblk) - nblk
    dest = blk_start[e_flat] * MB + rank
    a_ids = jnp.arange(N_ASSIGN, dtype=jnp.int32)
    row_tok = jnp.zeros((P_ROWS,), jnp.int32).at[dest].set(a_ids // TOP_K)
    row_dst = jnp.full((P_ROWS,), N_ASSIGN, jnp.int32).at[dest].set(a_ids)
    nwork = (nblk + SB - 1) // SB
    wend = jnp.cumsum(nwork)
    wstart = wend - nwork
    n_items = wend[-1]
    wid = jnp.arange(W_MAX, dtype=jnp.int32)
    valid = wid < n_items
    we = jnp.minimum(jnp.sum((wend[None, :] <= wid[:, None]).astype(jnp.int32), axis=1),
                     N_EXPERTS - 1)
    local = wid - wstart[we]
    work_blk = jnp.where(valid, blk_start[we] + local * SB, 0)
    work_n = jnp.where(valid, jnp.clip(nblk[we] - local * SB, 0, SB), 0)
    last_e = we[jnp.maximum(n_items - 1, 0)]
    work_e = jnp.where(valid, we, last_e)
    return (work_e.astype(jnp.int32), work_blk.astype(jnp.int32), work_n.astype(jnp.int32),
            row_tok, row_dst)


TM7 = 256
TILES_PER_SEQ7 = SEQ // TM7


def _final_kernel(x1_ref, y_ref, tw_ref, gt2_ref, gf_ref, o_ref):
    i = pl.program_id(0)
    row = i // TILES_PER_SEQ7
    tw = tw_ref[...]
    mix = jnp.zeros((TM7, D_MODEL), F32)
    for kk in range(TOP_K):
        mix += y_ref[:, kk * D_MODEL:(kk + 1) * D_MODEL] * tw[:, kk:kk + 1]
    xo = x1_ref[...] + gt2_ref[pl.ds(row, 1), :] * mix
    o_ref[...] = _rms(xo) * gf_ref[...]


def _final(x1, y4, top_w, gt2, g_final):
    return pl.pallas_call(
        _final_kernel,
        grid=(T_LAT // TM7,),
        in_specs=[
            pl.BlockSpec((TM7, D_MODEL), lambda i: (i, 0)),
            pl.BlockSpec((TM7, TOP_K * D_MODEL), lambda i: (i, 0)),
            pl.BlockSpec((TM7, LANE), lambda i: (i, 0)),
            pl.BlockSpec((MOD_ROWS, D_MODEL), lambda i: (0, 0)),
            pl.BlockSpec((1, D_MODEL), lambda i: (0, 0)),
        ],
        out_specs=pl.BlockSpec((TM7, D_MODEL), lambda i: (i, 0)),
        out_shape=jax.ShapeDtypeStruct((T_LAT, D_MODEL), F32),
        compiler_params=_params(1),
        name="combine_final_norm",
    )(x1, y4, top_w, gt2, g_final)


def _rope_tables():
    rows = SEQ // GRID_W
    row = jnp.repeat(jnp.arange(rows, dtype=F32), GRID_W)
    col = jnp.tile(jnp.arange(GRID_W, dtype=F32), rows)
    inv = ROPE_THETA ** (-jnp.arange(0, AXIS_DIM, 2, dtype=F32) / AXIS_DIM)
    ang = jnp.concatenate([row[:, None] * inv, col[:, None] * inv], axis=-1)
    cos = jnp.repeat(jnp.cos(ang), 2, axis=-1)
    sin = jnp.repeat(jnp.sin(ang), 2, axis=-1)
    sign = jnp.tile(jnp.array([-1.0, 1.0], F32), HEAD_DIM // 2)
    cos_t = jnp.concatenate([cos, jnp.ones((TM1, HEAD_DIM), F32)], axis=0)
    sin_t = jnp.concatenate([sin * sign, jnp.zeros((TM1, HEAD_DIM), F32)], axis=0)
    return cos_t, sin_t


def kernel(x, c, ctx, c_ctx, w_mod, b_mod, g_norm1, w_in, b_in, g_q, g_k, g_mlstm, w_out,
           g_norm2, w_router, b_router, w1, b1, w2, b2, g_final):
    x2 = x.reshape(T_LAT, D_MODEL)
    ctx2 = ctx.reshape(T_CTX, D_MODEL)
    cos_t, sin_t = _rope_tables()
    assert w_mod.shape[0] == 1
    l = 0
    c_rows = jnp.concatenate(
        [c, c_ctx[None, :], jnp.zeros((MOD_ROWS - BATCH - 1, D_MODEL), F32)], axis=0)
    mod = _modulation(c_rows, w_mod[l], b_mod[l][None, :])
    sh1, sc1, gt1, sh2, sc2, gt2 = [mod[:, k * D_MODEL:(k + 1) * D_MODEL]
                                    for k in range(N_ADALN)]
    pad = IN_COLS_PAD - IN_COLS
    w_in_p = jnp.pad(w_in[l], ((0, 0), (0, pad))).astype(BF16)
    b_in_p = jnp.pad(b_in[l], (0, pad))[None, :]
    q, k, v, mq, mkt, mv, mo, gates = _in_projection(
        x2, ctx2, sh1, sc1, g_norm1[l][None, :], w_in_p, b_in_p,
        g_q[l][None, :], g_k[l][None, :], cos_t, sin_t)
    a_lat = _attention(q, k, v)
    hf, hb = _mlstm(mq, mkt, mv, gates)
    wr_p = jnp.pad(w_router[l], ((0, 0), (0, LANE - N_EXPERTS)))
    br_p = jnp.pad(b_router[l], (0, LANE - N_EXPERTS), constant_values=NEG_BIG)[None, :]
    x1, h2, top_i, top_w = _out_projection(
        a_lat, hf, hb, mo, x2, w_out[l].astype(BF16), g_mlstm[l][None, :],
        gt1, sh2, sc2, g_norm2[l][None, :], wr_p, br_p)
    tables = _routing_tables(top_i[:, :TOP_K])
    y4 = _moe(h2, w1[l], b1[l], w2[l], b2[l], *tables)
    out = _final(x1, y4.reshape(T_LAT, TOP_K * D_MODEL), top_w, gt2, g_final[None, :])
    return out.reshape(BATCH, SEQ, D_MODEL)
```

```python
import jax
import jax.numpy as jnp
from jax import lax
from jax.experimental import pallas as pl
from jax.experimental.pallas import tpu as pltpu

F32 = jnp.float32
BF16 = jnp.bfloat16
HIGHEST = lax.Precision.HIGHEST

D_MODEL = 2048
BATCH = 2
SEQ = 4096
CTX_LEN = 256
GRID_W = 64
N_ADALN = 6
EPS = 1e-6
N_HEADS = 8
KV_HEADS = 2
GROUP = N_HEADS // KV_HEADS
HEAD_DIM = 128
AXIS_DIM = HEAD_DIM // 2
ROPE_THETA = 10000.0
ATT_WIDTH = N_HEADS * HEAD_DIM
ATT_SCALE = HEAD_DIM ** -0.5
M_HEADS = 4
M_DK = 128
M_DV = 256
M_CHUNK = 128
M_WIDTH = M_HEADS * M_DV
N_EXPERTS = 32
TOP_K = 4
D_FF = D_MODEL
SWIGLU_LIMIT = 7.0
SWIGLU_ALPHA = 1.702

T_LAT = BATCH * SEQ
T_CTX = BATCH * CTX_LEN
T_ALL = T_LAT + T_CTX
KV_W = KV_HEADS * HEAD_DIM
MQK_W = M_HEADS * M_DK
LANE = 128

C_Q = 0
C_K = C_Q + ATT_WIDTH
C_V = C_K + KV_W
C_MQ = C_V + KV_W
C_MK = C_MQ + MQK_W
C_MV = C_MK + MQK_W
C_MO = C_MV + M_WIDTH
C_G = C_MO + M_WIDTH
IN_COLS = C_G + 4 * M_HEADS
IN_COLS_PAD = C_G + LANE

VMEM_LIMIT = 56 * 1024 * 1024


def _params(n_axes):
    return pltpu.CompilerParams(
        dimension_semantics=("arbitrary",) * n_axes, vmem_limit_bytes=VMEM_LIMIT)


def _rms(x):
    return x * lax.rsqrt(jnp.mean(x * x, axis=-1, keepdims=True) + EPS)


MOD_ROWS = 8
MOD_TN = 1024


def _mod_kernel(c_ref, w_ref, b_ref, o_ref):
    c = c_ref[...]
    s = c / (1.0 + jnp.exp(-c))
    o_ref[...] = jnp.dot(s, w_ref[...], preferred_element_type=F32) + b_ref[...]


def _modulation(c_rows, w_mod, b_mod):
    n = w_mod.shape[1]
    return pl.pallas_call(
        _mod_kernel,
        grid=(n // MOD_TN,),
        in_specs=[
            pl.BlockSpec((MOD_ROWS, D_MODEL), lambda j: (0, 0)),
            pl.BlockSpec((D_MODEL, MOD_TN), lambda j: (0, j)),
            pl.BlockSpec((1, MOD_TN), lambda j: (0, j)),
        ],
        out_specs=pl.BlockSpec((MOD_ROWS, MOD_TN), lambda j: (0, j)),
        out_shape=jax.ShapeDtypeStruct((MOD_ROWS, n), F32),
        compiler_params=_params(1),
        name="adaln_mod",
    )(c_rows, w_mod, b_mod)


TM1 = 256
N_LAT_TILES1 = T_LAT // TM1
N_TILES1 = T_ALL // TM1
TILES_PER_SEQ1 = SEQ // TM1
PCH = 512


def _inproj_kernel(x_ref, ctx_ref, sh_ref, sc_ref, g1_ref, w_ref, b_ref, gq_ref, gk_ref,
                   cos_ref, sin_ref,
                   q_ref, k_ref, v_ref, mq_ref, mkt_ref, mv_ref, mo_ref, g_ref, h_scr):
    i = pl.program_id(0)
    row = jnp.where(i < N_LAT_TILES1, i // TILES_PER_SEQ1, BATCH)
    sc = sc_ref[pl.ds(row, 1), :]
    sh = sh_ref[pl.ds(row, 1), :]

    def norm_mod(xv):
        return (_rms(xv) * g1_ref[...]) * (1.0 + sc) + sh

    @pl.when(i < N_LAT_TILES1)
    def _():
        h_scr[...] = norm_mod(x_ref[...]).astype(BF16)

    @pl.when(i >= N_LAT_TILES1)
    def _():
        h_scr[...] = norm_mod(ctx_ref[...]).astype(BF16)

    h = h_scr[...]

    def proj(c0, c1):
        return jnp.dot(h, w_ref[:, c0:c1], preferred_element_type=F32) + b_ref[:, c0:c1]

    cos_f = cos_ref[...]
    sin_s = sin_ref[...]
    lane = lax.broadcasted_iota(jnp.int32, (TM1, HEAD_DIM), 1)
    even = (lane % 2) == 0

    def head_norm_rope(a, g):
        y = _rms(a) * g
        nxt = pltpu.roll(y, HEAD_DIM - 1, 1)
        prv = pltpu.roll(y, 1, 1)
        return y * cos_f + jnp.where(even, nxt, prv) * sin_s

    for c0 in range(C_Q, C_K, PCH):
        acc = proj(c0, c0 + PCH)
        for hh in range(PCH // HEAD_DIM):
            a = acc[:, hh * HEAD_DIM:(hh + 1) * HEAD_DIM]
            r = head_norm_rope(a, gq_ref[...]) * ATT_SCALE
            q_ref[:, c0 + hh * HEAD_DIM:c0 + (hh + 1) * HEAD_DIM] = r.astype(BF16)
    acc = proj(C_K, C_MQ)
    for hh in range(KV_HEADS):
        a = acc[:, hh * HEAD_DIM:(hh + 1) * HEAD_DIM]
        k_ref[:, hh * HEAD_DIM:(hh + 1) * HEAD_DIM] = head_norm_rope(a, gk_ref[...]).astype(BF16)
    v_ref[...] = acc[:, KV_W:].astype(BF16)
    mq_ref[...] = proj(C_MQ, C_MK).astype(BF16)
    mk = proj(C_MK, C_MV) * (M_DK ** -0.5)
    mkt_ref[...] = mk.T.astype(BF16)
    for c0 in range(C_MV, C_MO, PCH):
        mv_ref[:, c0 - C_MV:c0 - C_MV + PCH] = proj(c0, c0 + PCH).astype(BF16)
    for c0 in range(C_MO, C_G, PCH):
        mo_ref[:, c0 - C_MO:c0 - C_MO + PCH] = proj(c0, c0 + PCH)
    g_ref[...] = proj(C_G, IN_COLS_PAD)


def _in_projection(x2, ctx2, sh1, sc1, g1, w_in_p, b_in_p, g_q, g_k, cos_t, sin_t):
    lat_idx = lambda i: (jnp.minimum(i, N_LAT_TILES1 - 1), 0)
    ctx_idx = lambda i: (jnp.maximum(i - N_LAT_TILES1, 0), 0)
    rope_idx = lambda i: (jnp.where(i < N_LAT_TILES1, i % TILES_PER_SEQ1, TILES_PER_SEQ1), 0)
    full = lambda shape: pl.BlockSpec(shape, lambda i: (0, 0))
    row_blk = lambda w: pl.BlockSpec((TM1, w), lambda i: (i, 0))
    out_shapes = [
        jax.ShapeDtypeStruct((T_ALL, ATT_WIDTH), BF16),
        jax.ShapeDtypeStruct((T_ALL, KV_W), BF16),
        jax.ShapeDtypeStruct((T_ALL, KV_W), BF16),
        jax.ShapeDtypeStruct((T_ALL, MQK_W), BF16),
        jax.ShapeDtypeStruct((MQK_W, T_ALL), BF16),
        jax.ShapeDtypeStruct((T_ALL, M_WIDTH), BF16),
        jax.ShapeDtypeStruct((T_ALL, M_WIDTH), F32),
        jax.ShapeDtypeStruct((T_ALL, LANE), F32),
    ]
    out_specs = [
        row_blk(ATT_WIDTH), row_blk(KV_W), row_blk(KV_W), row_blk(MQK_W),
        pl.BlockSpec((MQK_W, TM1), lambda i: (0, i)),
        row_blk(M_WIDTH), row_blk(M_WIDTH), row_blk(LANE),
    ]
    return pl.pallas_call(
        _inproj_kernel,
        grid=(N_TILES1,),
        in_specs=[
            pl.BlockSpec((TM1, D_MODEL), lat_idx),
            pl.BlockSpec((TM1, D_MODEL), ctx_idx),
            full((MOD_ROWS, D_MODEL)), full((MOD_ROWS, D_MODEL)), full((1, D_MODEL)),
            full((D_MODEL, IN_COLS_PAD)), full((1, IN_COLS_PAD)),
            full((1, HEAD_DIM)), full((1, HEAD_DIM)),
            pl.BlockSpec((TM1, HEAD_DIM), rope_idx),
            pl.BlockSpec((TM1, HEAD_DIM), rope_idx),
        ],
        out_specs=out_specs,
        out_shape=out_shapes,
        scratch_shapes=[pltpu.VMEM((TM1, D_MODEL), BF16)],
        compiler_params=_params(1),
        name="in_proj",
    )(x2, ctx2, sh1, sc1, g1, w_in_p, b_in_p, g_q, g_k, cos_t, sin_t)


TQ = 256
CK = 512
GQ_W = GROUP * HEAD_DIM


def _attn_kernel(q_ref, kl_ref, vl_ref, kc_ref, vc_ref, o_ref):
    chunks = [(kl_ref, vl_ref, c * CK, CK) for c in range(SEQ // CK)]
    chunks.append((kc_ref, vc_ref, 0, CTX_LEN))
    for g in range(GROUP):
        qh = q_ref[:, g * HEAD_DIM:(g + 1) * HEAD_DIM]
        m = jnp.full((TQ, 1), -jnp.inf, F32)
        l = jnp.zeros((TQ, 1), F32)
        acc = jnp.zeros((TQ, HEAD_DIM), F32)
        for kr, vr, st, sz in chunks:
            s = lax.dot_general(qh, kr[st:st + sz, :], (((1,), (1,)), ((), ())),
                                preferred_element_type=F32)
            m_new = jnp.maximum(m, jnp.max(s, axis=-1, keepdims=True))
            p = jnp.exp(s - m_new)
            alpha = jnp.exp(m - m_new)
            l = alpha * l + jnp.sum(p, axis=-1, keepdims=True)
            acc = alpha * acc + jnp.dot(p.astype(BF16), vr[st:st + sz, :],
                                        preferred_element_type=F32)
            m = m_new
        o_ref[:, g * HEAD_DIM:(g + 1) * HEAD_DIM] = (acc / l).astype(BF16)


def _attention(q, k, v):
    nq = SEQ // TQ
    ctx_blk0 = T_LAT // CTX_LEN
    lat_kv = pl.BlockSpec((SEQ, HEAD_DIM), lambda b, h, i: (b, h))
    ctx_kv = pl.BlockSpec((CTX_LEN, HEAD_DIM), lambda b, h, i: (ctx_blk0 + b, h))
    q_blk = pl.BlockSpec((TQ, GQ_W), lambda b, h, i: (b * nq + i, h))
    return pl.pallas_call(
        _attn_kernel,
        grid=(BATCH, KV_HEADS, nq),
        in_specs=[q_blk, lat_kv, lat_kv, ctx_kv, ctx_kv],
        out_specs=q_blk,
        out_shape=jax.ShapeDtypeStruct((T_LAT, ATT_WIDTH), BF16),
        compiler_params=_params(3),
        name="gqa_attention",
    )(q, k, v, k, v)


N_CHUNKS = SEQ // M_CHUNK
CTX_CHUNKS = CTX_LEN // M_CHUNK
N_CHAINS = 2 * M_HEADS


def _log_sigmoid(x):
    return jnp.minimum(x, 0.0) - jnp.log(1.0 + jnp.exp(-jnp.abs(x)))


def _mlstm_kernel(qf_ref, qb_ref, qc_ref, kf_ref, kb_ref, kc_ref, vf_ref, vb_ref, vc_ref,
                  gf_ref, gb_ref, gc_ref, hf_ref, hb_ref, c_scr, n_scr, m_scr):
    j = pl.program_id(1)
    ri = lax.broadcasted_iota(jnp.int32, (M_CHUNK, M_CHUNK), 0)
    ci = lax.broadcasted_iota(jnp.int32, (M_CHUNK, M_CHUNK), 1)
    lower = ci <= ri
    upper = ci >= ri
    tri_l = lower.astype(F32)
    tri_u = upper.astype(F32)

    def gate_tables(g, rev):
        gt = g.T
        mcol, mrow = (tri_u, tri_l) if rev else (tri_l, tri_u)
        bcol = jnp.dot(mcol, _log_sigmoid(g), precision=HIGHEST, preferred_element_type=F32)
        brow = jnp.dot(_log_sigmoid(gt), mrow, precision=HIGHEST, preferred_element_type=F32)
        return gt, bcol, brow

    def chain_step(chain, q, kt, v, tabs, rev, h_out):
        gt, bcol, brow = tabs
        icol = chain
        fcol = N_CHAINS + chain
        last = 0 if rev else M_CHUNK - 1
        b_col = bcol[:, fcol:fcol + 1]
        b_row = brow[fcol:fcol + 1, :]
        i_row = gt[icol:icol + 1, :]
        tot = brow[fcol:fcol + 1, last:last + 1]
        m_old = m_scr[chain][:1, :1]
        c_old = c_scr[chain]
        n_old = n_scr[chain]
        if h_out is not None:
            d = jnp.where(upper if rev else lower, b_col - b_row + i_row, -jnp.inf)
            inter = b_col + m_old
            m_t = jnp.maximum(inter, jnp.max(d, axis=-1, keepdims=True))
            w_intra = jnp.exp(d - m_t)
            w_inter = jnp.exp(inter - m_t)
            s = jnp.dot(q, kt, preferred_element_type=F32) * w_intra
            qc = jnp.dot(q, c_old.astype(BF16), preferred_element_type=F32)
            qn = jnp.dot(q, n_old.astype(BF16), preferred_element_type=F32)[:, :1]
            num = jnp.dot(s.astype(BF16), v, preferred_element_type=F32) + w_inter * qc
            den = jnp.sum(s, axis=-1, keepdims=True) + w_inter * qn
            h_out(num / jnp.maximum(jnp.abs(den), jnp.exp(-m_t)))
        g_row = tot - b_row + i_row
        m_new = jnp.maximum(tot + m_old, jnp.max(g_row, axis=-1, keepdims=True))
        w_state = jnp.exp(g_row - m_new)
        decay = jnp.exp(tot + m_old - m_new)
        kw = kt.astype(F32) * w_state
        c_scr[chain] = decay * c_old + jnp.dot(kw.astype(BF16), v, preferred_element_type=F32)
        n_new = decay * n_old[:, :1] + jnp.sum(kw, axis=-1, keepdims=True)
        n_scr[chain] = jnp.broadcast_to(n_new, (M_DK, LANE))
        m_scr[chain] = jnp.broadcast_to(m_new, (8, LANE))

    def run_chunk(q_ref, kt_ref, v_ref, g_ref, r0, rev, h_ref):
        tabs = gate_tables(g_ref[r0:r0 + M_CHUNK, :], rev)
        for hd in range(M_HEADS):
            chain = (M_HEADS if rev else 0) + hd
            q = q_ref[r0:r0 + M_CHUNK, hd * M_DK:(hd + 1) * M_DK]
            kt = kt_ref[hd * M_DK:(hd + 1) * M_DK, r0:r0 + M_CHUNK]
            v = v_ref[r0:r0 + M_CHUNK, hd * M_DV:(hd + 1) * M_DV]
            if h_ref is None:
                h_out = None
            else:
                def h_out(hv, hd=hd):
                    h_ref[:, hd * M_DV:(hd + 1) * M_DV] = hv
            chain_step(chain, q, kt, v, tabs, rev, h_out)

    @pl.when(j == 0)
    def _():
        c_scr[...] = jnp.zeros_like(c_scr)
        n_scr[...] = jnp.zeros_like(n_scr)
        m_scr[...] = jnp.zeros_like(m_scr)
        for cc in range(CTX_CHUNKS):
            run_chunk(qc_ref, kc_ref, vc_ref, gc_ref, cc * M_CHUNK, False, None)
            run_chunk(qc_ref, kc_ref, vc_ref, gc_ref, (CTX_CHUNKS - 1 - cc) * M_CHUNK, True, None)

    run_chunk(qf_ref, kf_ref, vf_ref, gf_ref, 0, False, hf_ref)
    run_chunk(qb_ref, kb_ref, vb_ref, gb_ref, 0, True, hb_ref)


def _mlstm(mq, mkt, mv, gates):
    ctx_blk0 = T_LAT // CTX_LEN
    fwd = lambda b, j: b * N_CHUNKS + j
    bwd = lambda b, j: b * N_CHUNKS + N_CHUNKS - 1 - j
    ctx = lambda b, j: ctx_blk0 + b

    def rows(width, tile, idx):
        return pl.BlockSpec((tile, width), lambda b, j: (idx(b, j), 0))

    def cols(tile, idx):
        return pl.BlockSpec((MQK_W, tile), lambda b, j: (0, idx(b, j)))

    h_shape = jax.ShapeDtypeStruct((T_LAT, M_WIDTH), F32)
    return pl.pallas_call(
        _mlstm_kernel,
        grid=(BATCH, N_CHUNKS),
        in_specs=[
            rows(MQK_W, M_CHUNK, fwd), rows(MQK_W, M_CHUNK, bwd), rows(MQK_W, CTX_LEN, ctx),
            cols(M_CHUNK, fwd), cols(M_CHUNK, bwd), cols(CTX_LEN, ctx),
            rows(M_WIDTH, M_CHUNK, fwd), rows(M_WIDTH, M_CHUNK, bwd), rows(M_WIDTH, CTX_LEN, ctx),
            rows(LANE, M_CHUNK, fwd), rows(LANE, M_CHUNK, bwd), rows(LANE, CTX_LEN, ctx),
        ],
        out_specs=[rows(M_WIDTH, M_CHUNK, fwd), rows(M_WIDTH, M_CHUNK, bwd)],
        out_shape=[h_shape, h_shape],
        scratch_shapes=[
            pltpu.VMEM((N_CHAINS, M_DK, M_DV), F32),
            pltpu.VMEM((N_CHAINS, M_DK, LANE), F32),
            pltpu.VMEM((N_CHAINS, 8, LANE), F32),
        ],
        compiler_params=_params(2),
        name="mlstm_scan",
    )(mq, mq, mq, mkt, mkt, mkt, mv, mv, mv, gates, gates, gates)


TM4 = 256
TILES_PER_SEQ4 = SEQ // TM4
NEG_BIG = -1e30
D_HALF = D_MODEL // 2
HI_MASK = 0xFFFF0000


def _pack_bf16_pairs(v):
    def bits(a):
        return lax.bitcast_convert_type(a.astype(BF16).astype(F32), jnp.uint32)
    return (bits(v[:, :D_HALF]) >> 16) | (bits(v[:, D_HALF:]) & jnp.uint32(HI_MASK))


def _unpack_bf16_pairs(u):
    lo = lax.bitcast_convert_type(u << 16, F32)
    hi = lax.bitcast_convert_type(u & jnp.uint32(HI_MASK), F32)
    return jnp.concatenate([lo, hi], axis=-1).astype(BF16)


def _outproj_kernel(a_ref, hf_ref, hb_ref, mo_ref, x_ref, w_ref, gm_ref, gt1_ref, sh2_ref,
                    sc2_ref, g2_ref, wr_ref, br_ref,
                    x1_ref, h2_ref, ti_ref, tw_ref):
    i = pl.program_id(0)
    row = i // TILES_PER_SEQ4
    acc = jnp.dot(a_ref[...], w_ref[0:ATT_WIDTH, :], preferred_element_type=F32)
    for hd in range(M_HEADS):
        sl = slice(hd * M_DV, (hd + 1) * M_DV)
        hn = _rms(hf_ref[:, sl] + hb_ref[:, sl]) * gm_ref[:, sl]
        mo = mo_ref[:, sl]
        ym = hn / (1.0 + jnp.exp(-mo))
        acc += jnp.dot(ym.astype(BF16), w_ref[ATT_WIDTH + hd * M_DV:ATT_WIDTH + (hd + 1) * M_DV, :],
                       preferred_element_type=F32)
    x1 = x_ref[...] + gt1_ref[pl.ds(row, 1), :] * acc
    x1_ref[...] = x1
    h2 = (_rms(x1) * g2_ref[...]) * (1.0 + sc2_ref[pl.ds(row, 1), :]) + sh2_ref[pl.ds(row, 1), :]
    h2_ref[...] = _pack_bf16_pairs(h2)
    logits =jnp.dot(h2, wr_ref[...], precision=HIGHEST, preferred_element_type=F32) + br_ref[...]
    lane = lax.broadcasted_iota(jnp.int32, (TM4, LANE), 1)
    vals, idxs = [], []
    for _ in range(TOP_K):
        mx = jnp.max(logits, axis=-1, keepdims=True)
        ik = jnp.min(jnp.where(logits == mx, lane, LANE), axis=-1, keepdims=True)
        vals.append(mx)
        idxs.append(ik)
        logits = jnp.where(lane == ik, -jnp.inf, logits)
    es = [jnp.exp(vk - vals[0]) for vk in vals]
    tot = es[0] + es[1] + es[2] + es[3]
    ti = jnp.zeros((TM4, LANE), jnp.int32)
    tw = jnp.zeros((TM4, LANE), F32)
    for kk in range(TOP_K):
        ti = jnp.where(lane == kk, idxs[kk], ti)
        tw = jnp.where(lane == kk, es[kk] / tot, tw)
    ti_ref[...] = ti
    tw_ref[...] = tw


def _out_projection(a_lat, hf, hb, mo, x2, w_out_b, g_mlstm, gt1, sh2, sc2, g2, wr_p, br_p):
    full = lambda shape: pl.BlockSpec(shape, lambda i: (0, 0))
    row_blk = lambda w: pl.BlockSpec((TM4, w), lambda i: (i, 0))
    return pl.pallas_call(
        _outproj_kernel,
        grid=(T_LAT // TM4,),
        in_specs=[
            row_blk(ATT_WIDTH), row_blk(M_WIDTH), row_blk(M_WIDTH), row_blk(M_WIDTH),
            row_blk(D_MODEL), full((ATT_WIDTH + M_WIDTH, D_MODEL)), full((1, M_WIDTH)),
            full((MOD_ROWS, D_MODEL)), full((MOD_ROWS, D_MODEL)), full((MOD_ROWS, D_MODEL)),
            full((1, D_MODEL)), full((D_MODEL, LANE)), full((1, LANE)),
        ],
        out_specs=[row_blk(D_MODEL), row_blk(D_HALF), row_blk(LANE), row_blk(LANE)],
        out_shape=[
            jax.ShapeDtypeStruct((T_LAT, D_MODEL), F32),
            jax.ShapeDtypeStruct((T_LAT, D_HALF), jnp.uint32),
            jax.ShapeDtypeStruct((T_LAT, LANE), jnp.int32),
            jax.ShapeDtypeStruct((T_LAT, LANE), F32),
        ],
        compiler_params=_params(1),
        name="out_proj_router",
    )(a_lat, hf, hb, mo, x2, w_out_b, g_mlstm, gt1, sh2, sc2, g2, wr_p, br_p)


MB = 128
N_ASSIGN = T_LAT * TOP_K
N_BLOCKS = N_ASSIGN // MB + N_EXPERTS
P_ROWS = N_BLOCKS * MB
N_PADS = P_ROWS - N_ASSIGN
SB = 10
W_MAX = N_EXPERTS + N_BLOCKS // SB
XROWS = SB * MB


def _routing_tables(top_idx):
    e_flat = top_idx.reshape(N_ASSIGN)
    onehot = (e_flat[:, None] == jnp.arange(N_EXPERTS, dtype=jnp.int32)[None, :]).astype(jnp.int32)
    csum = jnp.cumsum(onehot, axis=0)
    rank = jnp.sum(csum * onehot, axis=1) - 1
    counts = csum[-1]
    nblk = (counts + MB - 1) // MB
    blk_start = jnp.cumsum(nblk) - nblk
    dest = (blk_start[e_flat] * MB + rank).astype(jnp.int32)
    npad = nblk * MB - counts
    pend = jnp.cumsum(npad)
    n_valid_blk = jnp.sum(nblk)
    pid = jnp.arange(N_PADS, dtype=jnp.int32)
    pe = jnp.minimum(jnp.sum((pend[None, :] <= pid[:, None]).astype(jnp.int32), axis=1),
                     N_EXPERTS - 1)
    prow = blk_start[pe] * MB + counts[pe] + (pid - (pend[pe] - npad[pe]))
    pad_rows = jnp.where(pid < pend[-1], prow, n_valid_blk * MB + pid - pend[-1]).astype(jnp.int32)
    nwork = (nblk + SB - 1) // SB
    wend = jnp.cumsum(nwork)
    wstart = wend - nwork
    n_items = wend[-1]
    wid = jnp.arange(W_MAX, dtype=jnp.int32)
    valid = wid < n_items
    we = jnp.minimum(jnp.sum((wend[None, :] <= wid[:, None]).astype(jnp.int32), axis=1),
                     N_EXPERTS - 1)
    local = wid - wstart[we]
    work_blk = jnp.where(valid, blk_start[we] + local * SB, 0)
    work_n = jnp.where(valid, jnp.clip(nblk[we] - local * SB, 0, SB), 0)
    last_e = we[jnp.maximum(n_items - 1, 0)]
    work_e = jnp.where(valid, we, last_e)
    return (dest, pad_rows,
            work_e.astype(jnp.int32), work_blk.astype(jnp.int32), work_n.astype(jnp.int32),
            n_valid_blk.astype(jnp.int32)[None])


TMD = 256


def _dispatch_kernel(dest_ref, pad_ref, h2_ref, x_hbm, zrow, sem, psem):
    i = pl.program_id(0)

    def pad_copy(p):
        return pltpu.make_async_copy(zrow.at[pl.ds(0, 1)], x_hbm.at[pl.ds(pad_ref[p], 1)], psem)

    @pl.when(i == 0)
    def _():
        zrow[...] = jnp.zeros_like(zrow)

        def issue(p, carry):
            pad_copy(p).start()
            return carry
        lax.fori_loop(0, N_PADS, issue, 0)

    base = i * (TMD * TOP_K)

    def issue(t, carry):
        for kk in range(TOP_K):
            d = dest_ref[base + t * TOP_K + kk]
            pltpu.make_async_copy(h2_ref.at[pl.ds(t, 1)], x_hbm.at[pl.ds(d, 1)], sem).start()
        return carry
    lax.fori_loop(0, TMD, issue, 0)
    for kk in range(TOP_K):
        pltpu.make_async_copy(h2_ref, x_hbm.at[pl.ds(0, TMD)], sem).wait()

    @pl.when(i == 0)
    def _():
        for _ in range(N_PADS // TMD):
            pltpu.make_async_copy(h2_ref, x_hbm.at[pl.ds(0, TMD)], psem).wait()


def _dispatch(h2, dest, pad_rows):
    grid_spec = pltpu.PrefetchScalarGridSpec(
        num_scalar_prefetch=2,
        grid=(T_LAT // TMD,),
        in_specs=[pl.BlockSpec((TMD, D_HALF), lambda i, *_: (i, 0))],
        out_specs=pl.BlockSpec(memory_space=pl.ANY),
        scratch_shapes=[
            pltpu.VMEM((8, D_HALF), jnp.uint32),
            pltpu.SemaphoreType.DMA(()),
            pltpu.SemaphoreType.DMA(()),
        ],
    )
    return pl.pallas_call(
        _dispatch_kernel,
        grid_spec=grid_spec,
        out_shape=jax.ShapeDtypeStruct((P_ROWS, D_HALF), jnp.uint32),
        compiler_params=_params(1),
        name="moe_dispatch",
    )(dest, pad_rows, h2)


TF = 512
NF = D_FF // TF
MC = 2 * MB


def _moe_kernel(we_ref, wb_ref, wn_ref, nvb_ref,
                x_hbm, w1g_ref, w1u_ref, w2_ref, b1g_ref, b1u_ref, b2_ref,
                y_hbm, xbuf, xb16, acc, xsem, ysem):
    w = pl.program_id(0)
    f = pl.program_id(1)
    nsub = wn_ref[w]
    r0 = wb_ref[w] * MB
    slot = w % 2

    def x_copy(item, slot_, s):
        src = x_hbm.at[pl.ds(wb_ref[item] * MB + s * MB, MB)]
        return pltpu.make_async_copy(src, xbuf.at[slot_, pl.ds(s * MB, MB)], xsem.at[slot_])

    def start_x(item, slot_):
        def body(s, carry):
            x_copy(item, slot_, s).start()
            return carry
        lax.fori_loop(0, wn_ref[item], body, 0)

    @pl.when(jnp.logical_and(w == 0, f == 0))
    def _():
        start_x(0, 0)

    @pl.when(f == 0)
    def _():
        def wait(s, carry):
            x_copy(w, slot, s).wait()
            return carry
        lax.fori_loop(0, nsub, wait, 0)

        def body(s, carry):
            rs = pl.ds(pl.multiple_of(s * MB, MB), MB)
            xb16[rs, :] = _unpack_bf16_pairs(xbuf[slot, rs, :])
            acc[rs, :] = jnp.broadcast_to(b2_ref[...], (MB, D_MODEL))
            return carry
        lax.fori_loop(0, nsub, body, 0)

        @pl.when(w + 1 < W_MAX)
        def _():
            start_x(w + 1, 1 - slot)

    def y_copy(s):
        rs = pl.ds(pl.multiple_of(s * MB, MB), MB)
        return pltpu.make_async_copy(acc.at[rs], y_hbm.at[pl.ds(r0 + s * MB, MB)], ysem)

    def chunk(row, m):
        rs = pl.ds(row, m)
        xb = xb16[rs, :]
        gate = jnp.dot(xb, w1g_ref[...], preferred_element_type=F32) + b1g_ref[...]
        up = jnp.dot(xb, w1u_ref[...], preferred_element_type=F32) + b1u_ref[...]
        gate = jnp.minimum(gate, SWIGLU_LIMIT)
        up = jnp.clip(up, -SWIGLU_LIMIT, SWIGLU_LIMIT)
        glu = gate / (1.0 + jnp.exp(-SWIGLU_ALPHA * gate))
        acc[rs, :] += jnp.dot((up + 1.0) * glu, w2_ref[...], preferred_element_type=F32)

    nquad = nsub // 4
    rem = nsub - nquad * 4

    def quad(c, carry):
        row = pl.multiple_of(c * (2 * MC), 2 * MC)
        chunk(row, MC)
        chunk(row + MC, MC)
        return carry
    lax.fori_loop(0, nquad, quad, 0)

    @pl.when(rem >= 2)
    def _():
        chunk(pl.multiple_of(nquad * (2 * MC), MC), MC)

    @pl.when(rem % 2 == 1)
    def _():
        chunk(pl.multiple_of((nsub - 1) * MB, MB), MB)

    @pl.when(f == NF - 1)
    def _():
        def start(s, carry):
            y_copy(s).start()
            return carry
        lax.fori_loop(0, nsub, start, 0)

        def wait(s, carry):
            y_copy(s).wait()
            return carry
        lax.fori_loop(0, nsub, wait, 0)

    @pl.when(jnp.logical_and(w == W_MAX - 1, f == NF - 1))
    def _():
        acc[0:MB, :] = jnp.zeros((MB, D_MODEL), F32)

        def tail_copy(bk):
            return pltpu.make_async_copy(acc.at[0:MB], y_hbm.at[pl.ds(bk * MB, MB)], ysem)

        def start(bk, carry):
            tail_copy(bk).start()
            return carry
        lax.fori_loop(nvb_ref[0], N_BLOCKS, start, 0)

        def wait(bk, carry):
            tail_copy(bk).wait()
            return carry
        lax.fori_loop(nvb_ref[0], N_BLOCKS, wait, 0)


def _moe(x_sorted, w1, b1, w2, b2, work_e, work_blk, work_n, n_valid_blk):
    b1r = b1.reshape(N_EXPERTS, 1, 2 * D_FF)
    b2r = b2.reshape(N_EXPERTS, 1, D_MODEL)
    grid_spec = pltpu.PrefetchScalarGridSpec(
        num_scalar_prefetch=4,
        grid=(W_MAX, NF),
        in_specs=[
            pl.BlockSpec(memory_space=pl.ANY),
            pl.BlockSpec((None, D_MODEL, TF), lambda w, f, we, *_: (we[w], 0, f)),
            pl.BlockSpec((None, D_MODEL, TF), lambda w, f, we, *_: (we[w], 0, NF + f)),
            pl.BlockSpec((None, TF, D_MODEL), lambda w, f, we, *_: (we[w], f, 0)),
            pl.BlockSpec((None, 1, TF), lambda w, f, we, *_: (we[w], 0, f)),
            pl.BlockSpec((None, 1, TF), lambda w, f, we, *_: (we[w], 0, NF + f)),
            pl.BlockSpec((None, 1, D_MODEL), lambda w, f, we, *_: (we[w], 0, 0)),
        ],
        out_specs=pl.BlockSpec(memory_space=pl.ANY),
        scratch_shapes=[
            pltpu.VMEM((2, XROWS, D_HALF), jnp.uint32),
            pltpu.VMEM((XROWS, D_MODEL), BF16),
            pltpu.VMEM((XROWS, D_MODEL), F32),
            pltpu.SemaphoreType.DMA((2,)),
            pltpu.SemaphoreType.DMA(()),
        ],
    )
    return pl.pallas_call(
        _moe_kernel,
        grid_spec=grid_spec,
        out_shape=jax.ShapeDtypeStruct((P_ROWS, D_MODEL), F32),
        compiler_params=_params(2),
        name="moe_experts",
    )(work_e, work_blk, work_n, n_valid_blk, x_sorted, w1, w1, w2, b1r, b1r, b2r)


TM7 = 256
N_TILES7 = T_LAT // TM7
TILES_PER_SEQ7 = SEQ // TM7


def _final_kernel(pos_ref, x1_ref, tw_ref, gt2_ref, gf_ref, y_hbm, o_ref, ybuf, sem):
    i = pl.program_id(0)
    slot = i % 2

    def start_gather(tile, slot_):
        base = tile * (TM7 * TOP_K)

        def body(t, carry):
            for kk in range(TOP_K):
                p = pos_ref[base + t * TOP_K + kk]
                pltpu.make_async_copy(y_hbm.at[pl.ds(p, 1)], ybuf.at[slot_, kk, pl.ds(t, 1)],
                                      sem.at[slot_]).start()
            return carry
        lax.fori_loop(0, TM7, body, 0)

    @pl.when(i == 0)
    def _():
        start_gather(0, 0)

    @pl.when(i + 1 < N_TILES7)
    def _():
        start_gather(i + 1, 1 - slot)

    for kk in range(TOP_K):
        pltpu.make_async_copy(y_hbm.at[pl.ds(0, TM7)], ybuf.at[slot, kk], sem.at[slot]).wait()

    row = i // TILES_PER_SEQ7
    tw = tw_ref[...]
    mix = jnp.zeros((TM7, D_MODEL), F32)
    for kk in range(TOP_K):
        mix += ybuf[slot, kk] * tw[:, kk:kk + 1]
    xo = x1_ref[...] + gt2_ref[pl.ds(row, 1), :] * mix
    o_ref[...] = _rms(xo) * gf_ref[...]


def _final(dest, x1, top_w, gt2, g_final, y_sorted):
    grid_spec = pltpu.PrefetchScalarGridSpec(
        num_scalar_prefetch=1,
        grid=(N_TILES7,),
        in_specs=[
            pl.BlockSpec((TM7, D_MODEL), lambda i, *_: (i, 0)),
            pl.BlockSpec((TM7, LANE), lambda i, *_: (i, 0)),
            pl.BlockSpec((MOD_ROWS, D_MODEL), lambda i, *_: (0, 0)),
            pl.BlockSpec((1, D_MODEL), lambda i, *_: (0, 0)),
            pl.BlockSpec(memory_space=pl.ANY),
        ],
        out_specs=pl.BlockSpec((TM7, D_MODEL), lambda i, *_: (i, 0)),
        scratch_shapes=[
            pltpu.VMEM((2, TOP_K, TM7, D_MODEL), F32),
            pltpu.SemaphoreType.DMA((2,)),
        ],
    )
    return pl.pallas_call(
        _final_kernel,
        grid_spec=grid_spec,
        out_shape=jax.ShapeDtypeStruct((T_LAT, D_MODEL), F32),
        compiler_params=_params(1),
        name="combine_final_norm",
    )(dest, x1, top_w, gt2, g_final, y_sorted)


def _rope_tables():
    rows = SEQ // GRID_W
    row = jnp.repeat(jnp.arange(rows, dtype=F32), GRID_W)
    col = jnp.tile(jnp.arange(GRID_W, dtype=F32), rows)
    inv = ROPE_THETA ** (-jnp.arange(0, AXIS_DIM, 2, dtype=F32) / AXIS_DIM)
    ang = jnp.concatenate([row[:, None] * inv, col[:, None] * inv], axis=-1)
    cos = jnp.repeat(jnp.cos(ang), 2, axis=-1)
    sin = jnp.repeat(jnp.sin(ang), 2, axis=-1)
    sign = jnp.tile(jnp.array([-1.0, 1.0], F32), HEAD_DIM // 2)
    cos_t = jnp.concatenate([cos, jnp.ones((TM1, HEAD_DIM), F32)], axis=0)
    sin_t = jnp.concatenate([sin * sign, jnp.zeros((TM1, HEAD_DIM), F32)], axis=0)
    return cos_t, sin_t


def kernel(x, c, ctx, c_ctx, w_mod, b_mod, g_norm1, w_in, b_in, g_q, g_k, g_mlstm, w_out,
           g_norm2, w_router, b_router, w1, b1, w2, b2, g_final):
    x2 = x.reshape(T_LAT, D_MODEL)
    ctx2 = ctx.reshape(T_CTX, D_MODEL)
    cos_t, sin_t = _rope_tables()
    assert w_mod.shape[0] == 1
    l = 0
    c_rows = jnp.concatenate(
        [c, c_ctx[None, :], jnp.zeros((MOD_ROWS - BATCH - 1, D_MODEL), F32)], axis=0)
    mod = _modulation(c_rows, w_mod[l], b_mod[l][None, :])
    sh1, sc1, gt1, sh2, sc2, gt2 = [mod[:, k * D_MODEL:(k + 1) * D_MODEL]
                                    for k in range(N_ADALN)]
    pad = IN_COLS_PAD - IN_COLS
    w_in_p = jnp.pad(w_in[l], ((0, 0), (0, pad))).astype(BF16)
    b_in_p = jnp.pad(b_in[l], (0, pad))[None, :]
    q, k, v, mq, mkt, mv, mo, gates = _in_projection(
        x2, ctx2, sh1, sc1, g_norm1[l][None, :], w_in_p, b_in_p,
        g_q[l][None, :], g_k[l][None, :], cos_t, sin_t)
    a_lat = _attention(q, k, v)
    hf, hb = _mlstm(mq, mkt, mv, gates)
    wr_p = jnp.pad(w_router[l], ((0, 0), (0, LANE - N_EXPERTS)))
    br_p = jnp.pad(b_router[l], (0, LANE - N_EXPERTS), constant_values=NEG_BIG)[None, :]
    x1, h2, top_i, top_w = _out_projection(
        a_lat, hf, hb, mo, x2, w_out[l].astype(BF16), g_mlstm[l][None, :],
        gt1, sh2, sc2, g_norm2[l][None, :], wr_p, br_p)
    dest, pad_rows, work_e, work_blk, work_n, n_valid_blk = _routing_tables(top_i[:, :TOP_K])
    x_sorted = _dispatch(h2, dest, pad_rows)
    y_sorted = _moe(x_sorted, w1[l], b1[l], w2[l], b2[l], work_e, work_blk, work_n, n_valid_blk)
    out = _final(dest, x1, top_w, gt2, g_final[None, :], y_sorted)
    return out.reshape(BATCH, SEQ, D_MODEL)
```

```python
import jax
import jax.numpy as jnp
from jax import lax
from jax.experimental import pallas as pl
from jax.experimental.pallas import tpu as pltpu

F32 = jnp.float32
BF16 = jnp.bfloat16
HIGHEST = lax.Precision.HIGHEST

D_MODEL = 2048
BATCH = 2
SEQ = 4096
CTX_LEN = 256
GRID_W = 64
N_ADALN = 6
EPS = 1e-6
N_HEADS = 8
KV_HEADS = 2
GROUP = N_HEADS // KV_HEADS
HEAD_DIM = 128
AXIS_DIM = HEAD_DIM // 2
ROPE_THETA = 10000.0
ATT_WIDTH = N_HEADS * HEAD_DIM
ATT_SCALE = HEAD_DIM ** -0.5
M_HEADS = 4
M_DK = 128
M_DV = 256
M_CHUNK = 128
M_WIDTH = M_HEADS * M_DV
N_EXPERTS = 32
TOP_K = 4
D_FF = D_MODEL
SWIGLU_LIMIT = 7.0
SWIGLU_ALPHA = 1.702

T_LAT = BATCH * SEQ
T_CTX = BATCH * CTX_LEN
T_ALL = T_LAT + T_CTX
KV_W = KV_HEADS * HEAD_DIM
MQK_W = M_HEADS * M_DK
LANE = 128

C_Q = 0
C_K = C_Q + ATT_WIDTH
C_V = C_K + KV_W
C_MQ = C_V + KV_W
C_MK = C_MQ + MQK_W
C_MV = C_MK + MQK_W
C_MO = C_MV + M_WIDTH
C_G = C_MO + M_WIDTH
IN_COLS = C_G + 4 * M_HEADS
IN_COLS_PAD = C_G + LANE

VMEM_LIMIT = 56 * 1024 * 1024


def _params(n_axes):
    return pltpu.CompilerParams(
        dimension_semantics=("arbitrary",) * n_axes, vmem_limit_bytes=VMEM_LIMIT)


def _rms(x):
    return x * lax.rsqrt(jnp.mean(x * x, axis=-1, keepdims=True) + EPS)


MOD_ROWS = 8
MOD_TN = 1024


def _mod_kernel(c_ref, w_ref, b_ref, o_ref):
    c = c_ref[...]
    s = c / (1.0 + jnp.exp(-c))
    o_ref[...] = jnp.dot(s, w_ref[...], preferred_element_type=F32) + b_ref[...]


def _modulation(c_rows, w_mod, b_mod):
    n = w_mod.shape[1]
    return pl.pallas_call(
        _mod_kernel,
        grid=(n // MOD_TN,),
        in_specs=[
            pl.BlockSpec((MOD_ROWS, D_MODEL), lambda j: (0, 0)),
            pl.BlockSpec((D_MODEL, MOD_TN), lambda j: (0, j)),
            pl.BlockSpec((1, MOD_TN), lambda j: (0, j)),
        ],
        out_specs=pl.BlockSpec((MOD_ROWS, MOD_TN), lambda j: (0, j)),
        out_shape=jax.ShapeDtypeStruct((MOD_ROWS, n), F32),
        compiler_params=_params(1),
        name="adaln_mod",
    )(c_rows, w_mod, b_mod)


TM1 = 256
N_LAT_TILES1 = T_LAT // TM1
N_TILES1 = T_ALL // TM1
TILES_PER_SEQ1 = SEQ // TM1
PCH = 512
LOG2E = 1.4426950408889634
Q_SCALE = ATT_SCALE * LOG2E
V_AUG = 2 * HEAD_DIM


def _inproj_kernel(x_ref, ctx_ref, sh_ref, sc_ref, g1_ref, w_ref, b_ref, wg_ref, bg_ref,
                   gq_ref, gk_ref, cos_ref, sin_ref,
                   q_ref, k_ref, v_ref, mq_ref, mkt_ref, mv_ref, mo_ref, g_ref, h_scr):
    i = pl.program_id(0)
    row = jnp.where(i < N_LAT_TILES1, i // TILES_PER_SEQ1, BATCH)
    sc = sc_ref[pl.ds(row, 1), :]
    sh = sh_ref[pl.ds(row, 1), :]

    def norm_mod(xv):
        return (_rms(xv) * g1_ref[...]) * (1.0 + sc) + sh

    @pl.when(i < N_LAT_TILES1)
    def _():
        h_scr[...] = norm_mod(x_ref[...]).astype(BF16)

    @pl.when(i >= N_LAT_TILES1)
    def _():
        h_scr[...] = norm_mod(ctx_ref[...]).astype(BF16)

    h = h_scr[...]

    def proj(c0, c1):
        return jnp.dot(h, w_ref[:, c0:c1], preferred_element_type=F32) + b_ref[:, c0:c1]

    cos_f = cos_ref[...]
    sin_s = sin_ref[...]
    lane = lax.broadcasted_iota(jnp.int32, (TM1, HEAD_DIM), 1)
    even = (lane % 2) == 0

    def head_norm_rope(a, g):
        y = _rms(a) * g
        nxt = pltpu.roll(y, HEAD_DIM - 1, 1)
        prv = pltpu.roll(y, 1, 1)
        return y * cos_f + jnp.where(even, nxt, prv) * sin_s

    for c0 in range(C_Q, C_K, PCH):
        acc = proj(c0, c0 + PCH)
        for hh in range(PCH // HEAD_DIM):
            a = acc[:, hh * HEAD_DIM:(hh + 1) * HEAD_DIM]
            r = head_norm_rope(a, gq_ref[...]) * Q_SCALE
            q_ref[:, c0 + hh * HEAD_DIM:c0 + (hh + 1) * HEAD_DIM] = r.astype(BF16)
    acc = proj(C_K, C_MQ)
    for hh in range(KV_HEADS):
        a = acc[:, hh * HEAD_DIM:(hh + 1) * HEAD_DIM]
        k_ref[:, hh * HEAD_DIM:(hh + 1) * HEAD_DIM] = head_norm_rope(a, gk_ref[...]).astype(BF16)
    for hh in range(KV_HEADS):
        v_ref[:, hh * V_AUG:hh * V_AUG + HEAD_DIM] = (
            acc[:, KV_W + hh * HEAD_DIM:KV_W + (hh + 1) * HEAD_DIM].astype(BF16))
        v_ref[:, hh * V_AUG + HEAD_DIM:(hh + 1) * V_AUG] = jnp.ones((TM1, HEAD_DIM), BF16)
    mq_ref[...] = proj(C_MQ, C_MK).astype(BF16)
    mk = proj(C_MK, C_MV) * (M_DK ** -0.5)
    mkt_ref[...] = mk.T.astype(BF16)
    for c0 in range(C_MV, C_MO, PCH):
        mv_ref[:, c0 - C_MV:c0 - C_MV + PCH] = proj(c0, c0 + PCH).astype(BF16)
    for c0 in range(C_MO, C_G, PCH):
        mo_ref[:, c0 - C_MO:c0 - C_MO + PCH] = proj(c0, c0 + PCH)
    g_ref[...] = jnp.dot(h, wg_ref[...], preferred_element_type=F32) + bg_ref[...]


def _in_projection(x2, ctx2, sh1, sc1, g1, w_main, b_main, w_gate, b_gate, g_q, g_k, cos_t, sin_t):
    lat_idx = lambda i: (jnp.minimum(i, N_LAT_TILES1 - 1), 0)
    ctx_idx = lambda i: (jnp.maximum(i - N_LAT_TILES1, 0), 0)
    rope_idx = lambda i: (jnp.where(i < N_LAT_TILES1, i % TILES_PER_SEQ1, TILES_PER_SEQ1), 0)
    full = lambda shape: pl.BlockSpec(shape, lambda i: (0, 0))
    row_blk = lambda w: pl.BlockSpec((TM1, w), lambda i: (i, 0))
    out_shapes = [
        jax.ShapeDtypeStruct((T_ALL, ATT_WIDTH), BF16),
        jax.ShapeDtypeStruct((T_ALL, KV_W), BF16),
        jax.ShapeDtypeStruct((T_ALL, KV_HEADS * V_AUG), BF16),
        jax.ShapeDtypeStruct((T_ALL, MQK_W), BF16),
        jax.ShapeDtypeStruct((MQK_W, T_ALL), BF16),
        jax.ShapeDtypeStruct((T_ALL, M_WIDTH), BF16),
        jax.ShapeDtypeStruct((T_ALL, M_WIDTH), F32),
        jax.ShapeDtypeStruct((T_ALL, LANE), F32),
    ]
    out_specs = [
        row_blk(ATT_WIDTH), row_blk(KV_W), row_blk(KV_HEADS * V_AUG), row_blk(MQK_W),
        pl.BlockSpec((MQK_W, TM1), lambda i: (0, i)),
        row_blk(M_WIDTH), row_blk(M_WIDTH), row_blk(LANE),
    ]
    return pl.pallas_call(
        _inproj_kernel,
        grid=(N_TILES1,),
        in_specs=[
            pl.BlockSpec((TM1, D_MODEL), lat_idx),
            pl.BlockSpec((TM1, D_MODEL), ctx_idx),
            full((MOD_ROWS, D_MODEL)), full((MOD_ROWS, D_MODEL)), full((1, D_MODEL)),
            full((D_MODEL, C_G)), full((1, C_G)), full((D_MODEL, LANE)), full((1, LANE)),
            full((1, HEAD_DIM)), full((1, HEAD_DIM)),
            pl.BlockSpec((TM1, HEAD_DIM), rope_idx),
            pl.BlockSpec((TM1, HEAD_DIM), rope_idx),
        ],
        out_specs=out_specs,
        out_shape=out_shapes,
        scratch_shapes=[pltpu.VMEM((TM1, D_MODEL), BF16)],
        compiler_params=_params(1),
        name="in_proj",
    )(x2, ctx2, sh1, sc1, g1, w_main, b_main, w_gate, b_gate, g_q, g_k, cos_t, sin_t)


TQ = 256
CK = 512
GQ_W = GROUP * HEAD_DIM


def _attn_kernel(q_ref, kl_ref, vl_ref, kc_ref, vc_ref, o_ref):
    chunks = [(kl_ref, vl_ref, c * CK, CK) for c in range(SEQ // CK)]
    chunks.append((kc_ref, vc_ref, 0, CTX_LEN))
    for g in range(GROUP):
        qh = q_ref[:, g * HEAD_DIM:(g + 1) * HEAD_DIM]
        m = jnp.full((TQ, 1), -jnp.inf, F32)
        acc = jnp.zeros((TQ, V_AUG), F32)
        for kr, vr, st, sz in chunks:
            s = lax.dot_general(qh, kr[st:st + sz, :], (((1,), (1,)), ((), ())),
                                preferred_element_type=F32)
            m_new = jnp.maximum(m, jnp.max(s, axis=-1, keepdims=True))
            p = jnp.exp2(s - m_new)
            alpha = jnp.exp2(m - m_new)
            acc = alpha * acc + jnp.dot(p.astype(BF16), vr[st:st + sz, :],
                                        preferred_element_type=F32)
            m = m_new
        o_ref[:, g * HEAD_DIM:(g + 1) * HEAD_DIM] = (
            acc[:, :HEAD_DIM] / acc[:, HEAD_DIM:]).astype(BF16)


def _attention(q, k, v):
    nq = SEQ // TQ
    ctx_blk0 = T_LAT // CTX_LEN
    lat_k = pl.BlockSpec((SEQ, HEAD_DIM), lambda b, h, i: (b, h))
    lat_v = pl.BlockSpec((SEQ, V_AUG), lambda b, h, i: (b, h))
    ctx_k = pl.BlockSpec((CTX_LEN, HEAD_DIM), lambda b, h, i: (ctx_blk0 + b, h))
    ctx_v = pl.BlockSpec((CTX_LEN, V_AUG), lambda b, h, i: (ctx_blk0 + b, h))
    q_blk = pl.BlockSpec((TQ, GQ_W), lambda b, h, i: (b * nq + i, h))
    return pl.pallas_call(
        _attn_kernel,
        grid=(BATCH, KV_HEADS, nq),
        in_specs=[q_blk, lat_k, lat_v, ctx_k, ctx_v],
        out_specs=q_blk,
        out_shape=jax.ShapeDtypeStruct((T_LAT, ATT_WIDTH), BF16),
        compiler_params=_params(3),
        name="gqa_attention",
    )(q, k, v, k, v)


N_CHUNKS = SEQ // M_CHUNK
CTX_CHUNKS = CTX_LEN // M_CHUNK
N_CHAINS = 2 * M_HEADS


def _log_sigmoid(x):
    return jnp.minimum(x, 0.0) - jnp.log(1.0 + jnp.exp(-jnp.abs(x)))


def _mlstm_kernel(qf_ref, qb_ref, qc_ref, kf_ref, kb_ref, kc_ref, vf_ref, vb_ref, vc_ref,
                  gf_ref, gb_ref, gc_ref, hf_ref, hb_ref, c_scr, n_scr, m_scr):
    j = pl.program_id(1)
    ri = lax.broadcasted_iota(jnp.int32, (M_CHUNK, M_CHUNK), 0)
    ci = lax.broadcasted_iota(jnp.int32, (M_CHUNK, M_CHUNK), 1)
    lower = ci <= ri
    upper = ci >= ri
    tri_l = lower.astype(F32)
    tri_u = upper.astype(F32)

    def gate_tables(g, rev):
        gt = g.T
        mcol, mrow = (tri_u, tri_l) if rev else (tri_l, tri_u)
        bcol = jnp.dot(mcol, _log_sigmoid(g), precision=HIGHEST, preferred_element_type=F32)
        brow = jnp.dot(_log_sigmoid(gt), mrow, precision=HIGHEST, preferred_element_type=F32)
        return gt, bcol, brow

    def chain_step(chain, q, kt, v, tabs, rev, h_out):
        gt, bcol, brow = tabs
        icol = chain
        fcol = N_CHAINS + chain
        last = 0 if rev else M_CHUNK - 1
        b_col = bcol[:, fcol:fcol + 1]
        b_row = brow[fcol:fcol + 1, :]
        i_row = gt[icol:icol + 1, :]
        tot = brow[fcol:fcol + 1, last:last + 1]
        m_old = m_scr[chain][:1, :1]
        c_old = c_scr[chain]
        n_old = n_scr[chain]
        if h_out is not None:
            d = jnp.where(upper if rev else lower, b_col - b_row + i_row, -jnp.inf)
            inter = b_col + m_old
            m_t = jnp.maximum(inter, jnp.max(d, axis=-1, keepdims=True))
            w_intra = jnp.exp(d - m_t)
            w_inter = jnp.exp(inter - m_t)
            s = jnp.dot(q, kt, preferred_element_type=F32) * w_intra
            qc = jnp.dot(q, c_old.astype(BF16), preferred_element_type=F32)
            qn = jnp.dot(q, n_old.astype(BF16), preferred_element_type=F32)[:, :1]
            num = jnp.dot(s.astype(BF16), v, preferred_element_type=F32) + w_inter * qc
            den = jnp.sum(s, axis=-1, keepdims=True) + w_inter * qn
            h_out(num / jnp.maximum(jnp.abs(den), jnp.exp(-m_t)))
        g_row = tot - b_row + i_row
        m_new = jnp.maximum(tot + m_old, jnp.max(g_row, axis=-1, keepdims=True))
        w_state = jnp.exp(g_row - m_new)
        decay = jnp.exp(tot + m_old - m_new)
        kw = kt.astype(F32) * w_state
        c_scr[chain] = decay * c_old + jnp.dot(kw.astype(BF16), v, preferred_element_type=F32)
        n_new = decay * n_old[:, :1] + jnp.sum(kw, axis=-1, keepdims=True)
        n_scr[chain] = jnp.broadcast_to(n_new, (M_DK, LANE))
        m_scr[chain] = jnp.broadcast_to(m_new, (8, LANE))

    def run_chunk(q_ref, kt_ref, v_ref, g_ref, r0, rev, h_ref):
        tabs = gate_tables(g_ref[r0:r0 + M_CHUNK, :], rev)
        for hd in range(M_HEADS):
            chain = (M_HEADS if rev else 0) + hd
            q = q_ref[r0:r0 + M_CHUNK, hd * M_DK:(hd + 1) * M_DK]
            kt = kt_ref[hd * M_DK:(hd + 1) * M_DK, r0:r0 + M_CHUNK]
            v = v_ref[r0:r0 + M_CHUNK, hd * M_DV:(hd + 1) * M_DV]
            if h_ref is None:
                h_out = None
            else:
                def h_out(hv, hd=hd):
                    h_ref[:, hd * M_DV:(hd + 1) * M_DV] = hv
            chain_step(chain, q, kt, v, tabs, rev, h_out)

    @pl.when(j == 0)
    def _():
        c_scr[...] = jnp.zeros_like(c_scr)
        n_scr[...] = jnp.zeros_like(n_scr)
        m_scr[...] = jnp.zeros_like(m_scr)
        for cc in range(CTX_CHUNKS):
            run_chunk(qc_ref, kc_ref, vc_ref, gc_ref, cc * M_CHUNK, False, None)
            run_chunk(qc_ref, kc_ref, vc_ref, gc_ref, (CTX_CHUNKS - 1 - cc) * M_CHUNK, True, None)

    run_chunk(qf_ref, kf_ref, vf_ref, gf_ref, 0, False, hf_ref)
    run_chunk(qb_ref, kb_ref, vb_ref, gb_ref, 0, True, hb_ref)


def _mlstm(mq, mkt, mv, gates):
    ctx_blk0 = T_LAT // CTX_LEN
    fwd = lambda b, j: b * N_CHUNKS + j
    bwd = lambda b, j: b * N_CHUNKS + N_CHUNKS - 1 - j
    ctx = lambda b, j: ctx_blk0 + b

    def rows(width, tile, idx):
        return pl.BlockSpec((tile, width), lambda b, j: (idx(b, j), 0))

    def cols(tile, idx):
        return pl.BlockSpec((MQK_W, tile), lambda b, j: (0, idx(b, j)))

    h_shape = jax.ShapeDtypeStruct((T_LAT, M_WIDTH), F32)
    return pl.pallas_call(
        _mlstm_kernel,
        grid=(BATCH, N_CHUNKS),
        in_specs=[
            rows(MQK_W, M_CHUNK, fwd), rows(MQK_W, M_CHUNK, bwd), rows(MQK_W, CTX_LEN, ctx),
            cols(M_CHUNK, fwd), cols(M_CHUNK, bwd), cols(CTX_LEN, ctx),
            rows(M_WIDTH, M_CHUNK, fwd), rows(M_WIDTH, M_CHUNK, bwd), rows(M_WIDTH, CTX_LEN, ctx),
            rows(LANE, M_CHUNK, fwd), rows(LANE, M_CHUNK, bwd), rows(LANE, CTX_LEN, ctx),
        ],
        out_specs=[rows(M_WIDTH, M_CHUNK, fwd), rows(M_WIDTH, M_CHUNK, bwd)],
        out_shape=[h_shape, h_shape],
        scratch_shapes=[
            pltpu.VMEM((N_CHAINS, M_DK, M_DV), F32),
            pltpu.VMEM((N_CHAINS, M_DK, LANE), F32),
            pltpu.VMEM((N_CHAINS, 8, LANE), F32),
        ],
        compiler_params=_params(2),
        name="mlstm_scan",
    )(mq, mq, mq, mkt, mkt, mkt, mv, mv, mv, gates, gates, gates)


TM4 = 256
TILES_PER_SEQ4 = SEQ // TM4
NEG_BIG = -1e30


def _outproj_kernel(a_ref, hf_ref, hb_ref, mo_ref, x_ref, w_ref, gm_ref, gt1_ref, sh2_ref,
                    sc2_ref, g2_ref, wr_ref, br_ref,
                    x1_ref, h2_ref, ti_ref, tr_ref, tw_ref, tc_ref):
    i = pl.program_id(0)
    row = i // TILES_PER_SEQ4
    acc = jnp.dot(a_ref[...], w_ref[0:ATT_WIDTH, :], preferred_element_type=F32)
    for hd in range(M_HEADS):
        sl = slice(hd * M_DV, (hd + 1) * M_DV)
        hn = _rms(hf_ref[:, sl] + hb_ref[:, sl]) * gm_ref[:, sl]
        mo = mo_ref[:, sl]
        ym = hn / (1.0 + jnp.exp(-mo))
        acc += jnp.dot(ym.astype(BF16), w_ref[ATT_WIDTH + hd * M_DV:ATT_WIDTH + (hd + 1) * M_DV, :],
                       preferred_element_type=F32)
    x1 = x_ref[...] + gt1_ref[pl.ds(row, 1), :] * acc
    x1_ref[...] = x1
    h2 = (_rms(x1) * g2_ref[...]) * (1.0 + sc2_ref[pl.ds(row, 1), :]) + sh2_ref[pl.ds(row, 1), :]
    h2_ref[...] = h2
    logits =jnp.dot(h2, wr_ref[...], precision=HIGHEST, preferred_element_type=F32) + br_ref[...]
    lane = lax.broadcasted_iota(jnp.int32, (TM4, LANE), 1)
    vals, idxs = [], []
    for _ in range(TOP_K):
        mx = jnp.max(logits, axis=-1, keepdims=True)
        ik = jnp.min(jnp.where(logits == mx, lane, LANE), axis=-1, keepdims=True)
        vals.append(mx)
        idxs.append(ik)
        logits = jnp.where(lane == ik, -jnp.inf, logits)
    es = [jnp.exp(vk - vals[0]) for vk in vals]
    tot = es[0] + es[1] + es[2] + es[3]
    chosen = jnp.zeros((TM4, LANE), F32)
    for kk in range(TOP_K):
        chosen = jnp.where(lane == idxs[kk], 1.0, chosen)
    ri = lax.broadcasted_iota(jnp.int32, (TM4, TM4), 0)
    ci = lax.broadcasted_iota(jnp.int32, (TM4, TM4), 1)
    before = jnp.where(ci < ri, 1.0, 0.0).astype(BF16)
    earlier = jnp.dot(before, chosen.astype(BF16), preferred_element_type=F32)
    ti = jnp.zeros((TM4, LANE), jnp.int32)
    tr = jnp.zeros((TM4, LANE), jnp.int32)
    tw = jnp.zeros((TM4, LANE), F32)
    for kk in range(TOP_K):
        rk = jnp.sum(jnp.where(lane == idxs[kk], earlier, 0.0), axis=-1, keepdims=True)
        ti = jnp.where(lane == kk, idxs[kk], ti)
        tr = jnp.where(lane == kk, rk.astype(jnp.int32), tr)
        tw = jnp.where(lane == kk, es[kk] / tot, tw)
    ti_ref[...] = ti
    tr_ref[...] = tr
    tw_ref[...] = tw
    tc_ref[...] = jnp.sum(chosen, axis=0, keepdims=True).astype(jnp.int32)


def _out_projection(a_lat, hf, hb, mo, x2, w_out_b, g_mlstm, gt1, sh2, sc2, g2, wr_p, br_p):
    full = lambda shape: pl.BlockSpec(shape, lambda i: (0, 0))
    row_blk = lambda w: pl.BlockSpec((TM4, w), lambda i: (i, 0))
    return pl.pallas_call(
        _outproj_kernel,
        grid=(T_LAT // TM4,),
        in_specs=[
            row_blk(ATT_WIDTH), row_blk(M_WIDTH), row_blk(M_WIDTH), row_blk(M_WIDTH),
            row_blk(D_MODEL), full((ATT_WIDTH + M_WIDTH, D_MODEL)), full((1, M_WIDTH)),
            full((MOD_ROWS, D_MODEL)), full((MOD_ROWS, D_MODEL)), full((MOD_ROWS, D_MODEL)),
            full((1, D_MODEL)), full((D_MODEL, LANE)), full((1, LANE)),
        ],
        out_specs=[row_blk(D_MODEL), row_blk(D_MODEL), row_blk(LANE), row_blk(LANE), row_blk(LANE),
                   pl.BlockSpec((None, 1, LANE), lambda i: (i, 0, 0))],
        out_shape=[
            jax.ShapeDtypeStruct((T_LAT, D_MODEL), F32),
            jax.ShapeDtypeStruct((T_LAT, D_MODEL), F32),
            jax.ShapeDtypeStruct((T_LAT, LANE), jnp.int32),
            jax.ShapeDtypeStruct((T_LAT, LANE), jnp.int32),
            jax.ShapeDtypeStruct((T_LAT, LANE), F32),
            jax.ShapeDtypeStruct((T_LAT // TM4, 1, LANE), jnp.int32),
        ],
        compiler_params=_params(1),
        name="out_proj_router",
    )(a_lat, hf, hb, mo, x2, w_out_b, g_mlstm, gt1, sh2, sc2, g2, wr_p, br_p)


MB = 128
N_ASSIGN = T_LAT * TOP_K
N_BLOCKS = N_ASSIGN // MB + N_EXPERTS
P_ROWS = N_BLOCKS * MB
N_PADS = P_ROWS - N_ASSIGN
SB = 10
W_MAX = N_EXPERTS + N_BLOCKS // SB
XROWS = SB * MB


def _routing_tables(top_idx, tile_rank, tile_counts):
    n_tiles = tile_counts.shape[0]
    counts = jnp.sum(tile_counts, axis=0)
    tile_base = jnp.cumsum(tile_counts, axis=0) - tile_counts
    nblk = (counts + MB - 1) // MB
    blk_start = jnp.cumsum(nblk) - nblk
    n_valid_blk = jnp.sum(nblk)
    offs = blk_start[None, :] * MB + tile_base
    e4 = top_idx.reshape(n_tiles, -1, TOP_K, 1)
    hit = e4 == jnp.arange(N_EXPERTS, dtype=jnp.int32)
    dest = jnp.sum(jnp.where(hit, offs[:, None, None, :], 0), axis=-1)
    dest = (dest.reshape(T_LAT, TOP_K) + tile_rank).reshape(N_ASSIGN).astype(jnp.int32)
    nwork = (nblk + SB - 1) // SB
    wend = jnp.cumsum(nwork)
    wstart = wend - nwork
    n_items = wend[-1]
    wid = jnp.arange(W_MAX, dtype=jnp.int32)
    valid = wid < n_items
    we = jnp.minimum(jnp.sum((wend[None, :] <= wid[:, None]).astype(jnp.int32), axis=1),
                     N_EXPERTS - 1)
    local = wid - wstart[we]
    work_blk = jnp.where(valid, blk_start[we] + local * SB, 0)
    work_n = jnp.where(valid, jnp.clip(nblk[we] - local * SB, 0, SB), 0)
    last_e = we[jnp.maximum(n_items - 1, 0)]
    work_e = jnp.where(valid, we, last_e)
    i32 = lambda a: a.astype(jnp.int32)
    return (dest, i32(counts), i32(blk_start), i32(nblk), i32(n_valid_blk)[None],
            i32(work_e), i32(work_blk), i32(work_n))


TMD = 256


def _dispatch_kernel(dest_ref, cnt_ref, bs_ref, nb_ref, nvb_ref, h2_ref, x_hbm, zrow, sem, psem):
    i = pl.program_id(0)

    @pl.when(i == 0)
    def _():
        zrow[...] = jnp.zeros_like(zrow)

        def zero_row(r, carry):
            pltpu.make_async_copy(zrow.at[pl.ds(0, 1)], x_hbm.at[pl.ds(r, 1)], psem).start()
            return carry

        def expert(e, carry):
            first = bs_ref[e] * MB
            lax.fori_loop(first + cnt_ref[e], first + nb_ref[e] * MB, zero_row, 0)
            return carry
        lax.fori_loop(0, N_EXPERTS, expert, 0)
        lax.fori_loop(nvb_ref[0] * MB, P_ROWS, zero_row, 0)

    base = i * (TMD * TOP_K)

    def issue(t, carry):
        for kk in range(TOP_K):
            d = dest_ref[base + t * TOP_K + kk]
            pltpu.make_async_copy(h2_ref.at[pl.ds(t, 1)], x_hbm.at[pl.ds(d, 1)], sem).start()
        return carry
    lax.fori_loop(0, TMD, issue, 0)
    for kk in range(TOP_K):
        pltpu.make_async_copy(h2_ref, x_hbm.at[pl.ds(0, TMD)], sem).wait()

    @pl.when(i == 0)
    def _():
        for _ in range(N_PADS // TMD):
            pltpu.make_async_copy(h2_ref, x_hbm.at[pl.ds(0, TMD)], psem).wait()


def _dispatch(h2, dest, counts, blk_start, nblk, n_valid_blk):
    grid_spec = pltpu.PrefetchScalarGridSpec(
        num_scalar_prefetch=5,
        grid=(T_LAT // TMD,),
        in_specs=[pl.BlockSpec((TMD, D_MODEL), lambda i, *_: (i, 0))],
        out_specs=pl.BlockSpec(memory_space=pl.ANY),
        scratch_shapes=[
            pltpu.VMEM((8, D_MODEL), F32),
            pltpu.SemaphoreType.DMA(()),
            pltpu.SemaphoreType.DMA(()),
        ],
    )
    return pl.pallas_call(
        _dispatch_kernel,
        grid_spec=grid_spec,
        out_shape=jax.ShapeDtypeStruct((P_ROWS, D_MODEL), F32),
        compiler_params=_params(1),
        name="moe_dispatch",
    )(dest, counts, blk_start, nblk, n_valid_blk, h2)


TF = 512
NF = D_FF // TF
MC = 2 * MB


def _moe_kernel(we_ref, wb_ref, wn_ref, nvb_ref,
                x_hbm, w1g_ref, w1u_ref, w2_ref, b1g_ref, b1u_ref, b2_ref,
                y_hbm, xbuf, xb16, acc, xsem, ysem):
    w = pl.program_id(0)
    f = pl.program_id(1)
    nsub = wn_ref[w]
    r0 = wb_ref[w] * MB

    def x_copy(item, s):
        src = x_hbm.at[pl.ds(wb_ref[item] * MB + s * MB, MB)]
        return pltpu.make_async_copy(src, xbuf.at[pl.ds(s * MB, MB)], xsem)

    def start_x(item):
        def body(s, carry):
            x_copy(item, s).start()
            return carry
        lax.fori_loop(0, wn_ref[item], body, 0)

    @pl.when(jnp.logical_and(w == 0, f == 0))
    def _():
        start_x(0)

    @pl.when(f == 0)
    def _():
        def wait(s, carry):
            x_copy(w, s).wait()
            return carry
        lax.fori_loop(0, nsub, wait, 0)

        def body(s, carry):
            rs = pl.ds(pl.multiple_of(s * MB, MB), MB)
            xb16[rs, :] = xbuf[rs, :].astype(BF16)
            acc[rs, :] = jnp.broadcast_to(b2_ref[...], (MB, D_MODEL))
            return carry
        lax.fori_loop(0, nsub, body, 0)

    @pl.when(jnp.logical_and(f == 1, w + 1 < W_MAX))
    def _():
        start_x(w + 1)

    def y_copy(s):
        rs = pl.ds(pl.multiple_of(s * MB, MB), MB)
        return pltpu.make_async_copy(acc.at[rs], y_hbm.at[pl.ds(r0 + s * MB, MB)], ysem)

    def chunk(row, m):
        rs = pl.ds(row, m)
        xb = xb16[rs, :]
        gate = jnp.dot(xb, w1g_ref[...], preferred_element_type=F32) + b1g_ref[...]
        up = jnp.dot(xb, w1u_ref[...], preferred_element_type=F32) + b1u_ref[...]
        gate = jnp.minimum(gate, SWIGLU_LIMIT)
        up = jnp.clip(up, -SWIGLU_LIMIT, SWIGLU_LIMIT)
        glu = gate / (1.0 + jnp.exp(-SWIGLU_ALPHA * gate))
        acc[rs, :] += jnp.dot((up + 1.0) * glu, w2_ref[...], preferred_element_type=F32)

    nquad = nsub // 4
    rem = nsub - nquad * 4

    def quad(c, carry):
        row = pl.multiple_of(c * (2 * MC), 2 * MC)
        chunk(row, MC)
        chunk(row + MC, MC)
        return carry
    lax.fori_loop(0, nquad, quad, 0)

    @pl.when(rem >= 2)
    def _():
        chunk(pl.multiple_of(nquad * (2 * MC), MC), MC)

    @pl.when(rem % 2 == 1)
    def _():
        chunk(pl.multiple_of((nsub - 1) * MB, MB), MB)

    @pl.when(f == NF - 1)
    def _():
        def start(s, carry):
            y_copy(s).start()
            return carry
        lax.fori_loop(0, nsub, start, 0)

        def wait(s, carry):
            y_copy(s).wait()
            return carry
        lax.fori_loop(0, nsub, wait, 0)

    @pl.when(jnp.logical_and(w == W_MAX - 1, f == NF - 1))
    def _():
        acc[0:MB, :] = jnp.zeros((MB, D_MODEL), F32)

        def tail_copy(bk):
            return pltpu.make_async_copy(acc.at[0:MB], y_hbm.at[pl.ds(bk * MB, MB)], ysem)

        def start(bk, carry):
            tail_copy(bk).start()
            return carry
        lax.fori_loop(nvb_ref[0], N_BLOCKS, start, 0)

        def wait(bk, carry):
            tail_copy(bk).wait()
            return carry
        lax.fori_loop(nvb_ref[0], N_BLOCKS, wait, 0)


def _moe(x_sorted, w1, b1, w2, b2, work_e, work_blk, work_n, n_valid_blk):
    b1r = b1.reshape(N_EXPERTS, 1, 2 * D_FF)
    b2r = b2.reshape(N_EXPERTS, 1, D_MODEL)
    grid_spec = pltpu.PrefetchScalarGridSpec(
        num_scalar_prefetch=4,
        grid=(W_MAX, NF),
        in_specs=[
            pl.BlockSpec(memory_space=pl.ANY),
            pl.BlockSpec((None, D_MODEL, TF), lambda w, f, we, *_: (we[w], 0, f)),
            pl.BlockSpec((None, D_MODEL, TF), lambda w, f, we, *_: (we[w], 0, NF + f)),
            pl.BlockSpec((None, TF, D_MODEL), lambda w, f, we, *_: (we[w], f, 0)),
            pl.BlockSpec((None, 1, TF), lambda w, f, we, *_: (we[w], 0, f)),
            pl.BlockSpec((None, 1, TF), lambda w, f, we, *_: (we[w], 0, NF + f)),
            pl.BlockSpec((None, 1, D_MODEL), lambda w, f, we, *_: (we[w], 0, 0)),
        ],
        out_specs=pl.BlockSpec(memory_space=pl.ANY),
        scratch_shapes=[
            pltpu.VMEM((XROWS, D_MODEL), F32),
            pltpu.VMEM((XROWS, D_MODEL), BF16),
            pltpu.VMEM((XROWS, D_MODEL), F32),
            pltpu.SemaphoreType.DMA(()),
            pltpu.SemaphoreType.DMA(()),
        ],
    )
    return pl.pallas_call(
        _moe_kernel,
        grid_spec=grid_spec,
        out_shape=jax.ShapeDtypeStruct((P_ROWS, D_MODEL), F32),
        compiler_params=_params(2),
        name="moe_experts",
    )(work_e, work_blk, work_n, n_valid_blk, x_sorted, w1, w1, w2, b1r, b1r, b2r)


TM7 = 256
N_TILES7 = T_LAT // TM7
TILES_PER_SEQ7 = SEQ // TM7


def _final_kernel(pos_ref, x1_ref, tw_ref, gt2_ref, gf_ref, y_hbm, o_ref, ybuf, sem):
    i = pl.program_id(0)
    slot = i % 2

    def start_gather(tile, slot_):
        base = tile * (TM7 * TOP_K)

        def body(t, carry):
            for kk in range(TOP_K):
                p = pos_ref[base + t * TOP_K + kk]
                pltpu.make_async_copy(y_hbm.at[pl.ds(p, 1)], ybuf.at[slot_, kk, pl.ds(t, 1)],
                                      sem.at[slot_]).start()
            return carry
        lax.fori_loop(0, TM7, body, 0)

    @pl.when(i == 0)
    def _():
        start_gather(0, 0)

    @pl.when(i + 1 < N_TILES7)
    def _():
        start_gather(i + 1, 1 - slot)

    for kk in range(TOP_K):
        pltpu.make_async_copy(y_hbm.at[pl.ds(0, TM7)], ybuf.at[slot, kk], sem.at[slot]).wait()

    row = i // TILES_PER_SEQ7
    tw = tw_ref[...]
    mix = jnp.zeros((TM7, D_MODEL), F32)
    for kk in range(TOP_K):
        mix += ybuf[slot, kk] * tw[:, kk:kk + 1]
    xo = x1_ref[...] + gt2_ref[pl.ds(row, 1), :] * mix
    o_ref[...] = _rms(xo) * gf_ref[...]


def _final(dest, x1, top_w, gt2, g_final, y_sorted):
    grid_spec = pltpu.PrefetchScalarGridSpec(
        num_scalar_prefetch=1,
        grid=(N_TILES7,),
        in_specs=[
            pl.BlockSpec((TM7, D_MODEL), lambda i, *_: (i, 0)),
            pl.BlockSpec((TM7, LANE), lambda i, *_: (i, 0)),
            pl.BlockSpec((MOD_ROWS, D_MODEL), lambda i, *_: (0, 0)),
            pl.BlockSpec((1, D_MODEL), lambda i, *_: (0, 0)),
            pl.BlockSpec(memory_space=pl.ANY),
        ],
        out_specs=pl.BlockSpec((TM7, D_MODEL), lambda i, *_: (i, 0)),
        scratch_shapes=[
            pltpu.VMEM((2, TOP_K, TM7, D_MODEL), F32),
            pltpu.SemaphoreType.DMA((2,)),
        ],
    )
    return pl.pallas_call(
        _final_kernel,
        grid_spec=grid_spec,
        out_shape=jax.ShapeDtypeStruct((T_LAT, D_MODEL), F32),
        compiler_params=_params(1),
        name="combine_final_norm",
    )(dest, x1, top_w, gt2, g_final, y_sorted)


def _rope_tables():
    rows = SEQ // GRID_W
    row = jnp.repeat(jnp.arange(rows, dtype=F32), GRID_W)
    col = jnp.tile(jnp.arange(GRID_W, dtype=F32), rows)
    inv = ROPE_THETA ** (-jnp.arange(0, AXIS_DIM, 2, dtype=F32) / AXIS_DIM)
    ang = jnp.concatenate([row[:, None] * inv, col[:, None] * inv], axis=-1)
    cos = jnp.repeat(jnp.cos(ang), 2, axis=-1)
    sin = jnp.repeat(jnp.sin(ang), 2, axis=-1)
    sign = jnp.tile(jnp.array([-1.0, 1.0], F32), HEAD_DIM // 2)
    cos_t = jnp.concatenate([cos, jnp.ones((TM1, HEAD_DIM), F32)], axis=0)
    sin_t = jnp.concatenate([sin * sign, jnp.zeros((TM1, HEAD_DIM), F32)], axis=0)
    return cos_t, sin_t


def kernel(x, c, ctx, c_ctx, w_mod, b_mod, g_norm1, w_in, b_in, g_q, g_k, g_mlstm, w_out,
           g_norm2, w_router, b_router, w1, b1, w2, b2, g_final):
    x2 = x.reshape(T_LAT, D_MODEL)
    ctx2 = ctx.reshape(T_CTX, D_MODEL)
    cos_t, sin_t = _rope_tables()
    assert w_mod.shape[0] == 1
    l = 0
    c_rows = jnp.concatenate(
        [c, c_ctx[None, :], jnp.zeros((MOD_ROWS - BATCH - 1, D_MODEL), F32)], axis=0)
    mod = _modulation(c_rows, w_mod[l], b_mod[l][None, :])
    sh1, sc1, gt1, sh2, sc2, gt2 = [mod[:, k * D_MODEL:(k + 1) * D_MODEL]
                                    for k in range(N_ADALN)]
    pad = IN_COLS_PAD - IN_COLS
    w_main = w_in[l][:, :C_G].astype(BF16)
    w_gate = jnp.pad(w_in[l][:, C_G:], ((0, 0), (0, pad))).astype(BF16)
    b_main = b_in[l][None, :C_G]
    b_gate = jnp.pad(b_in[l][C_G:], (0, pad))[None, :]
    q, k, v, mq, mkt, mv, mo, gates = _in_projection(
        x2, ctx2, sh1, sc1, g_norm1[l][None, :], w_main, b_main, w_gate, b_gate,
        g_q[l][None, :], g_k[l][None, :], cos_t, sin_t)
    a_lat = _attention(q, k, v)
    hf, hb = _mlstm(mq, mkt, mv, gates)
    wr_p = jnp.pad(w_router[l], ((0, 0), (0, LANE - N_EXPERTS)))
    br_p = jnp.pad(b_router[l], (0, LANE - N_EXPERTS), constant_values=NEG_BIG)[None, :]
    x1, h2, top_i, top_r, top_w, tile_cnt = _out_projection(
        a_lat, hf, hb, mo, x2, w_out[l].astype(BF16), g_mlstm[l][None, :],
        gt1, sh2, sc2, g_norm2[l][None, :], wr_p, br_p)
    dest, counts, blk_start, nblk, n_valid_blk, work_e, work_blk, work_n = _routing_tables(
        top_i[:, :TOP_K], top_r[:, :TOP_K], tile_cnt[:, 0, :N_EXPERTS])
    x_sorted = _dispatch(h2, dest, counts, blk_start, nblk, n_valid_blk)
    y_sorted = _moe(x_sorted, w1[l], b1[l], w2[l], b2[l], work_e, work_blk, work_n, n_valid_blk)
    out = _final(dest, x1, top_w, gt2, g_final[None, :], y_sorted)
    return out.reshape(BATCH, SEQ, D_MODEL)
```

```python
import jax
import jax.numpy as jnp
from jax import lax
from jax.experimental import pallas as pl
from jax.experimental.pallas import tpu as pltpu

F32 = jnp.float32
BF16 = jnp.bfloat16
HIGHEST = lax.Precision.HIGHEST

D_MODEL = 2048
BATCH = 2
SEQ = 4096
CTX_LEN = 256
GRID_W = 64
N_ADALN = 6
EPS = 1e-6
N_HEADS = 8
KV_HEADS = 2
GROUP = N_HEADS // KV_HEADS
HEAD_DIM = 128
AXIS_DIM = HEAD_DIM // 2
ROPE_THETA = 10000.0
ATT_WIDTH = N_HEADS * HEAD_DIM
ATT_SCALE = HEAD_DIM ** -0.5
M_HEADS = 4
M_DK = 128
M_DV = 256
M_CHUNK = 128
M_WIDTH = M_HEADS * M_DV
N_EXPERTS = 32
TOP_K = 4
D_FF = D_MODEL
SWIGLU_LIMIT = 7.0
SWIGLU_ALPHA = 1.702

T_LAT = BATCH * SEQ
T_CTX = BATCH * CTX_LEN
T_ALL = T_LAT + T_CTX
KV_W = KV_HEADS * HEAD_DIM
MQK_W = M_HEADS * M_DK
LANE = 128

C_Q = 0
C_K = C_Q + ATT_WIDTH
C_V = C_K + KV_W
C_MQ = C_V + KV_W
C_MK = C_MQ + MQK_W
C_MV = C_MK + MQK_W
C_MO = C_MV + M_WIDTH
C_G = C_MO + M_WIDTH
IN_COLS = C_G + 4 * M_HEADS
IN_COLS_PAD = C_G + LANE

VMEM_LIMIT = 56 * 1024 * 1024


def _params(n_axes):
    return pltpu.CompilerParams(
        dimension_semantics=("arbitrary",) * n_axes, vmem_limit_bytes=VMEM_LIMIT)


def _rms(x):
    return x * lax.rsqrt(jnp.mean(x * x, axis=-1, keepdims=True) + EPS)


MOD_ROWS = 8
MOD_TN = 1024


def _mod_kernel(c_ref, w_ref, b_ref, o_ref):
    c = c_ref[...]
    s = c / (1.0 + jnp.exp(-c))
    o_ref[...] = jnp.dot(s, w_ref[...], preferred_element_type=F32) + b_ref[...]


def _modulation(c_rows, w_mod, b_mod):
    n = w_mod.shape[1]
    return pl.pallas_call(
        _mod_kernel,
        grid=(n // MOD_TN,),
        in_specs=[
            pl.BlockSpec((MOD_ROWS, D_MODEL), lambda j: (0, 0)),
            pl.BlockSpec((D_MODEL, MOD_TN), lambda j: (0, j)),
            pl.BlockSpec((1, MOD_TN), lambda j: (0, j)),
        ],
        out_specs=pl.BlockSpec((MOD_ROWS, MOD_TN), lambda j: (0, j)),
        out_shape=jax.ShapeDtypeStruct((MOD_ROWS, n), F32),
        compiler_params=_params(1),
        name="adaln_mod",
    )(c_rows, w_mod, b_mod)


TM1 = 256
N_LAT_TILES1 = T_LAT // TM1
N_TILES1 = T_ALL // TM1
TILES_PER_SEQ1 = SEQ // TM1
PCH = 512
LOG2E = 1.4426950408889634
Q_SCALE = ATT_SCALE * LOG2E
V_AUG = 2 * HEAD_DIM


def _inproj_kernel(x_ref, ctx_ref, sh_ref, sc_ref, g1_ref, w_ref, b_ref, wg_ref, bg_ref,
                   gq_ref, gk_ref, cos_ref, sin_ref,
                   q_ref, k_ref, v_ref, mq_ref, mkt_ref, mv_ref, mo_ref, g_ref, h_scr):
    i = pl.program_id(0)
    row = jnp.where(i < N_LAT_TILES1, i // TILES_PER_SEQ1, BATCH)
    sc = sc_ref[pl.ds(row, 1), :]
    sh = sh_ref[pl.ds(row, 1), :]

    def norm_mod(xv):
        return (_rms(xv) * g1_ref[...]) * (1.0 + sc) + sh

    @pl.when(i < N_LAT_TILES1)
    def _():
        h_scr[...] = norm_mod(x_ref[...]).astype(BF16)

    @pl.when(i >= N_LAT_TILES1)
    def _():
        h_scr[...] = norm_mod(ctx_ref[...]).astype(BF16)

    h = h_scr[...]

    def proj(c0, c1):
        return jnp.dot(h, w_ref[:, c0:c1], preferred_element_type=F32) + b_ref[:, c0:c1]

    cos_f = cos_ref[...]
    sin_s = sin_ref[...]
    lane = lax.broadcasted_iota(jnp.int32, (TM1, HEAD_DIM), 1)
    even = (lane % 2) == 0

    def head_norm_rope(a, g):
        y = _rms(a) * g
        nxt = pltpu.roll(y, HEAD_DIM - 1, 1)
        prv = pltpu.roll(y, 1, 1)
        return y * cos_f + jnp.where(even, nxt, prv) * sin_s

    for c0 in range(C_Q, C_K, PCH):
        acc = proj(c0, c0 + PCH)
        for hh in range(PCH // HEAD_DIM):
            a = acc[:, hh * HEAD_DIM:(hh + 1) * HEAD_DIM]
            r = head_norm_rope(a, gq_ref[...]) * Q_SCALE
            q_ref[:, c0 + hh * HEAD_DIM:c0 + (hh + 1) * HEAD_DIM] = r.astype(BF16)
    acc = proj(C_K, C_MQ)
    for hh in range(KV_HEADS):
        a = acc[:, hh * HEAD_DIM:(hh + 1) * HEAD_DIM]
        k_ref[:, hh * HEAD_DIM:(hh + 1) * HEAD_DIM] = head_norm_rope(a, gk_ref[...]).astype(BF16)
    for hh in range(KV_HEADS):
        v_ref[:, hh * V_AUG:hh * V_AUG + HEAD_DIM] = (
            acc[:, KV_W + hh * HEAD_DIM:KV_W + (hh + 1) * HEAD_DIM].astype(BF16))
        v_ref[:, hh * V_AUG + HEAD_DIM:(hh + 1) * V_AUG] = jnp.ones((TM1, HEAD_DIM), BF16)
    mq_ref[...] = proj(C_MQ, C_MK).astype(BF16)
    mk = proj(C_MK, C_MV) * (M_DK ** -0.5)
    mkt_ref[...] = mk.T.astype(BF16)
    for c0 in range(C_MV, C_MO, PCH):
        mv_ref[:, c0 - C_MV:c0 - C_MV + PCH] = proj(c0, c0 + PCH).astype(BF16)
    for c0 in range(C_MO, C_G, PCH):
        mo_ref[:, c0 - C_MO:c0 - C_MO + PCH] = proj(c0, c0 + PCH)
    g_ref[...] = jnp.dot(h, wg_ref[...], preferred_element_type=F32) + bg_ref[...]


def _in_projection(x2, ctx2, sh1, sc1, g1, w_main, b_main, w_gate, b_gate, g_q, g_k, cos_t, sin_t):
    lat_idx = lambda i: (jnp.minimum(i, N_LAT_TILES1 - 1), 0)
    ctx_idx = lambda i: (jnp.maximum(i - N_LAT_TILES1, 0), 0)
    rope_idx = lambda i: (jnp.where(i < N_LAT_TILES1, i % TILES_PER_SEQ1, TILES_PER_SEQ1), 0)
    full = lambda shape: pl.BlockSpec(shape, lambda i: (0, 0))
    row_blk = lambda w: pl.BlockSpec((TM1, w), lambda i: (i, 0))
    out_shapes = [
        jax.ShapeDtypeStruct((T_ALL, ATT_WIDTH), BF16),
        jax.ShapeDtypeStruct((T_ALL, KV_W), BF16),
        jax.ShapeDtypeStruct((T_ALL, KV_HEADS * V_AUG), BF16),
        jax.ShapeDtypeStruct((T_ALL, MQK_W), BF16),
        jax.ShapeDtypeStruct((MQK_W, T_ALL), BF16),
        jax.ShapeDtypeStruct((T_ALL, M_WIDTH), BF16),
        jax.ShapeDtypeStruct((T_ALL, M_WIDTH), F32),
        jax.ShapeDtypeStruct((T_ALL, LANE), F32),
    ]
    out_specs = [
        row_blk(ATT_WIDTH), row_blk(KV_W), row_blk(KV_HEADS * V_AUG), row_blk(MQK_W),
        pl.BlockSpec((MQK_W, TM1), lambda i: (0, i)),
        row_blk(M_WIDTH), row_blk(M_WIDTH), row_blk(LANE),
    ]
    return pl.pallas_call(
        _inproj_kernel,
        grid=(N_TILES1,),
        in_specs=[
            pl.BlockSpec((TM1, D_MODEL), lat_idx),
            pl.BlockSpec((TM1, D_MODEL), ctx_idx),
            full((MOD_ROWS, D_MODEL)), full((MOD_ROWS, D_MODEL)), full((1, D_MODEL)),
            full((D_MODEL, C_G)), full((1, C_G)), full((D_MODEL, LANE)), full((1, LANE)),
            full((1, HEAD_DIM)), full((1, HEAD_DIM)),
            pl.BlockSpec((TM1, HEAD_DIM), rope_idx),
            pl.BlockSpec((TM1, HEAD_DIM), rope_idx),
        ],
        out_specs=out_specs,
        out_shape=out_shapes,
        scratch_shapes=[pltpu.VMEM((TM1, D_MODEL), BF16)],
        compiler_params=_params(1),
        name="in_proj",
    )(x2, ctx2, sh1, sc1, g1, w_main, b_main, w_gate, b_gate, g_q, g_k, cos_t, sin_t)


TQ = 256
CK = 512
GQ_W = GROUP * HEAD_DIM


def _attn_kernel(q_ref, kl_ref, vl_ref, kc_ref, vc_ref, o_ref):
    chunks = [(kl_ref, vl_ref, c * CK, CK) for c in range(SEQ // CK)]
    chunks.append((kc_ref, vc_ref, 0, CTX_LEN))
    for g in range(GROUP):
        qh = q_ref[:, g * HEAD_DIM:(g + 1) * HEAD_DIM]
        m = jnp.full((TQ, 1), -jnp.inf, F32)
        acc = jnp.zeros((TQ, V_AUG), F32)
        for kr, vr, st, sz in chunks:
            s = lax.dot_general(qh, kr[st:st + sz, :], (((1,), (1,)), ((), ())),
                                preferred_element_type=F32)
            m_new = jnp.maximum(m, jnp.max(s, axis=-1, keepdims=True))
            p = jnp.exp2(s - m_new)
            alpha = jnp.exp2(m - m_new)
            acc = alpha * acc + jnp.dot(p.astype(BF16), vr[st:st + sz, :],
                                        preferred_element_type=F32)
            m = m_new
        o_ref[:, g * HEAD_DIM:(g + 1) * HEAD_DIM] = (
            acc[:, :HEAD_DIM] / acc[:, HEAD_DIM:]).astype(BF16)


def _attention(q, k, v):
    nq = SEQ // TQ
    ctx_blk0 = T_LAT // CTX_LEN
    lat_k = pl.BlockSpec((SEQ, HEAD_DIM), lambda b, h, i: (b, h))
    lat_v = pl.BlockSpec((SEQ, V_AUG), lambda b, h, i: (b, h))
    ctx_k = pl.BlockSpec((CTX_LEN, HEAD_DIM), lambda b, h, i: (ctx_blk0 + b, h))
    ctx_v = pl.BlockSpec((CTX_LEN, V_AUG), lambda b, h, i: (ctx_blk0 + b, h))
    q_blk = pl.BlockSpec((TQ, GQ_W), lambda b, h, i: (b * nq + i, h))
    return pl.pallas_call(
        _attn_kernel,
        grid=(BATCH, KV_HEADS, nq),
        in_specs=[q_blk, lat_k, lat_v, ctx_k, ctx_v],
        out_specs=q_blk,
        out_shape=jax.ShapeDtypeStruct((T_LAT, ATT_WIDTH), BF16),
        compiler_params=_params(3),
        name="gqa_attention",
    )(q, k, v, k, v)


N_CHUNKS = SEQ // M_CHUNK
CTX_CHUNKS = CTX_LEN // M_CHUNK
N_CHAINS = 2 * M_HEADS


def _log_sigmoid(x):
    return jnp.minimum(x, 0.0) - jnp.log(1.0 + jnp.exp(-jnp.abs(x)))


def _mlstm_kernel(qf_ref, qb_ref, qc_ref, kf_ref, kb_ref, kc_ref, vf_ref, vb_ref, vc_ref,
                  gf_ref, gb_ref, gc_ref, hf_ref, hb_ref, *state):
    c_scr = state[0:N_CHAINS]
    n_scr = state[N_CHAINS:2 * N_CHAINS]
    m_scr = state[2 * N_CHAINS:3 * N_CHAINS]
    j = pl.program_id(1)
    ri = lax.broadcasted_iota(jnp.int32, (M_CHUNK, M_CHUNK), 0)
    ci = lax.broadcasted_iota(jnp.int32, (M_CHUNK, M_CHUNK), 1)
    lower = ci <= ri
    upper = ci >= ri
    tri_l = lower.astype(F32)
    tri_u = upper.astype(F32)

    def gate_tables(g, rev):
        gt = g.T
        mcol, mrow = (tri_u, tri_l) if rev else (tri_l, tri_u)
        bcol = jnp.dot(mcol, _log_sigmoid(g), precision=HIGHEST, preferred_element_type=F32)
        brow = jnp.dot(_log_sigmoid(gt), mrow, precision=HIGHEST, preferred_element_type=F32)
        return gt, bcol, brow

    def chain_step(chain, q, kt, v, tabs, rev, h_out):
        gt, bcol, brow = tabs
        icol = chain
        fcol = N_CHAINS + chain
        last = 0 if rev else M_CHUNK - 1
        b_col = bcol[:, fcol:fcol + 1]
        b_row = brow[fcol:fcol + 1, :]
        i_row = gt[icol:icol + 1, :]
        tot = brow[fcol:fcol + 1, last:last + 1]
        m_old = m_scr[chain][:1, :1]
        c_old = c_scr[chain][...]
        n_old = n_scr[chain][...]
        if h_out is not None:
            d = jnp.where(upper if rev else lower, b_col - b_row + i_row, -jnp.inf)
            inter = b_col + m_old
            m_t = jnp.maximum(inter, jnp.max(d, axis=-1, keepdims=True))
            w_intra = jnp.exp(d - m_t)
            w_inter = jnp.exp(inter - m_t)
            s = jnp.dot(q, kt, preferred_element_type=F32) * w_intra
            qc = jnp.dot(q, c_old.astype(BF16), preferred_element_type=F32)
            qn = jnp.dot(q, n_old.astype(BF16), preferred_element_type=F32)[:, :1]
            num = jnp.dot(s.astype(BF16), v, preferred_element_type=F32) + w_inter * qc
            den = jnp.sum(s, axis=-1, keepdims=True) + w_inter * qn
            h_out(num / jnp.maximum(jnp.abs(den), jnp.exp(-m_t)))
        g_row = tot - b_row + i_row
        m_new = jnp.maximum(tot + m_old, jnp.max(g_row, axis=-1, keepdims=True))
        w_state = jnp.exp(g_row - m_new)
        decay = jnp.exp(tot + m_old - m_new)
        kw = kt.astype(F32) * w_state
        c_scr[chain][...] = decay * c_old + jnp.dot(kw.astype(BF16), v, preferred_element_type=F32)
        n_new = decay * n_old[:, :1] + jnp.sum(kw, axis=-1, keepdims=True)
        n_scr[chain][...] = jnp.broadcast_to(n_new, (M_DK, LANE))
        m_scr[chain][...] = jnp.broadcast_to(m_new, (8, LANE))

    def run_chunk(q_ref, kt_ref, v_ref, g_ref, r0, rev, h_ref):
        tabs = gate_tables(g_ref[r0:r0 + M_CHUNK, :], rev)
        for hd in range(M_HEADS):
            chain = (M_HEADS if rev else 0) + hd
            q = q_ref[r0:r0 + M_CHUNK, hd * M_DK:(hd + 1) * M_DK]
            kt = kt_ref[hd * M_DK:(hd + 1) * M_DK, r0:r0 + M_CHUNK]
            v = v_ref[r0:r0 + M_CHUNK, hd * M_DV:(hd + 1) * M_DV]
            if h_ref is None:
                h_out = None
            else:
                def h_out(hv, hd=hd):
                    h_ref[:, hd * M_DV:(hd + 1) * M_DV] = hv
            chain_step(chain, q, kt, v, tabs, rev, h_out)

    @pl.when(j == 0)
    def _():
        for ref in state:
            ref[...] = jnp.zeros_like(ref)
        for cc in range(CTX_CHUNKS):
            run_chunk(qc_ref, kc_ref, vc_ref, gc_ref, cc * M_CHUNK, False, None)
            run_chunk(qc_ref, kc_ref, vc_ref, gc_ref, (CTX_CHUNKS - 1 - cc) * M_CHUNK, True, None)

    run_chunk(qf_ref, kf_ref, vf_ref, gf_ref, 0, False, hf_ref)
    run_chunk(qb_ref, kb_ref, vb_ref, gb_ref, 0, True, hb_ref)


def _mlstm(mq, mkt, mv, gates):
    ctx_blk0 = T_LAT // CTX_LEN
    fwd = lambda b, j: b * N_CHUNKS + j
    bwd = lambda b, j: b * N_CHUNKS + N_CHUNKS - 1 - j
    ctx = lambda b, j: ctx_blk0 + b

    def rows(width, tile, idx):
        return pl.BlockSpec((tile, width), lambda b, j: (idx(b, j), 0))

    def cols(tile, idx):
        return pl.BlockSpec((MQK_W, tile), lambda b, j: (0, idx(b, j)))

    h_shape = jax.ShapeDtypeStruct((T_LAT, M_WIDTH), F32)
    return pl.pallas_call(
        _mlstm_kernel,
        grid=(BATCH, N_CHUNKS),
        in_specs=[
            rows(MQK_W, M_CHUNK, fwd), rows(MQK_W, M_CHUNK, bwd), rows(MQK_W, CTX_LEN, ctx),
            cols(M_CHUNK, fwd), cols(M_CHUNK, bwd), cols(CTX_LEN, ctx),
            rows(M_WIDTH, M_CHUNK, fwd), rows(M_WIDTH, M_CHUNK, bwd), rows(M_WIDTH, CTX_LEN, ctx),
            rows(LANE, M_CHUNK, fwd), rows(LANE, M_CHUNK, bwd), rows(LANE, CTX_LEN, ctx),
        ],
        out_specs=[rows(M_WIDTH, M_CHUNK, fwd), rows(M_WIDTH, M_CHUNK, bwd)],
        out_shape=[h_shape, h_shape],
        scratch_shapes=([pltpu.VMEM((M_DK, M_DV), F32)] * N_CHAINS
                        + [pltpu.VMEM((M_DK, LANE), F32)] * N_CHAINS
                        + [pltpu.VMEM((8, LANE), F32)] * N_CHAINS),
        compiler_params=_params(2),
        name="mlstm_scan",
    )(mq, mq, mq, mkt, mkt, mkt, mv, mv, mv, gates, gates, gates)


TM4 = 256
TILES_PER_SEQ4 = SEQ // TM4
TILES_PER_STEP4 = 2
NEG_BIG = -1e30


def _outproj_kernel(a_ref, hf_ref, hb_ref, mo_ref, x_ref, w_ref, gm_ref, gt1_ref, sh2_ref,
                    sc2_ref, g2_ref, wrh_ref, wrl_ref, br_ref,
                    x1_ref, h2_ref, ti_ref, tr_ref, tw_ref, tc_ref):
    for hv in range(TILES_PER_STEP4):
        _outproj_tile(hv, a_ref, hf_ref, hb_ref, mo_ref, x_ref, w_ref, gm_ref, gt1_ref, sh2_ref,
                      sc2_ref, g2_ref, wrh_ref, wrl_ref, br_ref,
                      x1_ref, h2_ref, ti_ref, tr_ref, tw_ref, tc_ref)


def _outproj_tile(hv, a_ref, hf_ref, hb_ref, mo_ref, x_ref, w_ref, gm_ref, gt1_ref, sh2_ref,
                  sc2_ref, g2_ref, wrh_ref, wrl_ref, br_ref,
                  x1_ref, h2_ref, ti_ref, tr_ref, tw_ref, tc_ref):
    i = pl.program_id(0)
    row = (i * TILES_PER_STEP4 + hv) // TILES_PER_SEQ4
    rs = slice(hv * TM4, (hv + 1) * TM4)
    acc = jnp.dot(a_ref[rs, :], w_ref[0:ATT_WIDTH, :], preferred_element_type=F32)
    for hd in range(M_HEADS):
        sl = slice(hd * M_DV, (hd + 1) * M_DV)
        hn = _rms(hf_ref[rs, sl] + hb_ref[rs, sl]) * gm_ref[:, sl]
        mo = mo_ref[rs, sl]
        ym = hn / (1.0 + jnp.exp(-mo))
        acc += jnp.dot(ym.astype(BF16), w_ref[ATT_WIDTH + hd * M_DV:ATT_WIDTH + (hd + 1) * M_DV, :],
                       preferred_element_type=F32)
    x1 = x_ref[rs, :] + gt1_ref[pl.ds(row, 1), :] * acc
    x1_ref[rs, :] = x1
    h2 = (_rms(x1) * g2_ref[...]) * (1.0 + sc2_ref[pl.ds(row, 1), :]) + sh2_ref[pl.ds(row, 1), :]
    h2_ref[rs, :] = h2
    h_hi = h2.astype(BF16)
    h_lo = (h2 - h_hi.astype(F32)).astype(BF16)
    logits = (jnp.dot(h_hi, wrh_ref[...], preferred_element_type=F32)
              + jnp.dot(h_lo, wrh_ref[...], preferred_element_type=F32)
              + jnp.dot(h_hi, wrl_ref[...], preferred_element_type=F32)) + br_ref[...]
    lane = lax.broadcasted_iota(jnp.int32, (TM4, LANE), 1)
    vals, idxs = [], []
    for _ in range(TOP_K):
        mx = jnp.max(logits, axis=-1, keepdims=True)
        ik = jnp.min(jnp.where(logits == mx, lane, LANE), axis=-1, keepdims=True)
        vals.append(mx)
        idxs.append(ik)
        logits = jnp.where(lane == ik, -jnp.inf, logits)
    es = [jnp.exp(vk - vals[0]) for vk in vals]
    tot = es[0] + es[1] + es[2] + es[3]
    chosen = jnp.zeros((TM4, LANE), F32)
    for kk in range(TOP_K):
        chosen = jnp.where(lane == idxs[kk], 1.0, chosen)
    ri = lax.broadcasted_iota(jnp.int32, (TM4, TM4), 0)
    ci = lax.broadcasted_iota(jnp.int32, (TM4, TM4), 1)
    before = jnp.where(ci < ri, 1.0, 0.0).astype(BF16)
    earlier = jnp.dot(before, chosen.astype(BF16), preferred_element_type=F32)
    ti = jnp.zeros((TM4, LANE), jnp.int32)
    tr = jnp.zeros((TM4, LANE), jnp.int32)
    tw = jnp.zeros((TM4, LANE), F32)
    for kk in range(TOP_K):
        rk = jnp.sum(jnp.where(lane == idxs[kk], earlier, 0.0), axis=-1, keepdims=True)
        ti = jnp.where(lane == kk, idxs[kk], ti)
        tr = jnp.where(lane == kk, rk.astype(jnp.int32), tr)
        tw = jnp.where(lane == kk, es[kk] / tot, tw)
    ti_ref[rs, :] = ti
    tr_ref[rs, :] = tr
    tw_ref[rs, :] = tw
    tc_ref[hv] = jnp.sum(chosen, axis=0, keepdims=True).astype(jnp.int32)


def _out_projection(a_lat, hf, hb, mo, x2, w_out_b, g_mlstm, gt1, sh2, sc2, g2, wr_hi, wr_lo, br_p):
    full = lambda shape: pl.BlockSpec(shape, lambda i: (0, 0))
    rows = TILES_PER_STEP4 * TM4
    row_blk = lambda w: pl.BlockSpec((rows, w), lambda i: (i, 0))
    return pl.pallas_call(
        _outproj_kernel,
        grid=(T_LAT // rows,),
        in_specs=[
            row_blk(ATT_WIDTH), row_blk(M_WIDTH), row_blk(M_WIDTH), row_blk(M_WIDTH),
            row_blk(D_MODEL), full((ATT_WIDTH + M_WIDTH, D_MODEL)), full((1, M_WIDTH)),
            full((MOD_ROWS, D_MODEL)), full((MOD_ROWS, D_MODEL)), full((MOD_ROWS, D_MODEL)),
            full((1, D_MODEL)), full((D_MODEL, LANE)), full((D_MODEL, LANE)), full((1, LANE)),
        ],
        out_specs=[row_blk(D_MODEL), row_blk(D_MODEL), row_blk(LANE), row_blk(LANE), row_blk(LANE),
                   pl.BlockSpec((TILES_PER_STEP4, 1, LANE), lambda i: (i, 0, 0))],
        out_shape=[
            jax.ShapeDtypeStruct((T_LAT, D_MODEL), F32),
            jax.ShapeDtypeStruct((T_LAT, D_MODEL), F32),
            jax.ShapeDtypeStruct((T_LAT, LANE), jnp.int32),
            jax.ShapeDtypeStruct((T_LAT, LANE), jnp.int32),
            jax.ShapeDtypeStruct((T_LAT, LANE), F32),
            jax.ShapeDtypeStruct((T_LAT // TM4, 1, LANE), jnp.int32),
        ],
        compiler_params=_params(1),
        name="out_proj_router",
    )(a_lat, hf, hb, mo, x2, w_out_b, g_mlstm, gt1, sh2, sc2, g2, wr_hi, wr_lo, br_p)


MB = 128
N_ASSIGN = T_LAT * TOP_K
N_BLOCKS = N_ASSIGN // MB + N_EXPERTS
P_ROWS = N_BLOCKS * MB
N_PADS = P_ROWS - N_ASSIGN
SB = 10
W_MAX = N_EXPERTS + N_BLOCKS // SB
XROWS = SB * MB


def _routing_tables(top_idx, tile_rank, tile_counts):
    n_tiles = tile_counts.shape[0]
    counts = jnp.sum(tile_counts, axis=0)
    tile_base = jnp.cumsum(tile_counts, axis=0) - tile_counts
    nblk = (counts + MB - 1) // MB
    blk_start = jnp.cumsum(nblk) - nblk
    n_valid_blk = jnp.sum(nblk)
    offs = blk_start[None, :] * MB + tile_base
    e4 = top_idx.reshape(n_tiles, -1, TOP_K, 1)
    hit = e4 == jnp.arange(N_EXPERTS, dtype=jnp.int32)
    dest = jnp.sum(jnp.where(hit, offs[:, None, None, :], 0), axis=-1)
    dest = (dest.reshape(T_LAT, TOP_K) + tile_rank).reshape(N_ASSIGN).astype(jnp.int32)
    nwork = (nblk + SB - 1) // SB
    wend = jnp.cumsum(nwork)
    wstart = wend - nwork
    n_items = wend[-1]
    wid = jnp.arange(W_MAX, dtype=jnp.int32)
    valid = wid < n_items
    we = jnp.minimum(jnp.sum((wend[None, :] <= wid[:, None]).astype(jnp.int32), axis=1),
                     N_EXPERTS - 1)
    local = wid - wstart[we]
    work_blk = jnp.where(valid, blk_start[we] + local * SB, 0)
    work_n = jnp.where(valid, jnp.clip(nblk[we] - local * SB, 0, SB), 0)
    last_e = we[jnp.maximum(n_items - 1, 0)]
    work_e = jnp.where(valid, we, last_e)
    i32 = lambda a: a.astype(jnp.int32)
    return (dest, i32(counts), i32(blk_start), i32(nblk), i32(n_valid_blk)[None],
            i32(work_e), i32(work_blk), i32(work_n))


TMD = 256
ROW_GROUP = 8


def _dispatch_kernel(dest_ref, cnt_ref, bs_ref, nb_ref, nvb_ref, h2_ref, x_hbm, zrow, sem, psem):
    i = pl.program_id(0)

    @pl.when(i == 0)
    def _():
        zrow[...] = jnp.zeros_like(zrow)

        def zero_row(r, carry):
            pltpu.make_async_copy(zrow.at[pl.ds(0, 1)], x_hbm.at[pl.ds(r, 1)], psem).start()
            return carry

        def expert(e, carry):
            first = bs_ref[e] * MB
            lax.fori_loop(first + cnt_ref[e], first + nb_ref[e] * MB, zero_row, 0)
            return carry
        lax.fori_loop(0, N_EXPERTS, expert, 0)
        lax.fori_loop(nvb_ref[0] * MB, P_ROWS, zero_row, 0)

    base = i * (TMD * TOP_K)

    def issue(tg, carry):
        t0 = pl.multiple_of(tg * ROW_GROUP, ROW_GROUP)
        for j in range(ROW_GROUP):
            for kk in range(TOP_K):
                d = dest_ref[base + (t0 + j) * TOP_K + kk]
                pltpu.make_async_copy(h2_ref.at[pl.ds(t0 + j, 1)], x_hbm.at[pl.ds(d, 1)],
                                      sem).start(priority=kk % 2)
        return carry
    lax.fori_loop(0, TMD // ROW_GROUP, issue, 0)
    for kk in range(TOP_K):
        pltpu.make_async_copy(h2_ref, x_hbm.at[pl.ds(0, TMD)], sem).wait()

    @pl.when(i == 0)
    def _():
        for _ in range(N_PADS // TMD):
            pltpu.make_async_copy(h2_ref, x_hbm.at[pl.ds(0, TMD)], psem).wait()


def _dispatch(h2, dest, counts, blk_start, nblk, n_valid_blk):
    grid_spec = pltpu.PrefetchScalarGridSpec(
        num_scalar_prefetch=5,
        grid=(T_LAT // TMD,),
        in_specs=[pl.BlockSpec((TMD, D_MODEL), lambda i, *_: (i, 0))],
        out_specs=pl.BlockSpec(memory_space=pl.ANY),
        scratch_shapes=[
            pltpu.VMEM((8, D_MODEL), F32),
            pltpu.SemaphoreType.DMA(()),
            pltpu.SemaphoreType.DMA(()),
        ],
    )
    return pl.pallas_call(
        _dispatch_kernel,
        grid_spec=grid_spec,
        out_shape=jax.ShapeDtypeStruct((P_ROWS, D_MODEL), F32),
        compiler_params=_params(1),
        name="moe_dispatch",
    )(dest, counts, blk_start, nblk, n_valid_blk, h2)


TF = 512
NF = D_FF // TF
MC = 2 * MB
TF2 = TF // 2
W_PRIORITY = 1


def _moe_kernel(we_ref, wb_ref, wn_ref, nvb_ref,
                x_hbm, w1_hbm, w2_hbm, w1g_ref, w2a_ref, b1g_ref, b1u_ref, b2_ref,
                y_hbm, xbuf, xb16, acc, w1u_buf, w2b_buf, xsem, ysem, wsem):
    w = pl.program_id(0)
    f = pl.program_id(1)
    nsub = wn_ref[w]
    r0 = wb_ref[w] * MB

    step = w * NF + f
    wslot = step % 2

    def w_copies(item, ff, slot_):
        e = we_ref[item]
        up = pltpu.make_async_copy(w1_hbm.at[e, :, pl.ds(D_FF + ff * TF, TF)],
                                   w1u_buf.at[slot_], wsem.at[slot_])
        dn = pltpu.make_async_copy(w2_hbm.at[e, pl.ds(ff * TF + TF2, TF2), :],
                                   w2b_buf.at[slot_], wsem.at[slot_])
        return up, dn

    def start_w(item, ff, slot_):
        for cp in w_copies(item, ff, slot_):
            cp.start(priority=W_PRIORITY)

    @pl.when(jnp.logical_and(step == 0, nsub > 0))
    def _():
        start_w(0, 0, 0)

    last_f = f == NF - 1
    nxt_item = jnp.where(last_f, w + 1, w)
    nxt_f = jnp.where(last_f, 0, f + 1)

    @pl.when(jnp.logical_and(nxt_item < W_MAX, wn_ref[jnp.minimum(nxt_item, W_MAX - 1)] > 0))
    def _():
        start_w(nxt_item, nxt_f, 1 - wslot)

    @pl.when(nsub > 0)
    def _():
        for cp in w_copies(w, f, wslot):
            cp.wait()

    def x_copy(item, s):
        src = x_hbm.at[pl.ds(wb_ref[item] * MB + s * MB, MB)]
        return pltpu.make_async_copy(src, xbuf.at[pl.ds(s * MB, MB)], xsem)

    def start_x(item):
        def body(s, carry):
            x_copy(item, s).start()
            return carry
        lax.fori_loop(0, wn_ref[item], body, 0)

    @pl.when(jnp.logical_and(w == 0, f == 0))
    def _():
        start_x(0)

    @pl.when(f == 0)
    def _():
        def wait(s, carry):
            x_copy(w, s).wait()
            return carry
        lax.fori_loop(0, nsub, wait, 0)

        def body(s, carry):
            rs = pl.ds(pl.multiple_of(s * MB, MB), MB)
            xb16[rs, :] = xbuf[rs, :].astype(BF16)
            acc[rs, :] = jnp.broadcast_to(b2_ref[...], (MB, D_MODEL))
            return carry
        lax.fori_loop(0, nsub, body, 0)

    @pl.when(jnp.logical_and(f == 1, w + 1 < W_MAX))
    def _():
        start_x(w + 1)

    def y_copy(s):
        rs = pl.ds(pl.multiple_of(s * MB, MB), MB)
        return pltpu.make_async_copy(acc.at[rs], y_hbm.at[pl.ds(r0 + s * MB, MB)], ysem)

    def chunk(row, m):
        rs = pl.ds(row, m)
        xb = xb16[rs, :]
        gate = jnp.dot(xb, w1g_ref[...], preferred_element_type=F32) + b1g_ref[...]
        up = jnp.dot(xb, w1u_buf[wslot], preferred_element_type=F32) + b1u_ref[...]
        gate = jnp.minimum(gate, SWIGLU_LIMIT)
        up = jnp.clip(up, -SWIGLU_LIMIT, SWIGLU_LIMIT)
        glu = gate / (1.0 + jnp.exp(-SWIGLU_ALPHA * gate))
        act = (up + 1.0) * glu
        acc[rs, :] += (jnp.dot(act[:, :TF2], w2a_ref[...], preferred_element_type=F32)
                       + jnp.dot(act[:, TF2:], w2b_buf[wslot], preferred_element_type=F32))

    def write_back(first_blk, n_blk):
        @pl.when(last_f)
        def _():
            for s in range(n_blk):
                y_copy(first_blk + s).start()

    nquad = nsub // 4
    rem = nsub - nquad * 4

    def quad(c, carry):
        row = pl.multiple_of(c * (2 * MC), 2 * MC)
        chunk(row, MC)
        chunk(row + MC, MC)
        write_back(c * 4, 4)
        return carry
    lax.fori_loop(0, nquad, quad, 0)

    @pl.when(rem >= 2)
    def _():
        chunk(pl.multiple_of(nquad * (2 * MC), MC), MC)
        write_back(nquad * 4, 2)

    @pl.when(rem % 2 == 1)
    def _():
        chunk(pl.multiple_of((nsub - 1) * MB, MB), MB)
        write_back(nsub - 1, 1)

    @pl.when(last_f)
    def _():
        def wait(s, carry):
            y_copy(s).wait()
            return carry
        lax.fori_loop(0, nsub, wait, 0)

    @pl.when(jnp.logical_and(w == W_MAX - 1, f == NF - 1))
    def _():
        acc[0:MB, :] = jnp.zeros((MB, D_MODEL), F32)

        def tail_copy(bk):
            return pltpu.make_async_copy(acc.at[0:MB], y_hbm.at[pl.ds(bk * MB, MB)], ysem)

        def start(bk, carry):
            tail_copy(bk).start()
            return carry
        lax.fori_loop(nvb_ref[0], N_BLOCKS, start, 0)

        def wait(bk, carry):
            tail_copy(bk).wait()
            return carry
        lax.fori_loop(nvb_ref[0], N_BLOCKS, wait, 0)


def _moe(x_sorted, w1, b1, w2, b2, work_e, work_blk, work_n, n_valid_blk):
    b1r = b1.reshape(N_EXPERTS, 1, 2 * D_FF)
    b2r = b2.reshape(N_EXPERTS, 1, D_MODEL)
    grid_spec = pltpu.PrefetchScalarGridSpec(
        num_scalar_prefetch=4,
        grid=(W_MAX, NF),
        in_specs=[
            pl.BlockSpec(memory_space=pl.ANY),
            pl.BlockSpec(memory_space=pl.ANY),
            pl.BlockSpec(memory_space=pl.ANY),
            pl.BlockSpec((None, D_MODEL, TF), lambda w, f, we, *_: (we[w], 0, f)),
            pl.BlockSpec((None, TF2, D_MODEL), lambda w, f, we, *_: (we[w], 2 * f, 0)),
            pl.BlockSpec((None, 1, TF), lambda w, f, we, *_: (we[w], 0, f)),
            pl.BlockSpec((None, 1, TF), lambda w, f, we, *_: (we[w], 0, NF + f)),
            pl.BlockSpec((None, 1, D_MODEL), lambda w, f, we, *_: (we[w], 0, 0)),
        ],
        out_specs=pl.BlockSpec(memory_space=pl.ANY),
        scratch_shapes=[
            pltpu.VMEM((XROWS, D_MODEL), F32),
            pltpu.VMEM((XROWS, D_MODEL), BF16),
            pltpu.VMEM((XROWS, D_MODEL), F32),
            pltpu.VMEM((2, D_MODEL, TF), F32),
            pltpu.VMEM((2, TF2, D_MODEL), F32),
            pltpu.SemaphoreType.DMA(()),
            pltpu.SemaphoreType.DMA(()),
            pltpu.SemaphoreType.DMA((2,)),
        ],
    )
    return pl.pallas_call(
        _moe_kernel,
        grid_spec=grid_spec,
        out_shape=jax.ShapeDtypeStruct((P_ROWS, D_MODEL), F32),
        compiler_params=_params(2),
        name="moe_experts",
    )(work_e, work_blk, work_n, n_valid_blk, x_sorted, w1, w2, w1, w2, b1r, b1r, b2r)


TM7 = 256
N_TILES7 = T_LAT // TM7
TILES_PER_SEQ7 = SEQ // TM7


def _final_kernel(pos_ref, x1_ref, tw_ref, gt2_ref, gf_ref, y_hbm, o_ref, ybuf, sem):
    i = pl.program_id(0)
    slot = i % 2

    def start_gather(tile, slot_):
        base = tile * (TM7 * TOP_K)

        def body(tg, carry):
            t0 = pl.multiple_of(tg * ROW_GROUP, ROW_GROUP)
            for j in range(ROW_GROUP):
                for kk in range(TOP_K):
                    p = pos_ref[base + (t0 + j) * TOP_K + kk]
                    pltpu.make_async_copy(y_hbm.at[pl.ds(p, 1)],
                                          ybuf.at[slot_, kk, pl.ds(t0 + j, 1)],
                                          sem.at[slot_]).start(priority=kk % 2)
            return carry
        lax.fori_loop(0, TM7 // ROW_GROUP, body, 0)

    @pl.when(i == 0)
    def _():
        start_gather(0, 0)

    @pl.when(i + 1 < N_TILES7)
    def _():
        start_gather(i + 1, 1 - slot)

    for kk in range(TOP_K):
        pltpu.make_async_copy(y_hbm.at[pl.ds(0, TM7)], ybuf.at[slot, kk], sem.at[slot]).wait()

    row = i // TILES_PER_SEQ7
    tw = tw_ref[...]
    mix = jnp.zeros((TM7, D_MODEL), F32)
    for kk in range(TOP_K):
        mix += ybuf[slot, kk] * tw[:, kk:kk + 1]
    xo = x1_ref[...] + gt2_ref[pl.ds(row, 1), :] * mix
    o_ref[...] = _rms(xo) * gf_ref[...]


def _final(dest, x1, top_w, gt2, g_final, y_sorted):
    grid_spec = pltpu.PrefetchScalarGridSpec(
        num_scalar_prefetch=1,
        grid=(N_TILES7,),
        in_specs=[
            pl.BlockSpec((TM7, D_MODEL), lambda i, *_: (i, 0)),
            pl.BlockSpec((TM7, LANE), lambda i, *_: (i, 0)),
            pl.BlockSpec((MOD_ROWS, D_MODEL), lambda i, *_: (0, 0)),
            pl.BlockSpec((1, D_MODEL), lambda i, *_: (0, 0)),
            pl.BlockSpec(memory_space=pl.ANY),
        ],
        out_specs=pl.BlockSpec((TM7, D_MODEL), lambda i, *_: (i, 0)),
        scratch_shapes=[
            pltpu.VMEM((2, TOP_K, TM7, D_MODEL), F32),
            pltpu.SemaphoreType.DMA((2,)),
        ],
    )
    return pl.pallas_call(
        _final_kernel,
        grid_spec=grid_spec,
        out_shape=jax.ShapeDtypeStruct((T_LAT, D_MODEL), F32),
        compiler_params=_params(1),
        name="combine_final_norm",
    )(dest, x1, top_w, gt2, g_final, y_sorted)


def _rope_tables():
    rows = SEQ // GRID_W
    row = jnp.repeat(jnp.arange(rows, dtype=F32), GRID_W)
    col = jnp.tile(jnp.arange(GRID_W, dtype=F32), rows)
    inv = ROPE_THETA ** (-jnp.arange(0, AXIS_DIM, 2, dtype=F32) / AXIS_DIM)
    ang = jnp.concatenate([row[:, None] * inv, col[:, None] * inv], axis=-1)
    cos = jnp.repeat(jnp.cos(ang), 2, axis=-1)
    sin = jnp.repeat(jnp.sin(ang), 2, axis=-1)
    sign = jnp.tile(jnp.array([-1.0, 1.0], F32), HEAD_DIM // 2)
    cos_t = jnp.concatenate([cos, jnp.ones((TM1, HEAD_DIM), F32)], axis=0)
    sin_t = jnp.concatenate([sin * sign, jnp.zeros((TM1, HEAD_DIM), F32)], axis=0)
    return cos_t, sin_t


def kernel(x, c, ctx, c_ctx, w_mod, b_mod, g_norm1, w_in, b_in, g_q, g_k, g_mlstm, w_out,
           g_norm2, w_router, b_router, w1, b1, w2, b2, g_final):
    x2 = x.reshape(T_LAT, D_MODEL)
    ctx2 = ctx.reshape(T_CTX, D_MODEL)
    cos_t, sin_t = _rope_tables()
    assert w_mod.shape[0] == 1
    l = 0
    c_rows = jnp.concatenate(
        [c, c_ctx[None, :], jnp.zeros((MOD_ROWS - BATCH - 1, D_MODEL), F32)], axis=0)
    mod = _modulation(c_rows, w_mod[l], b_mod[l][None, :])
    sh1, sc1, gt1, sh2, sc2, gt2 = [mod[:, k * D_MODEL:(k + 1) * D_MODEL]
                                    for k in range(N_ADALN)]
    pad = IN_COLS_PAD - IN_COLS
    w_main = w_in[l][:, :C_G].astype(BF16)
    w_gate = jnp.pad(w_in[l][:, C_G:], ((0, 0), (0, pad))).astype(BF16)
    b_main = b_in[l][None, :C_G]
    b_gate = jnp.pad(b_in[l][C_G:], (0, pad))[None, :]
    q, k, v, mq, mkt, mv, mo, gates = _in_projection(
        x2, ctx2, sh1, sc1, g_norm1[l][None, :], w_main, b_main, w_gate, b_gate,
        g_q[l][None, :], g_k[l][None, :], cos_t, sin_t)
    a_lat = _attention(q, k, v)
    hf, hb = _mlstm(mq, mkt, mv, gates)
    wr_p = jnp.pad(w_router[l], ((0, 0), (0, LANE - N_EXPERTS)))
    wr_hi = wr_p.astype(BF16)
    wr_lo = (wr_p - wr_hi.astype(F32)).astype(BF16)
    br_p = jnp.pad(b_router[l], (0, LANE - N_EXPERTS), constant_values=NEG_BIG)[None, :]
    x1, h2, top_i, top_r, top_w, tile_cnt = _out_projection(
        a_lat, hf, hb, mo, x2, w_out[l].astype(BF16), g_mlstm[l][None, :],
        gt1, sh2, sc2, g_norm2[l][None, :], wr_hi, wr_lo, br_p)
    dest, counts, blk_start, nblk, n_valid_blk, work_e, work_blk, work_n = _routing_tables(
        top_i[:, :TOP_K], top_r[:, :TOP_K], tile_cnt[:, 0, :N_EXPERTS])
    x_sorted = _dispatch(h2, dest, counts, blk_start, nblk, n_valid_blk)
    y_sorted = _moe(x_sorted, w1[l], b1[l], w2[l], b2[l], work_e, work_blk, work_n, n_valid_blk)
    out = _final(dest, x1, top_w, gt2, g_final[None, :], y_sorted)
    return out.reshape(BATCH, SEQ, D_MODEL)
```

```python
import jax
import jax.numpy as jnp
from jax import lax
from jax.experimental import pallas as pl
from jax.experimental.pallas import tpu as pltpu

F32 = jnp.float32
BF16 = jnp.bfloat16
HIGHEST = lax.Precision.HIGHEST

D_MODEL = 2048
BATCH = 2
SEQ = 4096
CTX_LEN = 256
GRID_W = 64
N_ADALN = 6
EPS = 1e-6
N_HEADS = 8
KV_HEADS = 2
GROUP = N_HEADS // KV_HEADS
HEAD_DIM = 128
AXIS_DIM = HEAD_DIM // 2
ROPE_THETA = 10000.0
ATT_WIDTH = N_HEADS * HEAD_DIM
ATT_SCALE = HEAD_DIM ** -0.5
M_HEADS = 4
M_DK = 128
M_DV = 256
M_CHUNK = 128
M_WIDTH = M_HEADS * M_DV
N_EXPERTS = 32
TOP_K = 4
D_FF = D_MODEL
SWIGLU_LIMIT = 7.0
SWIGLU_ALPHA = 1.702

T_LAT = BATCH * SEQ
T_CTX = BATCH * CTX_LEN
T_ALL = T_LAT + T_CTX
KV_W = KV_HEADS * HEAD_DIM
MQK_W = M_HEADS * M_DK
LANE = 128

C_Q = 0
C_K = C_Q + ATT_WIDTH
C_V = C_K + KV_W
C_MQ = C_V + KV_W
C_MK = C_MQ + MQK_W
C_MV = C_MK + MQK_W
C_MO = C_MV + M_WIDTH
C_G = C_MO + M_WIDTH
IN_COLS = C_G + 4 * M_HEADS
IN_COLS_PAD = C_G + LANE

VMEM_LIMIT = 56 * 1024 * 1024


def _params(n_axes):
    return pltpu.CompilerParams(
        dimension_semantics=("arbitrary",) * n_axes, vmem_limit_bytes=VMEM_LIMIT)


def _rms(x):
    return x * lax.rsqrt(jnp.mean(x * x, axis=-1, keepdims=True) + EPS)


MOD_ROWS = 8
MOD_TN = 1024


def _mod_kernel(c_ref, w_ref, b_ref, o_ref):
    c = c_ref[...]
    s = c / (1.0 + jnp.exp(-c))
    o_ref[...] = jnp.dot(s, w_ref[...], preferred_element_type=F32) + b_ref[...]


def _modulation(c_rows, w_mod, b_mod):
    n = w_mod.shape[1]
    return pl.pallas_call(
        _mod_kernel,
        grid=(n // MOD_TN,),
        in_specs=[
            pl.BlockSpec((MOD_ROWS, D_MODEL), lambda j: (0, 0)),
            pl.BlockSpec((D_MODEL, MOD_TN), lambda j: (0, j)),
            pl.BlockSpec((1, MOD_TN), lambda j: (0, j)),
        ],
        out_specs=pl.BlockSpec((MOD_ROWS, MOD_TN), lambda j: (0, j)),
        out_shape=jax.ShapeDtypeStruct((MOD_ROWS, n), F32),
        compiler_params=_params(1),
        name="adaln_mod",
    )(c_rows, w_mod, b_mod)


TM1 = 256
N_LAT_TILES1 = T_LAT // TM1
N_TILES1 = T_ALL // TM1
TILES_PER_SEQ1 = SEQ // TM1
PCH = 512
LOG2E = 1.4426950408889634
Q_SCALE = ATT_SCALE * LOG2E
V_AUG = 2 * HEAD_DIM


def _inproj_kernel(x_ref, ctx_ref, sh_ref, sc_ref, g1_ref, w_ref, b_ref, wg_ref, bg_ref,
                   gq_ref, gk_ref, cos_ref, sin_ref,
                   q_ref, k_ref, v_ref, mq_ref, mkt_ref, mv_ref, mo_ref, g_ref, h_scr):
    i = pl.program_id(0)
    row = jnp.where(i < N_LAT_TILES1, i // TILES_PER_SEQ1, BATCH)
    sc = sc_ref[pl.ds(row, 1), :]
    sh = sh_ref[pl.ds(row, 1), :]

    def norm_mod(xv):
        return (_rms(xv) * g1_ref[...]) * (1.0 + sc) + sh

    @pl.when(i < N_LAT_TILES1)
    def _():
        h_scr[...] = norm_mod(x_ref[...]).astype(BF16)

    @pl.when(i >= N_LAT_TILES1)
    def _():
        h_scr[...] = norm_mod(ctx_ref[...]).astype(BF16)

    h = h_scr[...]

    def proj(c0, c1):
        return jnp.dot(h, w_ref[:, c0:c1], preferred_element_type=F32) + b_ref[:, c0:c1]

    cos_f = cos_ref[...]
    sin_s = sin_ref[...]
    lane = lax.broadcasted_iota(jnp.int32, (TM1, HEAD_DIM), 1)
    even = (lane % 2) == 0

    def head_norm_rope(a, g):
        y = _rms(a) * g
        nxt = pltpu.roll(y, HEAD_DIM - 1, 1)
        prv = pltpu.roll(y, 1, 1)
        return y * cos_f + jnp.where(even, nxt, prv) * sin_s

    for c0 in range(C_Q, C_K, PCH):
        acc = proj(c0, c0 + PCH)
        for hh in range(PCH // HEAD_DIM):
            a = acc[:, hh * HEAD_DIM:(hh + 1) * HEAD_DIM]
            r = head_norm_rope(a, gq_ref[...]) * Q_SCALE
            q_ref[:, c0 + hh * HEAD_DIM:c0 + (hh + 1) * HEAD_DIM] = r.astype(BF16)
    acc = proj(C_K, C_MQ)
    for hh in range(KV_HEADS):
        a = acc[:, hh * HEAD_DIM:(hh + 1) * HEAD_DIM]
        k_ref[:, hh * HEAD_DIM:(hh + 1) * HEAD_DIM] = head_norm_rope(a, gk_ref[...]).astype(BF16)
    for hh in range(KV_HEADS):
        v_ref[:, hh * V_AUG:hh * V_AUG + HEAD_DIM] = (
            acc[:, KV_W + hh * HEAD_DIM:KV_W + (hh + 1) * HEAD_DIM].astype(BF16))
        v_ref[:, hh * V_AUG + HEAD_DIM:(hh + 1) * V_AUG] = jnp.ones((TM1, HEAD_DIM), BF16)
    mq_ref[...] = proj(C_MQ, C_MK).astype(BF16)
    mk = proj(C_MK, C_MV) * (M_DK ** -0.5)
    mkt_ref[...] = mk.T.astype(BF16)
    for c0 in range(C_MV, C_MO, PCH):
        mv_ref[:, c0 - C_MV:c0 - C_MV + PCH] = proj(c0, c0 + PCH).astype(BF16)
    for c0 in range(C_MO, C_G, PCH):
        mo_ref[:, c0 - C_MO:c0 - C_MO + PCH] = proj(c0, c0 + PCH)
    g_ref[...] = jnp.dot(h, wg_ref[...], preferred_element_type=F32) + bg_ref[...]


def _in_projection(x2, ctx2, sh1, sc1, g1, w_main, b_main, w_gate, b_gate, g_q, g_k, cos_t, sin_t):
    lat_idx = lambda i: (jnp.minimum(i, N_LAT_TILES1 - 1), 0)
    ctx_idx = lambda i: (jnp.maximum(i - N_LAT_TILES1, 0), 0)
    rope_idx = lambda i: (jnp.where(i < N_LAT_TILES1, i % TILES_PER_SEQ1, TILES_PER_SEQ1), 0)
    full = lambda shape: pl.BlockSpec(shape, lambda i: (0, 0))
    row_blk = lambda w: pl.BlockSpec((TM1, w), lambda i: (i, 0))
    out_shapes = [
        jax.ShapeDtypeStruct((T_ALL, ATT_WIDTH), BF16),
        jax.ShapeDtypeStruct((T_ALL, KV_W), BF16),
        jax.ShapeDtypeStruct((T_ALL, KV_HEADS * V_AUG), BF16),
        jax.ShapeDtypeStruct((T_ALL, MQK_W), BF16),
        jax.ShapeDtypeStruct((MQK_W, T_ALL), BF16),
        jax.ShapeDtypeStruct((T_ALL, M_WIDTH), BF16),
        jax.ShapeDtypeStruct((T_ALL, M_WIDTH), F32),
        jax.ShapeDtypeStruct((T_ALL, LANE), F32),
    ]
    out_specs = [
        row_blk(ATT_WIDTH), row_blk(KV_W), row_blk(KV_HEADS * V_AUG), row_blk(MQK_W),
        pl.BlockSpec((MQK_W, TM1), lambda i: (0, i)),
        row_blk(M_WIDTH), row_blk(M_WIDTH), row_blk(LANE),
    ]
    return pl.pallas_call(
        _inproj_kernel,
        grid=(N_TILES1,),
        in_specs=[
            pl.BlockSpec((TM1, D_MODEL), lat_idx),
            pl.BlockSpec((TM1, D_MODEL), ctx_idx),
            full((MOD_ROWS, D_MODEL)), full((MOD_ROWS, D_MODEL)), full((1, D_MODEL)),
            pl.BlockSpec((D_MODEL, C_G), lambda i: (0, 0), pipeline_mode=pl.Buffered(1)),
            full((1, C_G)), full((D_MODEL, LANE)), full((1, LANE)),
            full((1, HEAD_DIM)), full((1, HEAD_DIM)),
            pl.BlockSpec((TM1, HEAD_DIM), rope_idx),
            pl.BlockSpec((TM1, HEAD_DIM), rope_idx),
        ],
        out_specs=out_specs,
        out_shape=out_shapes,
        scratch_shapes=[pltpu.VMEM((TM1, D_MODEL), BF16)],
        compiler_params=_params(1),
        name="in_proj",
    )(x2, ctx2, sh1, sc1, g1, w_main, b_main, w_gate, b_gate, g_q, g_k, cos_t, sin_t)


TQ = 256
CK = 512
GQ_W = GROUP * HEAD_DIM


def _attn_kernel(q_ref, kl_ref, vl_ref, kc_ref, vc_ref, o_ref):
    chunks = [(kl_ref, vl_ref, c * CK, CK) for c in range(SEQ // CK)]
    chunks.append((kc_ref, vc_ref, 0, CTX_LEN))
    for g in range(GROUP):
        qh = q_ref[:, g * HEAD_DIM:(g + 1) * HEAD_DIM]
        m = jnp.full((TQ, 1), -jnp.inf, F32)
        acc = jnp.zeros((TQ, V_AUG), F32)
        for kr, vr, st, sz in chunks:
            s = lax.dot_general(qh, kr[st:st + sz, :], (((1,), (1,)), ((), ())),
                                preferred_element_type=F32)
            m_new = jnp.maximum(m, jnp.max(s, axis=-1, keepdims=True))
            p = jnp.exp2(s - m_new)
            alpha = jnp.exp2(m - m_new)
            acc = alpha * acc + jnp.dot(p.astype(BF16), vr[st:st + sz, :],
                                        preferred_element_type=F32)
            m = m_new
        o_ref[:, g * HEAD_DIM:(g + 1) * HEAD_DIM] = (
            acc[:, :HEAD_DIM] / acc[:, HEAD_DIM:]).astype(BF16)


def _attention(q, k, v):
    nq = SEQ // TQ
    ctx_blk0 = T_LAT // CTX_LEN
    lat_k = pl.BlockSpec((SEQ, HEAD_DIM), lambda b, h, i: (b, h))
    lat_v = pl.BlockSpec((SEQ, V_AUG), lambda b, h, i: (b, h))
    ctx_k = pl.BlockSpec((CTX_LEN, HEAD_DIM), lambda b, h, i: (ctx_blk0 + b, h))
    ctx_v = pl.BlockSpec((CTX_LEN, V_AUG), lambda b, h, i: (ctx_blk0 + b, h))
    q_blk = pl.BlockSpec((TQ, GQ_W), lambda b, h, i: (b * nq + i, h))
    return pl.pallas_call(
        _attn_kernel,
        grid=(BATCH, KV_HEADS, nq),
        in_specs=[q_blk, lat_k, lat_v, ctx_k, ctx_v],
        out_specs=q_blk,
        out_shape=jax.ShapeDtypeStruct((T_LAT, ATT_WIDTH), BF16),
        compiler_params=_params(3),
        name="gqa_attention",
    )(q, k, v, k, v)


N_CHUNKS = SEQ // M_CHUNK
CTX_CHUNKS = CTX_LEN // M_CHUNK
N_CHAINS = 2 * M_HEADS


def _log_sigmoid(x):
    return jnp.minimum(x, 0.0) - jnp.log(1.0 + jnp.exp(-jnp.abs(x)))


def _mlstm_kernel(qf_ref, qb_ref, qc_ref, kf_ref, kb_ref, kc_ref, vf_ref, vb_ref, vc_ref,
                  gf_ref, gb_ref, gc_ref, hf_ref, hb_ref, *state):
    c_scr = state[0:N_CHAINS]
    n_scr = state[N_CHAINS:2 * N_CHAINS]
    m_scr = state[2 * N_CHAINS:3 * N_CHAINS]
    j = pl.program_id(1)
    ri = lax.broadcasted_iota(jnp.int32, (M_CHUNK, M_CHUNK), 0)
    ci = lax.broadcasted_iota(jnp.int32, (M_CHUNK, M_CHUNK), 1)
    lower = ci <= ri
    upper = ci >= ri
    tri_l = lower.astype(F32)
    tri_u = upper.astype(F32)

    def gate_tables(g, rev):
        gt = g.T
        mcol, mrow = (tri_u, tri_l) if rev else (tri_l, tri_u)
        bcol = jnp.dot(mcol, _log_sigmoid(g), precision=HIGHEST, preferred_element_type=F32)
        brow = jnp.dot(_log_sigmoid(gt), mrow, precision=HIGHEST, preferred_element_type=F32)
        return gt, bcol, brow

    def chain_step(chain, q, kt, v, tabs, rev, h_out):
        gt, bcol, brow = tabs
        icol = chain
        fcol = N_CHAINS + chain
        last = 0 if rev else M_CHUNK - 1
        b_col = bcol[:, fcol:fcol + 1]
        b_row = brow[fcol:fcol + 1, :]
        i_row = gt[icol:icol + 1, :]
        tot = brow[fcol:fcol + 1, last:last + 1]
        m_old = m_scr[chain][:1, :1]
        c_old = c_scr[chain][...]
        n_old = n_scr[chain][...]
        if h_out is not None:
            d = jnp.where(upper if rev else lower, b_col - b_row + i_row, -jnp.inf)
            inter = b_col + m_old
            m_t = jnp.maximum(inter, jnp.max(d, axis=-1, keepdims=True))
            w_intra = jnp.exp(d - m_t)
            w_inter = jnp.exp(inter - m_t)
            s = jnp.dot(q, kt, preferred_element_type=F32) * w_intra
            qc = jnp.dot(q, c_old.astype(BF16), preferred_element_type=F32)
            qn = jnp.dot(q, n_old.astype(BF16), preferred_element_type=F32)[:, :1]
            num = jnp.dot(s.astype(BF16), v, preferred_element_type=F32) + w_inter * qc
            den = jnp.sum(s, axis=-1, keepdims=True) + w_inter * qn
            h_out(num / jnp.maximum(jnp.abs(den), jnp.exp(-m_t)))
        g_row = tot - b_row + i_row
        m_new = jnp.maximum(tot + m_old, jnp.max(g_row, axis=-1, keepdims=True))
        w_state = jnp.exp(g_row - m_new)
        decay = jnp.exp(tot + m_old - m_new)
        kw = kt.astype(F32) * w_state
        c_scr[chain][...] = decay * c_old + jnp.dot(kw.astype(BF16), v, preferred_element_type=F32)
        n_new = decay * n_old[:, :1] + jnp.sum(kw, axis=-1, keepdims=True)
        n_scr[chain][...] = jnp.broadcast_to(n_new, (M_DK, LANE))
        m_scr[chain][...] = jnp.broadcast_to(m_new, (8, LANE))

    def run_chunk(q_ref, kt_ref, v_ref, g_ref, r0, rev, h_ref):
        tabs = gate_tables(g_ref[r0:r0 + M_CHUNK, :], rev)
        for hd in range(M_HEADS):
            chain = (M_HEADS if rev else 0) + hd
            q = q_ref[r0:r0 + M_CHUNK, hd * M_DK:(hd + 1) * M_DK]
            kt = kt_ref[hd * M_DK:(hd + 1) * M_DK, r0:r0 + M_CHUNK]
            v = v_ref[r0:r0 + M_CHUNK, hd * M_DV:(hd + 1) * M_DV]
            if h_ref is None:
                h_out = None
            else:
                def h_out(hv, hd=hd):
                    h_ref[:, hd * M_DV:(hd + 1) * M_DV] = hv
            chain_step(chain, q, kt, v, tabs, rev, h_out)

    @pl.when(j == 0)
    def _():
        for ref in state:
            ref[...] = jnp.zeros_like(ref)
        for cc in range(CTX_CHUNKS):
            run_chunk(qc_ref, kc_ref, vc_ref, gc_ref, cc * M_CHUNK, False, None)
            run_chunk(qc_ref, kc_ref, vc_ref, gc_ref, (CTX_CHUNKS - 1 - cc) * M_CHUNK, True, None)

    run_chunk(qf_ref, kf_ref, vf_ref, gf_ref, 0, False, hf_ref)
    run_chunk(qb_ref, kb_ref, vb_ref, gb_ref, 0, True, hb_ref)


def _mlstm(mq, mkt, mv, gates):
    ctx_blk0 = T_LAT // CTX_LEN
    fwd = lambda b, j: b * N_CHUNKS + j
    bwd = lambda b, j: b * N_CHUNKS + N_CHUNKS - 1 - j
    ctx = lambda b, j: ctx_blk0 + b

    def rows(width, tile, idx):
        return pl.BlockSpec((tile, width), lambda b, j: (idx(b, j), 0))

    def cols(tile, idx):
        return pl.BlockSpec((MQK_W, tile), lambda b, j: (0, idx(b, j)))

    h_shape = jax.ShapeDtypeStruct((T_LAT, M_WIDTH), F32)
    return pl.pallas_call(
        _mlstm_kernel,
        grid=(BATCH, N_CHUNKS),
        in_specs=[
            rows(MQK_W, M_CHUNK, fwd), rows(MQK_W, M_CHUNK, bwd), rows(MQK_W, CTX_LEN, ctx),
            cols(M_CHUNK, fwd), cols(M_CHUNK, bwd), cols(CTX_LEN, ctx),
            rows(M_WIDTH, M_CHUNK, fwd), rows(M_WIDTH, M_CHUNK, bwd), rows(M_WIDTH, CTX_LEN, ctx),
            rows(LANE, M_CHUNK, fwd), rows(LANE, M_CHUNK, bwd), rows(LANE, CTX_LEN, ctx),
        ],
        out_specs=[rows(M_WIDTH, M_CHUNK, fwd), rows(M_WIDTH, M_CHUNK, bwd)],
        out_shape=[h_shape, h_shape],
        scratch_shapes=([pltpu.VMEM((M_DK, M_DV), F32)] * N_CHAINS
                        + [pltpu.VMEM((M_DK, LANE), F32)] * N_CHAINS
                        + [pltpu.VMEM((8, LANE), F32)] * N_CHAINS),
        compiler_params=_params(2),
        name="mlstm_scan",
    )(mq, mq, mq, mkt, mkt, mkt, mv, mv, mv, gates, gates, gates)


TM4 = 256
TILES_PER_SEQ4 = SEQ // TM4
TILES_PER_STEP4 = 2
NEG_BIG = -1e30


def _outproj_kernel(a_ref, hf_ref, hb_ref, mo_ref, x_ref, w_ref, gm_ref, gt1_ref, sh2_ref,
                    sc2_ref, g2_ref, wrh_ref, wrl_ref, br_ref,
                    x1_ref, h2_ref, ti_ref, tr_ref, tw_ref, tc_ref):
    for hv in range(TILES_PER_STEP4):
        _outproj_tile(hv, a_ref, hf_ref, hb_ref, mo_ref, x_ref, w_ref, gm_ref, gt1_ref, sh2_ref,
                      sc2_ref, g2_ref, wrh_ref, wrl_ref, br_ref,
                      x1_ref, h2_ref, ti_ref, tr_ref, tw_ref, tc_ref)


def _outproj_tile(hv, a_ref, hf_ref, hb_ref, mo_ref, x_ref, w_ref, gm_ref, gt1_ref, sh2_ref,
                  sc2_ref, g2_ref, wrh_ref, wrl_ref, br_ref,
                  x1_ref, h2_ref, ti_ref, tr_ref, tw_ref, tc_ref):
    i = pl.program_id(0)
    row = (i * TILES_PER_STEP4 + hv) // TILES_PER_SEQ4
    rs = slice(hv * TM4, (hv + 1) * TM4)
    acc = jnp.dot(a_ref[rs, :], w_ref[0:ATT_WIDTH, :], preferred_element_type=F32)
    for hd in range(M_HEADS):
        sl = slice(hd * M_DV, (hd + 1) * M_DV)
        hn = _rms(hf_ref[rs, sl] + hb_ref[rs, sl]) * gm_ref[:, sl]
        mo = mo_ref[rs, sl]
        ym = hn / (1.0 + jnp.exp(-mo))
        acc += jnp.dot(ym.astype(BF16), w_ref[ATT_WIDTH + hd * M_DV:ATT_WIDTH + (hd + 1) * M_DV, :],
                       preferred_element_type=F32)
    x1 = x_ref[rs, :] + gt1_ref[pl.ds(row, 1), :] * acc
    x1_ref[rs, :] = x1
    h2 = (_rms(x1) * g2_ref[...]) * (1.0 + sc2_ref[pl.ds(row, 1), :]) + sh2_ref[pl.ds(row, 1), :]
    h2_ref[rs, :] = h2
    h_hi = h2.astype(BF16)
    h_lo = (h2 - h_hi.astype(F32)).astype(BF16)
    logits = (jnp.dot(h_hi, wrh_ref[...], preferred_element_type=F32)
              + jnp.dot(h_lo, wrh_ref[...], preferred_element_type=F32)
              + jnp.dot(h_hi, wrl_ref[...], preferred_element_type=F32)) + br_ref[...]
    lane = lax.broadcasted_iota(jnp.int32, (TM4, LANE), 1)
    vals, idxs = [], []
    for _ in range(TOP_K):
        mx = jnp.max(logits, axis=-1, keepdims=True)
        ik = jnp.min(jnp.where(logits == mx, lane, LANE), axis=-1, keepdims=True)
        vals.append(mx)
        idxs.append(ik)
        logits = jnp.where(lane == ik, -jnp.inf, logits)
    es = [jnp.exp(vk - vals[0]) for vk in vals]
    tot = es[0] + es[1] + es[2] + es[3]
    chosen = jnp.zeros((TM4, LANE), F32)
    for kk in range(TOP_K):
        chosen = jnp.where(lane == idxs[kk], 1.0, chosen)
    ri = lax.broadcasted_iota(jnp.int32, (TM4, TM4), 0)
    ci = lax.broadcasted_iota(jnp.int32, (TM4, TM4), 1)
    before = jnp.where(ci < ri, 1.0, 0.0).astype(BF16)
    earlier = jnp.dot(before, chosen.astype(BF16), preferred_element_type=F32)
    ti = jnp.zeros((TM4, LANE), jnp.int32)
    tr = jnp.zeros((TM4, LANE), jnp.int32)
    tw = jnp.zeros((TM4, LANE), F32)
    for kk in range(TOP_K):
        rk = jnp.sum(jnp.where(lane == idxs[kk], earlier, 0.0), axis=-1, keepdims=True)
        ti = jnp.where(lane == kk, idxs[kk], ti)
        tr = jnp.where(lane == kk, rk.astype(jnp.int32), tr)
        tw = jnp.where(lane == kk, es[kk] / tot, tw)
    ti_ref[rs, :] = ti
    tr_ref[rs, :] = tr
    tw_ref[rs, :] = tw
    tc_ref[hv] = jnp.sum(chosen, axis=0, keepdims=True).astype(jnp.int32)


def _out_projection(a_lat, hf, hb, mo, x2, w_out_b, g_mlstm, gt1, sh2, sc2, g2, wr_hi, wr_lo, br_p):
    full = lambda shape: pl.BlockSpec(shape, lambda i: (0, 0))
    rows = TILES_PER_STEP4 * TM4
    row_blk = lambda w: pl.BlockSpec((rows, w), lambda i: (i, 0))
    return pl.pallas_call(
        _outproj_kernel,
        grid=(T_LAT // rows,),
        in_specs=[
            row_blk(ATT_WIDTH), row_blk(M_WIDTH), row_blk(M_WIDTH), row_blk(M_WIDTH),
            row_blk(D_MODEL), full((ATT_WIDTH + M_WIDTH, D_MODEL)), full((1, M_WIDTH)),
            full((MOD_ROWS, D_MODEL)), full((MOD_ROWS, D_MODEL)), full((MOD_ROWS, D_MODEL)),
            full((1, D_MODEL)), full((D_MODEL, LANE)), full((D_MODEL, LANE)), full((1, LANE)),
        ],
        out_specs=[row_blk(D_MODEL), row_blk(D_MODEL), row_blk(LANE), row_blk(LANE), row_blk(LANE),
                   pl.BlockSpec((TILES_PER_STEP4, 1, LANE), lambda i: (i, 0, 0))],
        out_shape=[
            jax.ShapeDtypeStruct((T_LAT, D_MODEL), F32),
            jax.ShapeDtypeStruct((T_LAT, D_MODEL), F32),
            jax.ShapeDtypeStruct((T_LAT, LANE), jnp.int32),
            jax.ShapeDtypeStruct((T_LAT, LANE), jnp.int32),
            jax.ShapeDtypeStruct((T_LAT, LANE), F32),
            jax.ShapeDtypeStruct((T_LAT // TM4, 1, LANE), jnp.int32),
        ],
        compiler_params=_params(1),
        name="out_proj_router",
    )(a_lat, hf, hb, mo, x2, w_out_b, g_mlstm, gt1, sh2, sc2, g2, wr_hi, wr_lo, br_p)


MB = 128
N_ASSIGN = T_LAT * TOP_K
N_BLOCKS = N_ASSIGN // MB + N_EXPERTS
P_ROWS = N_BLOCKS * MB
N_PADS = P_ROWS - N_ASSIGN
SB = 10
W_MAX = N_EXPERTS + N_BLOCKS // SB
XROWS = SB * MB


def _routing_tables(top_idx, tile_rank, tile_counts):
    n_tiles = tile_counts.shape[0]
    counts = jnp.sum(tile_counts, axis=0)
    tile_base = jnp.cumsum(tile_counts, axis=0) - tile_counts
    nblk = (counts + MB - 1) // MB
    blk_start = jnp.cumsum(nblk) - nblk
    n_valid_blk = jnp.sum(nblk)
    offs = blk_start[None, :] * MB + tile_base
    e4 = top_idx.reshape(n_tiles, -1, TOP_K, 1)
    hit = e4 == jnp.arange(N_EXPERTS, dtype=jnp.int32)
    dest = jnp.sum(jnp.where(hit, offs[:, None, None, :], 0), axis=-1)
    dest = (dest.reshape(T_LAT, TOP_K) + tile_rank).reshape(N_ASSIGN).astype(jnp.int32)
    nwork = (nblk + SB - 1) // SB
    wend = jnp.cumsum(nwork)
    wstart = wend - nwork
    n_items = wend[-1]
    wid = jnp.arange(W_MAX, dtype=jnp.int32)
    valid = wid < n_items
    we = jnp.minimum(jnp.sum((wend[None, :] <= wid[:, None]).astype(jnp.int32), axis=1),
                     N_EXPERTS - 1)
    local = wid - wstart[we]
    work_blk = jnp.where(valid, blk_start[we] + local * SB, 0)
    work_n = jnp.where(valid, jnp.clip(nblk[we] - local * SB, 0, SB), 0)
    last_e = we[jnp.maximum(n_items - 1, 0)]
    work_e = jnp.where(valid, we, last_e)
    i32 = lambda a: a.astype(jnp.int32)
    return (dest, i32(counts), i32(blk_start), i32(nblk), i32(n_valid_blk)[None],
            i32(work_e), i32(work_blk), i32(work_n))


TMD = 256
ROW_GROUP = 8


def _dispatch_kernel(dest_ref, cnt_ref, bs_ref, nb_ref, nvb_ref, h2_ref, x_hbm, zrow, sem, psem):
    i = pl.program_id(0)

    @pl.when(i == 0)
    def _():
        zrow[...] = jnp.zeros_like(zrow)

        def zero_row(r, carry):
            pltpu.make_async_copy(zrow.at[pl.ds(0, 1)], x_hbm.at[pl.ds(r, 1)], psem).start()
            return carry

        def expert(e, carry):
            first = bs_ref[e] * MB
            lax.fori_loop(first + cnt_ref[e], first + nb_ref[e] * MB, zero_row, 0)
            return carry
        lax.fori_loop(0, N_EXPERTS, expert, 0)
        lax.fori_loop(nvb_ref[0] * MB, P_ROWS, zero_row, 0)

    base = i * (TMD * TOP_K)

    def issue(tg, carry):
        t0 = pl.multiple_of(tg * ROW_GROUP, ROW_GROUP)
        for j in range(ROW_GROUP):
            for kk in range(TOP_K):
                d = dest_ref[base + (t0 + j) * TOP_K + kk]
                pltpu.make_async_copy(h2_ref.at[pl.ds(t0 + j, 1)], x_hbm.at[pl.ds(d, 1)],
                                      sem).start(priority=kk % 2)
        return carry
    lax.fori_loop(0, TMD // ROW_GROUP, issue, 0)
    for kk in range(TOP_K):
        pltpu.make_async_copy(h2_ref, x_hbm.at[pl.ds(0, TMD)], sem).wait()

    @pl.when(i == 0)
    def _():
        for _ in range(N_PADS // TMD):
            pltpu.make_async_copy(h2_ref, x_hbm.at[pl.ds(0, TMD)], psem).wait()


def _dispatch(h2, dest, counts, blk_start, nblk, n_valid_blk):
    grid_spec = pltpu.PrefetchScalarGridSpec(
        num_scalar_prefetch=5,
        grid=(T_LAT // TMD,),
        in_specs=[pl.BlockSpec((TMD, D_MODEL), lambda i, *_: (i, 0))],
        out_specs=pl.BlockSpec(memory_space=pl.ANY),
        scratch_shapes=[
            pltpu.VMEM((8, D_MODEL), F32),
            pltpu.SemaphoreType.DMA(()),
            pltpu.SemaphoreType.DMA(()),
        ],
    )
    return pl.pallas_call(
        _dispatch_kernel,
        grid_spec=grid_spec,
        out_shape=jax.ShapeDtypeStruct((P_ROWS, D_MODEL), F32),
        compiler_params=_params(1),
        name="moe_dispatch",
    )(dest, counts, blk_start, nblk, n_valid_blk, h2)


TF = 512
NF = D_FF // TF
MC = 2 * MB
TF2 = TF // 2
W_PRIORITY = 1
W_AHEAD = 2
W_SLOTS = W_AHEAD + 1
MOE_VMEM_LIMIT = 62 * 1024 * 1024


def _moe_kernel(we_ref, wb_ref, wn_ref, nvb_ref,
                x_hbm, w1_hbm, w2_hbm, w1g_ref, w2a_ref, b1g_ref, b1u_ref, b2_ref,
                y_hbm, xbuf, xb16, acc, w1u_buf, w2b_buf, xsem, ysem, wsem):
    w = pl.program_id(0)
    f = pl.program_id(1)
    nsub = wn_ref[w]
    r0 = wb_ref[w] * MB

    step = w * NF + f
    wslot = step % W_SLOTS

    def w_copies(item, ff, slot_):
        e = we_ref[item]
        up = pltpu.make_async_copy(w1_hbm.at[e, :, pl.ds(D_FF + ff * TF, TF)],
                                   w1u_buf.at[slot_], wsem.at[slot_])
        dn = pltpu.make_async_copy(w2_hbm.at[e, pl.ds(ff * TF + TF2, TF2), :],
                                   w2b_buf.at[slot_], wsem.at[slot_])
        return up, dn

    def start_w(item, ff, slot_):
        for cp in w_copies(item, ff, slot_):
            cp.start(priority=W_PRIORITY)

    @pl.when(jnp.logical_and(step == 0, nsub > 0))
    def _():
        for s in range(W_AHEAD):
            start_w(0, s, s)

    last_f = f == NF - 1
    nxt_item = w + (f + W_AHEAD) // NF
    nxt_f = (f + W_AHEAD) % NF

    @pl.when(jnp.logical_and(nxt_item < W_MAX, wn_ref[jnp.minimum(nxt_item, W_MAX - 1)] > 0))
    def _():
        start_w(nxt_item, nxt_f, (step + W_AHEAD) % W_SLOTS)

    @pl.when(nsub > 0)
    def _():
        for cp in w_copies(w, f, wslot):
            cp.wait()

    def x_copy(item, s):
        src = x_hbm.at[pl.ds(wb_ref[item] * MB + s * MB, MB)]
        return pltpu.make_async_copy(src, xbuf.at[pl.ds(s * MB, MB)], xsem)

    def start_x(item):
        def body(s, carry):
            x_copy(item, s).start()
            return carry
        lax.fori_loop(0, wn_ref[item], body, 0)

    @pl.when(jnp.logical_and(w == 0, f == 0))
    def _():
        start_x(0)

    @pl.when(f == 0)
    def _():
        def wait(s, carry):
            x_copy(w, s).wait()
            return carry
        lax.fori_loop(0, nsub, wait, 0)

        def body(s, carry):
            rs = pl.ds(pl.multiple_of(s * MB, MB), MB)
            xb16[rs, :] = xbuf[rs, :].astype(BF16)
            acc[rs, :] = jnp.broadcast_to(b2_ref[...], (MB, D_MODEL))
            return carry
        lax.fori_loop(0, nsub, body, 0)

    @pl.when(jnp.logical_and(f == 1, w + 1 < W_MAX))
    def _():
        start_x(w + 1)

    def y_copy(s):
        rs = pl.ds(pl.multiple_of(s * MB, MB), MB)
        return pltpu.make_async_copy(acc.at[rs], y_hbm.at[pl.ds(r0 + s * MB, MB)], ysem)

    def chunk(row, m):
        rs = pl.ds(row, m)
        xb = xb16[rs, :]
        gate = jnp.dot(xb, w1g_ref[...], preferred_element_type=F32) + b1g_ref[...]
        up = jnp.dot(xb, w1u_buf[wslot], preferred_element_type=F32) + b1u_ref[...]
        gate = jnp.minimum(gate, SWIGLU_LIMIT)
        up = jnp.clip(up, -SWIGLU_LIMIT, SWIGLU_LIMIT)
        glu = gate / (1.0 + jnp.exp(-SWIGLU_ALPHA * gate))
        act = (up + 1.0) * glu
        acc[rs, :] += (jnp.dot(act[:, :TF2], w2a_ref[...], preferred_element_type=F32)
                       + jnp.dot(act[:, TF2:], w2b_buf[wslot], preferred_element_type=F32))

    def write_back(first_blk, n_blk):
        @pl.when(last_f)
        def _():
            for s in range(n_blk):
                y_copy(first_blk + s).start()

    nquad = nsub // 4
    rem = nsub - nquad * 4

    def quad(c, carry):
        row = pl.multiple_of(c * (2 * MC), 2 * MC)
        chunk(row, 2 * MC)
        write_back(c * 4, 4)
        return carry
    lax.fori_loop(0, nquad, quad, 0)

    @pl.when(rem >= 2)
    def _():
        chunk(pl.multiple_of(nquad * (2 * MC), MC), MC)
        write_back(nquad * 4, 2)

    @pl.when(rem % 2 == 1)
    def _():
        chunk(pl.multiple_of((nsub - 1) * MB, MB), MB)
        write_back(nsub - 1, 1)

    @pl.when(last_f)
    def _():
        def wait(s, carry):
            y_copy(s).wait()
            return carry
        lax.fori_loop(0, nsub, wait, 0)

    @pl.when(jnp.logical_and(w == W_MAX - 1, f == NF - 1))
    def _():
        acc[0:MB, :] = jnp.zeros((MB, D_MODEL), F32)

        def tail_copy(bk):
            return pltpu.make_async_copy(acc.at[0:MB], y_hbm.at[pl.ds(bk * MB, MB)], ysem)

        def start(bk, carry):
            tail_copy(bk).start()
            return carry
        lax.fori_loop(nvb_ref[0], N_BLOCKS, start, 0)

        def wait(bk, carry):
            tail_copy(bk).wait()
            return carry
        lax.fori_loop(nvb_ref[0], N_BLOCKS, wait, 0)


def _moe(x_sorted, w1, b1, w2, b2, work_e, work_blk, work_n, n_valid_blk):
    b1r = b1.reshape(N_EXPERTS, 1, 2 * D_FF)
    b2r = b2.reshape(N_EXPERTS, 1, D_MODEL)
    grid_spec = pltpu.PrefetchScalarGridSpec(
        num_scalar_prefetch=4,
        grid=(W_MAX, NF),
        in_specs=[
            pl.BlockSpec(memory_space=pl.ANY),
            pl.BlockSpec(memory_space=pl.ANY),
            pl.BlockSpec(memory_space=pl.ANY),
            pl.BlockSpec((None, D_MODEL, TF), lambda w, f, we, *_: (we[w], 0, f)),
            pl.BlockSpec((None, TF2, D_MODEL), lambda w, f, we, *_: (we[w], 2 * f, 0)),
            pl.BlockSpec((None, 1, TF), lambda w, f, we, *_: (we[w], 0, f)),
            pl.BlockSpec((None, 1, TF), lambda w, f, we, *_: (we[w], 0, NF + f)),
            pl.BlockSpec((None, 1, D_MODEL), lambda w, f, we, *_: (we[w], 0, 0)),
        ],
        out_specs=pl.BlockSpec(memory_space=pl.ANY),
        scratch_shapes=[
            pltpu.VMEM((XROWS, D_MODEL), F32),
            pltpu.VMEM((XROWS, D_MODEL), BF16),
            pltpu.VMEM((XROWS, D_MODEL), F32),
            pltpu.VMEM((W_SLOTS, D_MODEL, TF), F32),
            pltpu.VMEM((W_SLOTS, TF2, D_MODEL), F32),
            pltpu.SemaphoreType.DMA(()),
            pltpu.SemaphoreType.DMA(()),
            pltpu.SemaphoreType.DMA((W_SLOTS,)),
        ],
    )
    return pl.pallas_call(
        _moe_kernel,
        grid_spec=grid_spec,
        out_shape=jax.ShapeDtypeStruct((P_ROWS, D_MODEL), F32),
        compiler_params=pltpu.CompilerParams(
            dimension_semantics=("arbitrary", "arbitrary"), vmem_limit_bytes=MOE_VMEM_LIMIT),
        name="moe_experts",
    )(work_e, work_blk, work_n, n_valid_blk, x_sorted, w1, w2, w1, w2, b1r, b1r, b2r)


TM7 = 256
N_TILES7 = T_LAT // TM7
TILES_PER_SEQ7 = SEQ // TM7


def _final_kernel(pos_ref, x1_ref, tw_ref, gt2_ref, gf_ref, y_hbm, o_ref, ybuf, sem):
    i = pl.program_id(0)
    slot = i % 2

    def start_gather(tile, slot_):
        base = tile * (TM7 * TOP_K)

        def body(tg, carry):
            t0 = pl.multiple_of(tg * ROW_GROUP, ROW_GROUP)
            for j in range(ROW_GROUP):
                for kk in range(TOP_K):
                    p = pos_ref[base + (t0 + j) * TOP_K + kk]
                    pltpu.make_async_copy(y_hbm.at[pl.ds(p, 1)],
                                          ybuf.at[slot_, kk, pl.ds(t0 + j, 1)],
                                          sem.at[slot_]).start(priority=kk % 2)
            return carry
        lax.fori_loop(0, TM7 // ROW_GROUP, body, 0)

    @pl.when(i == 0)
    def _():
        start_gather(0, 0)

    @pl.when(i + 1 < N_TILES7)
    def _():
        start_gather(i + 1, 1 - slot)

    for kk in range(TOP_K):
        pltpu.make_async_copy(y_hbm.at[pl.ds(0, TM7)], ybuf.at[slot, kk], sem.at[slot]).wait()

    row = i // TILES_PER_SEQ7
    tw = tw_ref[...]
    mix = jnp.zeros((TM7, D_MODEL), F32)
    for kk in range(TOP_K):
        mix += ybuf[slot, kk] * tw[:, kk:kk + 1]
    xo = x1_ref[...] + gt2_ref[pl.ds(row, 1), :] * mix
    o_ref[...] = _rms(xo) * gf_ref[...]


def _final(dest, x1, top_w, gt2, g_final, y_sorted):
    grid_spec = pltpu.PrefetchScalarGridSpec(
        num_scalar_prefetch=1,
        grid=(N_TILES7,),
        in_specs=[
            pl.BlockSpec((TM7, D_MODEL), lambda i, *_: (i, 0)),
            pl.BlockSpec((TM7, LANE), lambda i, *_: (i, 0)),
            pl.BlockSpec((MOD_ROWS, D_MODEL), lambda i, *_: (0, 0)),
            pl.BlockSpec((1, D_MODEL), lambda i, *_: (0, 0)),
            pl.BlockSpec(memory_space=pl.ANY),
        ],
        out_specs=pl.BlockSpec((TM7, D_MODEL), lambda i, *_: (i, 0)),
        scratch_shapes=[
            pltpu.VMEM((2, TOP_K, TM7, D_MODEL), F32),
            pltpu.SemaphoreType.DMA((2,)),
        ],
    )
    return pl.pallas_call(
        _final_kernel,
        grid_spec=grid_spec,
        out_shape=jax.ShapeDtypeStruct((T_LAT, D_MODEL), F32),
        compiler_params=_params(1),
        name="combine_final_norm",
    )(dest, x1, top_w, gt2, g_final, y_sorted)


def _rope_tables():
    rows = SEQ // GRID_W
    row = jnp.repeat(jnp.arange(rows, dtype=F32), GRID_W)
    col = jnp.tile(jnp.arange(GRID_W, dtype=F32), rows)
    inv = ROPE_THETA ** (-jnp.arange(0, AXIS_DIM, 2, dtype=F32) / AXIS_DIM)
    ang = jnp.concatenate([row[:, None] * inv, col[:, None] * inv], axis=-1)
    cos = jnp.repeat(jnp.cos(ang), 2, axis=-1)
    sin = jnp.repeat(jnp.sin(ang), 2, axis=-1)
    sign = jnp.tile(jnp.array([-1.0, 1.0], F32), HEAD_DIM // 2)
    cos_t = jnp.concatenate([cos, jnp.ones((TM1, HEAD_DIM), F32)], axis=0)
    sin_t = jnp.concatenate([sin * sign, jnp.zeros((TM1, HEAD_DIM), F32)], axis=0)
    return cos_t, sin_t


def kernel(x, c, ctx, c_ctx, w_mod, b_mod, g_norm1, w_in, b_in, g_q, g_k, g_mlstm, w_out,
           g_norm2, w_router, b_router, w1, b1, w2, b2, g_final):
    x2 = x.reshape(T_LAT, D_MODEL)
    ctx2 = ctx.reshape(T_CTX, D_MODEL)
    cos_t, sin_t = _rope_tables()
    assert w_mod.shape[0] == 1
    l = 0
    c_rows = jnp.concatenate(
        [c, c_ctx[None, :], jnp.zeros((MOD_ROWS - BATCH - 1, D_MODEL), F32)], axis=0)
    mod = _modulation(c_rows, w_mod[l], b_mod[l][None, :])
    sh1, sc1, gt1, sh2, sc2, gt2 = [mod[:, k * D_MODEL:(k + 1) * D_MODEL]
                                    for k in range(N_ADALN)]
    pad = IN_COLS_PAD - IN_COLS
    w_main = w_in[l].astype(BF16)
    w_gate = jnp.pad(w_in[l][:, C_G:], ((0, 0), (0, pad))).astype(BF16)
    b_main = b_in[l][None, :C_G]
    b_gate = jnp.pad(b_in[l][C_G:], (0, pad))[None, :]
    q, k, v, mq, mkt, mv, mo, gates = _in_projection(
        x2, ctx2, sh1, sc1, g_norm1[l][None, :], w_main, b_main, w_gate, b_gate,
        g_q[l][None, :], g_k[l][None, :], cos_t, sin_t)
    a_lat = _attention(q, k, v)
    hf, hb = _mlstm(mq, mkt, mv, gates)
    wr_p = jnp.pad(w_router[l], ((0, 0), (0, LANE - N_EXPERTS)))
    wr_hi = wr_p.astype(BF16)
    wr_lo = (wr_p - wr_hi.astype(F32)).astype(BF16)
    br_p = jnp.pad(b_router[l], (0, LANE - N_EXPERTS), constant_values=NEG_BIG)[None, :]
    x1, h2, top_i, top_r, top_w, tile_cnt = _out_projection(
        a_lat, hf, hb, mo, x2, w_out[l].astype(BF16), g_mlstm[l][None, :],
        gt1, sh2, sc2, g_norm2[l][None, :], wr_hi, wr_lo, br_p)
    dest, counts, blk_start, nblk, n_valid_blk, work_e, work_blk, work_n = _routing_tables(
        top_i[:, :TOP_K], top_r[:, :TOP_K], tile_cnt[:, 0, :N_EXPERTS])
    x_sorted = _dispatch(h2, dest, counts, blk_start, nblk, n_valid_blk)
    y_sorted = _moe(x_sorted, w1[l], b1[l], w2[l], b2[l], work_e, work_blk, work_n, n_valid_blk)
    out = _final(dest, x1, top_w, gt2, g_final[None, :], y_sorted)
    return out.reshape(BATCH, SEQ, D_MODEL)
```

```python
import jax
import jax.numpy as jnp
from jax import lax
from jax.experimental import pallas as pl
from jax.experimental.pallas import tpu as pltpu

F32 = jnp.float32
BF16 = jnp.bfloat16
HIGHEST = lax.Precision.HIGHEST

D_MODEL = 2048
BATCH = 2
SEQ = 4096
CTX_LEN = 256
GRID_W = 64
N_ADALN = 6
EPS = 1e-6
N_HEADS = 8
KV_HEADS = 2
GROUP = N_HEADS // KV_HEADS
HEAD_DIM = 128
AXIS_DIM = HEAD_DIM // 2
ROPE_THETA = 10000.0
ATT_WIDTH = N_HEADS * HEAD_DIM
ATT_SCALE = HEAD_DIM ** -0.5
M_HEADS = 4
M_DK = 128
M_DV = 256
M_CHUNK = 128
M_WIDTH = M_HEADS * M_DV
N_EXPERTS = 32
TOP_K = 4
D_FF = D_MODEL
SWIGLU_LIMIT = 7.0
SWIGLU_ALPHA = 1.702

T_LAT = BATCH * SEQ
T_CTX = BATCH * CTX_LEN
T_ALL = T_LAT + T_CTX
KV_W = KV_HEADS * HEAD_DIM
MQK_W = M_HEADS * M_DK
LANE = 128

C_Q = 0
C_K = C_Q + ATT_WIDTH
C_V = C_K + KV_W
C_MQ = C_V + KV_W
C_MK = C_MQ + MQK_W
C_MV = C_MK + MQK_W
C_MO = C_MV + M_WIDTH
C_G = C_MO + M_WIDTH
IN_COLS = C_G + 4 * M_HEADS
IN_COLS_PAD = C_G + LANE

VMEM_LIMIT = 56 * 1024 * 1024


def _params(n_axes):
    return pltpu.CompilerParams(
        dimension_semantics=("arbitrary",) * n_axes, vmem_limit_bytes=VMEM_LIMIT)


def _rms(x):
    return x * lax.rsqrt(jnp.mean(x * x, axis=-1, keepdims=True) + EPS)


MOD_ROWS = 8
MOD_TN = 1024


def _mod_kernel(c_ref, w_ref, b_ref, o_ref):
    c = c_ref[...]
    s = c / (1.0 + jnp.exp(-c))
    o_ref[...] = jnp.dot(s, w_ref[...], preferred_element_type=F32) + b_ref[...]


def _modulation(c_rows, w_mod, b_mod):
    n = w_mod.shape[1]
    return pl.pallas_call(
        _mod_kernel,
        grid=(n // MOD_TN,),
        in_specs=[
            pl.BlockSpec((MOD_ROWS, D_MODEL), lambda j: (0, 0)),
            pl.BlockSpec((D_MODEL, MOD_TN), lambda j: (0, j)),
            pl.BlockSpec((1, MOD_TN), lambda j: (0, j)),
        ],
        out_specs=pl.BlockSpec((MOD_ROWS, MOD_TN), lambda j: (0, j)),
        out_shape=jax.ShapeDtypeStruct((MOD_ROWS, n), F32),
        compiler_params=_params(1),
        name="adaln_mod",
    )(c_rows, w_mod, b_mod)


TM1 = 256
N_LAT_TILES1 = T_LAT // TM1
N_TILES1 = T_ALL // TM1
TILES_PER_SEQ1 = SEQ // TM1
PCH = 512
LOG2E = 1.4426950408889634
Q_SCALE = ATT_SCALE * LOG2E
V_AUG = 2 * HEAD_DIM


def _inproj_kernel(x_ref, ctx_ref, sh_ref, sc_ref, g1_ref, w_ref, b_ref, wg_ref, bg_ref,
                   gq_ref, gk_ref, cos_ref, sin_ref,
                   q_ref, k_ref, v_ref, mq_ref, mkt_ref, mv_ref, mo_ref, g_ref, h_scr):
    i = pl.program_id(0)
    row = jnp.where(i < N_LAT_TILES1, i // TILES_PER_SEQ1, BATCH)
    sc = sc_ref[pl.ds(row, 1), :]
    sh = sh_ref[pl.ds(row, 1), :]

    def norm_mod(xv):
        return (_rms(xv) * g1_ref[...]) * (1.0 + sc) + sh

    @pl.when(i < N_LAT_TILES1)
    def _():
        h_scr[...] = norm_mod(x_ref[...]).astype(BF16)

    @pl.when(i >= N_LAT_TILES1)
    def _():
        h_scr[...] = norm_mod(ctx_ref[...]).astype(BF16)

    h = h_scr[...]

    def proj(c0, c1):
        return jnp.dot(h, w_ref[:, c0:c1], preferred_element_type=F32) + b_ref[:, c0:c1]

    cos_f = cos_ref[...]
    sin_s = sin_ref[...]
    lane = lax.broadcasted_iota(jnp.int32, (TM1, HEAD_DIM), 1)
    even = (lane % 2) == 0

    def head_norm_rope(a, g):
        y = _rms(a) * g
        nxt = pltpu.roll(y, HEAD_DIM - 1, 1)
        prv = pltpu.roll(y, 1, 1)
        return y * cos_f + jnp.where(even, nxt, prv) * sin_s

    for c0 in range(C_Q, C_K, PCH):
        acc = proj(c0, c0 + PCH)
        for hh in range(PCH // HEAD_DIM):
            a = acc[:, hh * HEAD_DIM:(hh + 1) * HEAD_DIM]
            r = head_norm_rope(a, gq_ref[...]) * Q_SCALE
            q_ref[:, c0 + hh * HEAD_DIM:c0 + (hh + 1) * HEAD_DIM] = r.astype(BF16)
    acc = proj(C_K, C_MQ)
    for hh in range(KV_HEADS):
        a = acc[:, hh * HEAD_DIM:(hh + 1) * HEAD_DIM]
        k_ref[:, hh * HEAD_DIM:(hh + 1) * HEAD_DIM] = head_norm_rope(a, gk_ref[...]).astype(BF16)
    for hh in range(KV_HEADS):
        v_ref[:, hh * V_AUG:hh * V_AUG + HEAD_DIM] = (
            acc[:, KV_W + hh * HEAD_DIM:KV_W + (hh + 1) * HEAD_DIM].astype(BF16))
        v_ref[:, hh * V_AUG + HEAD_DIM:(hh + 1) * V_AUG] = jnp.ones((TM1, HEAD_DIM), BF16)
    mq_ref[...] = proj(C_MQ, C_MK).astype(BF16)
    mk = proj(C_MK, C_MV) * (M_DK ** -0.5)
    mkt_ref[...] = mk.T.astype(BF16)
    for c0 in range(C_MV, C_MO, PCH):
        mv_ref[:, c0 - C_MV:c0 - C_MV + PCH] = proj(c0, c0 + PCH).astype(BF16)
    for c0 in range(C_MO, C_G, PCH):
        mo_ref[:, c0 - C_MO:c0 - C_MO + PCH] = proj(c0, c0 + PCH)
    g_ref[...] = jnp.dot(h, wg_ref[...], preferred_element_type=F32) + bg_ref[...]


def _in_projection(x2, ctx2, sh1, sc1, g1, w_main, b_main, w_gate, b_gate, g_q, g_k, cos_t, sin_t):
    lat_idx = lambda i: (jnp.minimum(i, N_LAT_TILES1 - 1), 0)
    ctx_idx = lambda i: (jnp.maximum(i - N_LAT_TILES1, 0), 0)
    rope_idx = lambda i: (jnp.where(i < N_LAT_TILES1, i % TILES_PER_SEQ1, TILES_PER_SEQ1), 0)
    full = lambda shape: pl.BlockSpec(shape, lambda i: (0, 0))
    row_blk = lambda w: pl.BlockSpec((TM1, w), lambda i: (i, 0))
    out_shapes = [
        jax.ShapeDtypeStruct((T_ALL, ATT_WIDTH), BF16),
        jax.ShapeDtypeStruct((T_ALL, KV_W), BF16),
        jax.ShapeDtypeStruct((T_ALL, KV_HEADS * V_AUG), BF16),
        jax.ShapeDtypeStruct((T_ALL, MQK_W), BF16),
        jax.ShapeDtypeStruct((MQK_W, T_ALL), BF16),
        jax.ShapeDtypeStruct((T_ALL, M_WIDTH), BF16),
        jax.ShapeDtypeStruct((T_ALL, M_WIDTH), F32),
        jax.ShapeDtypeStruct((T_ALL, LANE), F32),
    ]
    out_specs = [
        row_blk(ATT_WIDTH), row_blk(KV_W), row_blk(KV_HEADS * V_AUG), row_blk(MQK_W),
        pl.BlockSpec((MQK_W, TM1), lambda i: (0, i)),
        row_blk(M_WIDTH), row_blk(M_WIDTH), row_blk(LANE),
    ]
    return pl.pallas_call(
        _inproj_kernel,
        grid=(N_TILES1,),
        in_specs=[
            pl.BlockSpec((TM1, D_MODEL), lat_idx),
            pl.BlockSpec((TM1, D_MODEL), ctx_idx),
            full((MOD_ROWS, D_MODEL)), full((MOD_ROWS, D_MODEL)), full((1, D_MODEL)),
            pl.BlockSpec((D_MODEL, C_G), lambda i: (0, 0), pipeline_mode=pl.Buffered(1)),
            full((1, C_G)), full((D_MODEL, LANE)), full((1, LANE)),
            full((1, HEAD_DIM)), full((1, HEAD_DIM)),
            pl.BlockSpec((TM1, HEAD_DIM), rope_idx),
            pl.BlockSpec((TM1, HEAD_DIM), rope_idx),
        ],
        out_specs=out_specs,
        out_shape=out_shapes,
        scratch_shapes=[pltpu.VMEM((TM1, D_MODEL), BF16)],
        compiler_params=_params(1),
        name="in_proj",
    )(x2, ctx2, sh1, sc1, g1, w_main, b_main, w_gate, b_gate, g_q, g_k, cos_t, sin_t)


TQ = 512
CK = 512
GQ_W = GROUP * HEAD_DIM


def _attn_kernel(q_ref, kl_ref, vl_ref, kc_ref, vc_ref, o_ref):
    chunks = [(kl_ref, vl_ref, c * CK, CK) for c in range(SEQ // CK)]
    chunks.append((kc_ref, vc_ref, 0, CTX_LEN))
    for g in range(GROUP):
        qh = q_ref[:, g * HEAD_DIM:(g + 1) * HEAD_DIM]
        m = jnp.full((TQ, 1), -jnp.inf, F32)
        acc = jnp.zeros((TQ, V_AUG), F32)
        for kr, vr, st, sz in chunks:
            s = lax.dot_general(qh, kr[st:st + sz, :], (((1,), (1,)), ((), ())),
                                preferred_element_type=F32)
            m_new = jnp.maximum(m, jnp.max(s, axis=-1, keepdims=True))
            p = jnp.exp2(s - m_new)
            alpha = jnp.exp2(m - m_new)
            acc = alpha * acc + jnp.dot(p.astype(BF16), vr[st:st + sz, :],
                                        preferred_element_type=F32)
            m = m_new
        o_ref[:, g * HEAD_DIM:(g + 1) * HEAD_DIM] = (
            acc[:, :HEAD_DIM] / acc[:, HEAD_DIM:]).astype(BF16)


def _attention(q, k, v):
    nq = SEQ // TQ
    ctx_blk0 = T_LAT // CTX_LEN
    lat_k = pl.BlockSpec((SEQ, HEAD_DIM), lambda b, h, i: (b, h))
    lat_v = pl.BlockSpec((SEQ, V_AUG), lambda b, h, i: (b, h))
    ctx_k = pl.BlockSpec((CTX_LEN, HEAD_DIM), lambda b, h, i: (ctx_blk0 + b, h))
    ctx_v = pl.BlockSpec((CTX_LEN, V_AUG), lambda b, h, i: (ctx_blk0 + b, h))
    q_blk = pl.BlockSpec((TQ, GQ_W), lambda b, h, i: (b * nq + i, h))
    return pl.pallas_call(
        _attn_kernel,
        grid=(BATCH, KV_HEADS, nq),
        in_specs=[q_blk, lat_k, lat_v, ctx_k, ctx_v],
        out_specs=q_blk,
        out_shape=jax.ShapeDtypeStruct((T_LAT, ATT_WIDTH), BF16),
        compiler_params=_params(3),
        name="gqa_attention",
    )(q, k, v, k, v)


N_CHUNKS = SEQ // M_CHUNK
CTX_CHUNKS = CTX_LEN // M_CHUNK
N_CHAINS = 2 * M_HEADS


def _log_sigmoid(x):
    return jnp.minimum(x, 0.0) - jnp.log(1.0 + jnp.exp(-jnp.abs(x)))


def _mlstm_kernel(qf_ref, qb_ref, qc_ref, kf_ref, kb_ref, kc_ref, vf_ref, vb_ref, vc_ref,
                  gf_ref, gb_ref, gc_ref, hf_ref, hb_ref, *state):
    c_scr = state[0:N_CHAINS]
    n_scr = state[N_CHAINS:2 * N_CHAINS]
    m_scr = state[2 * N_CHAINS:3 * N_CHAINS]
    j = pl.program_id(1)
    ri = lax.broadcasted_iota(jnp.int32, (M_CHUNK, M_CHUNK), 0)
    ci = lax.broadcasted_iota(jnp.int32, (M_CHUNK, M_CHUNK), 1)
    lower = ci <= ri
    upper = ci >= ri
    tri_l = lower.astype(F32)
    tri_u = upper.astype(F32)

    def gate_tables(g, rev):
        gt = g.T
        mcol, mrow = (tri_u, tri_l) if rev else (tri_l, tri_u)
        bcol = jnp.dot(mcol, _log_sigmoid(g), precision=HIGHEST, preferred_element_type=F32)
        brow = jnp.dot(_log_sigmoid(gt), mrow, precision=HIGHEST, preferred_element_type=F32)
        return gt, bcol, brow

    def chain_step(chain, q, kt, v, tabs, rev, h_out):
        gt, bcol, brow = tabs
        icol = chain
        fcol = N_CHAINS + chain
        last = 0 if rev else M_CHUNK - 1
        b_col = bcol[:, fcol:fcol + 1]
        b_row = brow[fcol:fcol + 1, :]
        i_row = gt[icol:icol + 1, :]
        tot = brow[fcol:fcol + 1, last:last + 1]
        m_old = m_scr[chain][:1, :1]
        c_old = c_scr[chain][...]
        n_old = n_scr[chain][...]
        v_ones = jnp.concatenate([v, jnp.ones((M_CHUNK, LANE), BF16)], axis=1)
        c_row = i_row - b_row
        if h_out is not None:
            mask = upper if rev else lower
            mm = jnp.maximum(m_old, jnp.max(jnp.where(mask, c_row, -jnp.inf), axis=-1, keepdims=True))
            w_intra = jnp.exp(jnp.where(mask, c_row - mm, -jnp.inf))
            w_inter = jnp.exp(m_old - mm)
            s = jnp.dot(q, kt, preferred_element_type=F32) * w_intra
            qc = jnp.dot(q, c_old.astype(BF16), preferred_element_type=F32)
            qn = jnp.dot(q, n_old.astype(BF16), preferred_element_type=F32)[:, :1]
            sv = jnp.dot(s.astype(BF16), v_ones, preferred_element_type=F32)
            num = sv[:, :M_DV] + w_inter * qc
            den = sv[:, M_DV:M_DV + 1] + w_inter * qn
            h_out(num / jnp.maximum(jnp.abs(den), jnp.exp(-(b_col + mm))))
        g_row = tot + c_row
        m_new = jnp.maximum(tot + m_old, jnp.max(g_row, axis=-1, keepdims=True))
        w_state = jnp.exp(g_row - m_new)
        decay = jnp.exp(tot + m_old - m_new)
        kw = (kt.astype(F32) * w_state).astype(BF16)
        kv = jnp.dot(kw, v_ones, preferred_element_type=F32)
        c_scr[chain][...] = decay * c_old + kv[:, :M_DV]
        n_scr[chain][...] = decay * n_old + kv[:, M_DV:]
        m_scr[chain][...] = jnp.broadcast_to(m_new, (8, LANE))

    def run_chunk(q_ref, kt_ref, v_ref, g_ref, r0, rev, h_ref):
        tabs = gate_tables(g_ref[r0:r0 + M_CHUNK, :], rev)
        for hd in range(M_HEADS):
            chain = (M_HEADS if rev else 0) + hd
            q = q_ref[r0:r0 + M_CHUNK, hd * M_DK:(hd + 1) * M_DK]
            kt = kt_ref[hd * M_DK:(hd + 1) * M_DK, r0:r0 + M_CHUNK]
            v = v_ref[r0:r0 + M_CHUNK, hd * M_DV:(hd + 1) * M_DV]
            if h_ref is None:
                h_out = None
            else:
                def h_out(hv, hd=hd):
                    h_ref[:, hd * M_DV:(hd + 1) * M_DV] = hv
            chain_step(chain, q, kt, v, tabs, rev, h_out)

    @pl.when(j == 0)
    def _():
        for ref in state:
            ref[...] = jnp.zeros_like(ref)
        for cc in range(CTX_CHUNKS):
            run_chunk(qc_ref, kc_ref, vc_ref, gc_ref, cc * M_CHUNK, False, None)
            run_chunk(qc_ref, kc_ref, vc_ref, gc_ref, (CTX_CHUNKS - 1 - cc) * M_CHUNK, True, None)

    run_chunk(qf_ref, kf_ref, vf_ref, gf_ref, 0, False, hf_ref)
    run_chunk(qb_ref, kb_ref, vb_ref, gb_ref, 0, True, hb_ref)


def _mlstm(mq, mkt, mv, gates):
    ctx_blk0 = T_LAT // CTX_LEN
    fwd = lambda b, j: b * N_CHUNKS + j
    bwd = lambda b, j: b * N_CHUNKS + N_CHUNKS - 1 - j
    ctx = lambda b, j: ctx_blk0 + b

    def rows(width, tile, idx):
        return pl.BlockSpec((tile, width), lambda b, j: (idx(b, j), 0))

    def cols(tile, idx):
        return pl.BlockSpec((MQK_W, tile), lambda b, j: (0, idx(b, j)))

    h_shape = jax.ShapeDtypeStruct((T_LAT, M_WIDTH), F32)
    return pl.pallas_call(
        _mlstm_kernel,
        grid=(BATCH, N_CHUNKS),
        in_specs=[
            rows(MQK_W, M_CHUNK, fwd), rows(MQK_W, M_CHUNK, bwd), rows(MQK_W, CTX_LEN, ctx),
            cols(M_CHUNK, fwd), cols(M_CHUNK, bwd), cols(CTX_LEN, ctx),
            rows(M_WIDTH, M_CHUNK, fwd), rows(M_WIDTH, M_CHUNK, bwd), rows(M_WIDTH, CTX_LEN, ctx),
            rows(LANE, M_CHUNK, fwd), rows(LANE, M_CHUNK, bwd), rows(LANE, CTX_LEN, ctx),
        ],
        out_specs=[rows(M_WIDTH, M_CHUNK, fwd), rows(M_WIDTH, M_CHUNK, bwd)],
        out_shape=[h_shape, h_shape],
        scratch_shapes=([pltpu.VMEM((M_DK, M_DV), F32)] * N_CHAINS
                        + [pltpu.VMEM((M_DK, LANE), F32)] * N_CHAINS
                        + [pltpu.VMEM((8, LANE), F32)] * N_CHAINS),
        compiler_params=_params(2),
        name="mlstm_scan",
    )(mq, mq, mq, mkt, mkt, mkt, mv, mv, mv, gates, gates, gates)


TM4 = 256
TILES_PER_SEQ4 = SEQ // TM4
TILES_PER_STEP4 = 2
NEG_BIG = -1e30


def _outproj_kernel(a_ref, hf_ref, hb_ref, mo_ref, x_ref, w_ref, gm_ref, gt1_ref, sh2_ref,
                    sc2_ref, g2_ref, wrh_ref, wrl_ref, br_ref,
                    x1_ref, h2_ref, ti_ref, tr_ref, tw_ref, tc_ref):
    for hv in range(TILES_PER_STEP4):
        _outproj_tile(hv, a_ref, hf_ref, hb_ref, mo_ref, x_ref, w_ref, gm_ref, gt1_ref, sh2_ref,
                      sc2_ref, g2_ref, wrh_ref, wrl_ref, br_ref,
                      x1_ref, h2_ref, ti_ref, tr_ref, tw_ref, tc_ref)


def _outproj_tile(hv, a_ref, hf_ref, hb_ref, mo_ref, x_ref, w_ref, gm_ref, gt1_ref, sh2_ref,
                  sc2_ref, g2_ref, wrh_ref, wrl_ref, br_ref,
                  x1_ref, h2_ref, ti_ref, tr_ref, tw_ref, tc_ref):
    i = pl.program_id(0)
    row = (i * TILES_PER_STEP4 + hv) // TILES_PER_SEQ4
    rs = slice(hv * TM4, (hv + 1) * TM4)
    acc = jnp.dot(a_ref[rs, :], w_ref[0:ATT_WIDTH, :], preferred_element_type=F32)
    for hd in range(M_HEADS):
        sl = slice(hd * M_DV, (hd + 1) * M_DV)
        hn = _rms(hf_ref[rs, sl] + hb_ref[rs, sl]) * gm_ref[:, sl]
        mo = mo_ref[rs, sl]
        ym = hn / (1.0 + jnp.exp(-mo))
        acc += jnp.dot(ym.astype(BF16), w_ref[ATT_WIDTH + hd * M_DV:ATT_WIDTH + (hd + 1) * M_DV, :],
                       preferred_element_type=F32)
    x1 = x_ref[rs, :] + gt1_ref[pl.ds(row, 1), :] * acc
    x1_ref[rs, :] = x1
    h2 = (_rms(x1) * g2_ref[...]) * (1.0 + sc2_ref[pl.ds(row, 1), :]) + sh2_ref[pl.ds(row, 1), :]
    h2_ref[rs, :] = h2
    h_hi = h2.astype(BF16)
    h_lo = (h2 - h_hi.astype(F32)).astype(BF16)
    logits = (jnp.dot(h_hi, wrh_ref[...], preferred_element_type=F32)
              + jnp.dot(h_lo, wrh_ref[...], preferred_element_type=F32)
              + jnp.dot(h_hi, wrl_ref[...], preferred_element_type=F32)) + br_ref[...]
    lane = lax.broadcasted_iota(jnp.int32, (TM4, LANE), 1)
    vals, idxs = [], []
    for _ in range(TOP_K):
        mx = jnp.max(logits, axis=-1, keepdims=True)
        ik = jnp.min(jnp.where(logits == mx, lane, LANE), axis=-1, keepdims=True)
        vals.append(mx)
        idxs.append(ik)
        logits = jnp.where(lane == ik, -jnp.inf, logits)
    es = [jnp.exp(vk - vals[0]) for vk in vals]
    tot = es[0] + es[1] + es[2] + es[3]
    chosen = jnp.zeros((TM4, LANE), F32)
    for kk in range(TOP_K):
        chosen = jnp.where(lane == idxs[kk], 1.0, chosen)
    ri = lax.broadcasted_iota(jnp.int32, (TM4, TM4), 0)
    ci = lax.broadcasted_iota(jnp.int32, (TM4, TM4), 1)
    before = jnp.where(ci < ri, 1.0, 0.0).astype(BF16)
    earlier = jnp.dot(before, chosen.astype(BF16), preferred_element_type=F32)
    ti = jnp.zeros((TM4, LANE), jnp.int32)
    tr = jnp.zeros((TM4, LANE), jnp.int32)
    tw = jnp.zeros((TM4, LANE), F32)
    for kk in range(TOP_K):
        rk = jnp.sum(jnp.where(lane == idxs[kk], earlier, 0.0), axis=-1, keepdims=True)
        ti = jnp.where(lane == kk, idxs[kk], ti)
        tr = jnp.where(lane == kk, rk.astype(jnp.int32), tr)
        tw = jnp.where(lane == kk, es[kk] / tot, tw)
    ti_ref[rs, :] = ti
    tr_ref[rs, :] = tr
    tw_ref[rs, :] = tw
    tc_ref[hv] = jnp.sum(chosen, axis=0, keepdims=True).astype(jnp.int32)


def _out_projection(a_lat, hf, hb, mo, x2, w_out_b, g_mlstm, gt1, sh2, sc2, g2, wr_hi, wr_lo, br_p):
    full = lambda shape: pl.BlockSpec(shape, lambda i: (0, 0))
    rows = TILES_PER_STEP4 * TM4
    row_blk = lambda w: pl.BlockSpec((rows, w), lambda i: (i, 0))
    return pl.pallas_call(
        _outproj_kernel,
        grid=(T_LAT // rows,),
        in_specs=[
            row_blk(ATT_WIDTH), row_blk(M_WIDTH), row_blk(M_WIDTH), row_blk(M_WIDTH),
            row_blk(D_MODEL), full((ATT_WIDTH + M_WIDTH, D_MODEL)), full((1, M_WIDTH)),
            full((MOD_ROWS, D_MODEL)), full((MOD_ROWS, D_MODEL)), full((MOD_ROWS, D_MODEL)),
            full((1, D_MODEL)), full((D_MODEL, LANE)), full((D_MODEL, LANE)), full((1, LANE)),
        ],
        out_specs=[row_blk(D_MODEL), row_blk(D_MODEL), row_blk(LANE), row_blk(LANE), row_blk(LANE),
                   pl.BlockSpec((TILES_PER_STEP4, 1, LANE), lambda i: (i, 0, 0))],
        out_shape=[
            jax.ShapeDtypeStruct((T_LAT, D_MODEL), F32),
            jax.ShapeDtypeStruct((T_LAT, D_MODEL), F32),
            jax.ShapeDtypeStruct((T_LAT, LANE), jnp.int32),
            jax.ShapeDtypeStruct((T_LAT, LANE), jnp.int32),
            jax.ShapeDtypeStruct((T_LAT, LANE), F32),
            jax.ShapeDtypeStruct((T_LAT // TM4, 1, LANE), jnp.int32),
        ],
        compiler_params=_params(1),
        name="out_proj_router",
    )(a_lat, hf, hb, mo, x2, w_out_b, g_mlstm, gt1, sh2, sc2, g2, wr_hi, wr_lo, br_p)


MB = 128
N_ASSIGN = T_LAT * TOP_K
N_BLOCKS = N_ASSIGN // MB + N_EXPERTS
P_ROWS = N_BLOCKS * MB
N_PADS = P_ROWS - N_ASSIGN
SB = 10
W_MAX = N_EXPERTS + N_BLOCKS // SB
XROWS = SB * MB


def _routing_tables(top_idx, tile_rank, tile_counts):
    n_tiles = tile_counts.shape[0]
    counts = jnp.sum(tile_counts, axis=0)
    tile_base = jnp.cumsum(tile_counts, axis=0) - tile_counts
    nblk = (counts + MB - 1) // MB
    blk_start = jnp.cumsum(nblk) - nblk
    n_valid_blk = jnp.sum(nblk)
    offs = blk_start[None, :] * MB + tile_base
    e4 = top_idx.reshape(n_tiles, -1, TOP_K, 1)
    hit = e4 == jnp.arange(N_EXPERTS, dtype=jnp.int32)
    dest = jnp.sum(jnp.where(hit, offs[:, None, None, :], 0), axis=-1)
    dest = (dest.reshape(T_LAT, TOP_K) + tile_rank).reshape(N_ASSIGN).astype(jnp.int32)
    nwork = (nblk + SB - 1) // SB
    wend = jnp.cumsum(nwork)
    wstart = wend - nwork
    n_items = wend[-1]
    wid = jnp.arange(W_MAX, dtype=jnp.int32)
    valid = wid < n_items
    we = jnp.minimum(jnp.sum((wend[None, :] <= wid[:, None]).astype(jnp.int32), axis=1),
                     N_EXPERTS - 1)
    local = wid - wstart[we]
    work_blk = jnp.where(valid, blk_start[we] + local * SB, 0)
    work_n = jnp.where(valid, jnp.clip(nblk[we] - local * SB, 0, SB), 0)
    last_e = we[jnp.maximum(n_items - 1, 0)]
    work_e = jnp.where(valid, we, last_e)
    i32 = lambda a: a.astype(jnp.int32)
    return (dest, i32(counts), i32(blk_start), i32(nblk), i32(n_valid_blk)[None],
            i32(work_e), i32(work_blk), i32(work_n))


TMD = 256
ROW_GROUP = 8


def _dispatch_kernel(dest_ref, cnt_ref, bs_ref, nb_ref, nvb_ref, h2_ref, x_hbm, zrow, sem, psem):
    i = pl.program_id(0)

    @pl.when(i == 0)
    def _():
        zrow[...] = jnp.zeros_like(zrow)

        def zero_row(r, carry):
            pltpu.make_async_copy(zrow.at[pl.ds(0, 1)], x_hbm.at[pl.ds(r, 1)], psem).start()
            return carry

        def expert(e, carry):
            first = bs_ref[e] * MB
            lax.fori_loop(first + cnt_ref[e], first + nb_ref[e] * MB, zero_row, 0)
            return carry
        lax.fori_loop(0, N_EXPERTS, expert, 0)
        lax.fori_loop(nvb_ref[0] * MB, P_ROWS, zero_row, 0)

    base = i * (TMD * TOP_K)

    def issue(tg, carry):
        t0 = pl.multiple_of(tg * ROW_GROUP, ROW_GROUP)
        for j in range(ROW_GROUP):
            for kk in range(TOP_K):
                d = dest_ref[base + (t0 + j) * TOP_K + kk]
                pltpu.make_async_copy(h2_ref.at[pl.ds(t0 + j, 1)], x_hbm.at[pl.ds(d, 1)],
                                      sem).start(priority=kk % 2)
        return carry
    lax.fori_loop(0, TMD // ROW_GROUP, issue, 0)
    for kk in range(TOP_K):
        pltpu.make_async_copy(h2_ref, x_hbm.at[pl.ds(0, TMD)], sem).wait()

    @pl.when(i == 0)
    def _():
        for _ in range(N_PADS // TMD):
            pltpu.make_async_copy(h2_ref, x_hbm.at[pl.ds(0, TMD)], psem).wait()


def _dispatch(h2, dest, counts, blk_start, nblk, n_valid_blk):
    grid_spec = pltpu.PrefetchScalarGridSpec(
        num_scalar_prefetch=5,
        grid=(T_LAT // TMD,),
        in_specs=[pl.BlockSpec((TMD, D_MODEL), lambda i, *_: (i, 0))],
        out_specs=pl.BlockSpec(memory_space=pl.ANY),
        scratch_shapes=[
            pltpu.VMEM((8, D_MODEL), F32),
            pltpu.SemaphoreType.DMA(()),
            pltpu.SemaphoreType.DMA(()),
        ],
    )
    return pl.pallas_call(
        _dispatch_kernel,
        grid_spec=grid_spec,
        out_shape=jax.ShapeDtypeStruct((P_ROWS, D_MODEL), F32),
        compiler_params=_params(1),
        name="moe_dispatch",
    )(dest, counts, blk_start, nblk, n_valid_blk, h2)


TF = 512
NF = D_FF // TF
MC = 2 * MB
TF2 = TF // 2
W_PRIORITY = 1
W_AHEAD = 2
W_SLOTS = W_AHEAD + 1
MOE_VMEM_LIMIT = 62 * 1024 * 1024


def _moe_kernel(we_ref, wb_ref, wn_ref, nvb_ref,
                x_hbm, w1_hbm, w2_hbm, w1g_ref, w2a_ref, b1g_ref, b1u_ref, b2_ref,
                y_hbm, xbuf, xb16, acc, w1u_buf, w2b_buf, xsem, ysem, wsem):
    w = pl.program_id(0)
    f = pl.program_id(1)
    nsub = wn_ref[w]
    r0 = wb_ref[w] * MB

    step = w * NF + f
    wslot = step % W_SLOTS

    def w_copies(item, ff, slot_):
        e = we_ref[item]
        up = pltpu.make_async_copy(w1_hbm.at[e, :, pl.ds(D_FF + ff * TF, TF)],
                                   w1u_buf.at[slot_], wsem.at[slot_])
        dn = pltpu.make_async_copy(w2_hbm.at[e, pl.ds(ff * TF + TF2, TF2), :],
                                   w2b_buf.at[slot_], wsem.at[slot_])
        return up, dn

    def start_w(item, ff, slot_):
        for cp in w_copies(item, ff, slot_):
            cp.start(priority=W_PRIORITY)

    @pl.when(jnp.logical_and(step == 0, nsub > 0))
    def _():
        for s in range(W_AHEAD):
            start_w(0, s, s)

    last_f = f == NF - 1
    nxt_item = w + (f + W_AHEAD) // NF
    nxt_f = (f + W_AHEAD) % NF

    @pl.when(jnp.logical_and(nxt_item < W_MAX, wn_ref[jnp.minimum(nxt_item, W_MAX - 1)] > 0))
    def _():
        start_w(nxt_item, nxt_f, (step + W_AHEAD) % W_SLOTS)

    @pl.when(nsub > 0)
    def _():
        for cp in w_copies(w, f, wslot):
            cp.wait()

    def x_copy(item, s):
        src = x_hbm.at[pl.ds(wb_ref[item] * MB + s * MB, MB)]
        return pltpu.make_async_copy(src, xbuf.at[pl.ds(s * MB, MB)], xsem)

    def start_x(item):
        def body(s, carry):
            x_copy(item, s).start()
            return carry
        lax.fori_loop(0, wn_ref[item], body, 0)

    @pl.when(jnp.logical_and(w == 0, f == 0))
    def _():
        start_x(0)

    @pl.when(f == 0)
    def _():
        def wait(s, carry):
            x_copy(w, s).wait()
            return carry
        lax.fori_loop(0, nsub, wait, 0)

        def body(s, carry):
            rs = pl.ds(pl.multiple_of(s * MB, MB), MB)
            xb16[rs, :] = xbuf[rs, :].astype(BF16)
            acc[rs, :] = jnp.broadcast_to(b2_ref[...], (MB, D_MODEL))
            return carry
        lax.fori_loop(0, nsub, body, 0)

    @pl.when(jnp.logical_and(f == 1, w + 1 < W_MAX))
    def _():
        start_x(w + 1)

    def y_copy(s):
        rs = pl.ds(pl.multiple_of(s * MB, MB), MB)
        return pltpu.make_async_copy(acc.at[rs], y_hbm.at[pl.ds(r0 + s * MB, MB)], ysem)

    def chunk(row, m):
        rs = pl.ds(row, m)
        xb = xb16[rs, :]
        gate = jnp.dot(xb, w1g_ref[...], preferred_element_type=F32) + b1g_ref[...]
        up = jnp.dot(xb, w1u_buf[wslot], preferred_element_type=F32) + b1u_ref[...]
        gate = jnp.minimum(gate, SWIGLU_LIMIT)
        up = jnp.clip(up, -SWIGLU_LIMIT, SWIGLU_LIMIT)
        glu = gate / (1.0 + jnp.exp(-SWIGLU_ALPHA * gate))
        act = (up + 1.0) * glu
        acc[rs, :] += (jnp.dot(act[:, :TF2], w2a_ref[...], preferred_element_type=F32)
                       + jnp.dot(act[:, TF2:], w2b_buf[wslot], preferred_element_type=F32))

    def write_back(first_blk, n_blk):
        @pl.when(last_f)
        def _():
            for s in range(n_blk):
                y_copy(first_blk + s).start()

    nquad = nsub // 4
    rem = nsub - nquad * 4

    def quad(c, carry):
        row = pl.multiple_of(c * (2 * MC), 2 * MC)
        chunk(row, 2 * MC)
        write_back(c * 4, 4)
        return carry
    lax.fori_loop(0, nquad, quad, 0)

    @pl.when(rem >= 2)
    def _():
        chunk(pl.multiple_of(nquad * (2 * MC), MC), MC)
        write_back(nquad * 4, 2)

    @pl.when(rem % 2 == 1)
    def _():
        chunk(pl.multiple_of((nsub - 1) * MB, MB), MB)
        write_back(nsub - 1, 1)

    @pl.when(last_f)
    def _():
        def wait(s, carry):
            y_copy(s).wait()
            return carry
        lax.fori_loop(0, nsub, wait, 0)

    @pl.when(jnp.logical_and(w == W_MAX - 1, f == NF - 1))
    def _():
        acc[0:MB, :] = jnp.zeros((MB, D_MODEL), F32)

        def tail_copy(bk):
            return pltpu.make_async_copy(acc.at[0:MB], y_hbm.at[pl.ds(bk * MB, MB)], ysem)

        def start(bk, carry):
            tail_copy(bk).start()
            return carry
        lax.fori_loop(nvb_ref[0], N_BLOCKS, start, 0)

        def wait(bk, carry):
            tail_copy(bk).wait()
            return carry
        lax.fori_loop(nvb_ref[0], N_BLOCKS, wait, 0)


def _moe(x_sorted, w1, b1, w2, b2, work_e, work_blk, work_n, n_valid_blk):
    b1r = b1.reshape(N_EXPERTS, 1, 2 * D_FF)
    b2r = b2.reshape(N_EXPERTS, 1, D_MODEL)
    grid_spec = pltpu.PrefetchScalarGridSpec(
        num_scalar_prefetch=4,
        grid=(W_MAX, NF),
        in_specs=[
            pl.BlockSpec(memory_space=pl.ANY),
            pl.BlockSpec(memory_space=pl.ANY),
            pl.BlockSpec(memory_space=pl.ANY),
            pl.BlockSpec((None, D_MODEL, TF), lambda w, f, we, *_: (we[w], 0, f)),
            pl.BlockSpec((None, TF2, D_MODEL), lambda w, f, we, *_: (we[w], 2 * f, 0)),
            pl.BlockSpec((None, 1, TF), lambda w, f, we, *_: (we[w], 0, f)),
            pl.BlockSpec((None, 1, TF), lambda w, f, we, *_: (we[w], 0, NF + f)),
            pl.BlockSpec((None, 1, D_MODEL), lambda w, f, we, *_: (we[w], 0, 0)),
        ],
        out_specs=pl.BlockSpec(memory_space=pl.ANY),
        scratch_shapes=[
            pltpu.VMEM((XROWS, D_MODEL), F32),
            pltpu.VMEM((XROWS, D_MODEL), BF16),
            pltpu.VMEM((XROWS, D_MODEL), F32),
            pltpu.VMEM((W_SLOTS, D_MODEL, TF), F32),
            pltpu.VMEM((W_SLOTS, TF2, D_MODEL), F32),
            pltpu.SemaphoreType.DMA(()),
            pltpu.SemaphoreType.DMA(()),
            pltpu.SemaphoreType.DMA((W_SLOTS,)),
        ],
    )
    return pl.pallas_call(
        _moe_kernel,
        grid_spec=grid_spec,
        out_shape=jax.ShapeDtypeStruct((P_ROWS, D_MODEL), F32),
        compiler_params=pltpu.CompilerParams(
            dimension_semantics=("arbitrary", "arbitrary"), vmem_limit_bytes=MOE_VMEM_LIMIT),
        name="moe_experts",
    )(work_e, work_blk, work_n, n_valid_blk, x_sorted, w1, w2, w1, w2, b1r, b1r, b2r)


TM7 = 256
N_TILES7 = T_LAT // TM7
TILES_PER_SEQ7 = SEQ // TM7


def _final_kernel(pos_ref, x1_ref, tw_ref, gt2_ref, gf_ref, y_hbm, o_ref, ybuf, sem):
    i = pl.program_id(0)
    slot = i % 2

    def start_gather(tile, slot_):
        base = tile * (TM7 * TOP_K)

        def body(tg, carry):
            t0 = pl.multiple_of(tg * ROW_GROUP, ROW_GROUP)
            for j in range(ROW_GROUP):
                for kk in range(TOP_K):
                    p = pos_ref[base + (t0 + j) * TOP_K + kk]
                    pltpu.make_async_copy(y_hbm.at[pl.ds(p, 1)],
                                          ybuf.at[slot_, kk, pl.ds(t0 + j, 1)],
                                          sem.at[slot_]).start(priority=kk % 2)
            return carry
        lax.fori_loop(0, TM7 // ROW_GROUP, body, 0)

    @pl.when(i == 0)
    def _():
        start_gather(0, 0)

    @pl.when(i + 1 < N_TILES7)
    def _():
        start_gather(i + 1, 1 - slot)

    for kk in range(TOP_K):
        pltpu.make_async_copy(y_hbm.at[pl.ds(0, TM7)], ybuf.at[slot, kk], sem.at[slot]).wait()

    row = i // TILES_PER_SEQ7
    tw = tw_ref[...]
    mix = jnp.zeros((TM7, D_MODEL), F32)
    for kk in range(TOP_K):
        mix += ybuf[slot, kk] * tw[:, kk:kk + 1]
    xo = x1_ref[...] + gt2_ref[pl.ds(row, 1), :] * mix
    o_ref[...] = _rms(xo) * gf_ref[...]


def _final(dest, x1, top_w, gt2, g_final, y_sorted):
    grid_spec = pltpu.PrefetchScalarGridSpec(
        num_scalar_prefetch=1,
        grid=(N_TILES7,),
        in_specs=[
            pl.BlockSpec((TM7, D_MODEL), lambda i, *_: (i, 0)),
            pl.BlockSpec((TM7, LANE), lambda i, *_: (i, 0)),
            pl.BlockSpec((MOD_ROWS, D_MODEL), lambda i, *_: (0, 0)),
            pl.BlockSpec((1, D_MODEL), lambda i, *_: (0, 0)),
            pl.BlockSpec(memory_space=pl.ANY),
        ],
        out_specs=pl.BlockSpec((TM7, D_MODEL), lambda i, *_: (i, 0)),
        scratch_shapes=[
            pltpu.VMEM((2, TOP_K, TM7, D_MODEL), F32),
            pltpu.SemaphoreType.DMA((2,)),
        ],
    )
    return pl.pallas_call(
        _final_kernel,
        grid_spec=grid_spec,
        out_shape=jax.ShapeDtypeStruct((T_LAT, D_MODEL), F32),
        compiler_params=_params(1),
        name="combine_final_norm",
    )(dest, x1, top_w, gt2, g_final, y_sorted)


def _rope_tables():
    rows = SEQ // GRID_W
    row = jnp.repeat(jnp.arange(rows, dtype=F32), GRID_W)
    col = jnp.tile(jnp.arange(GRID_W, dtype=F32), rows)
    inv = ROPE_THETA ** (-jnp.arange(0, AXIS_DIM, 2, dtype=F32) / AXIS_DIM)
    ang = jnp.concatenate([row[:, None] * inv, col[:, None] * inv], axis=-1)
    cos = jnp.repeat(jnp.cos(ang), 2, axis=-1)
    sin = jnp.repeat(jnp.sin(ang), 2, axis=-1)
    sign = jnp.tile(jnp.array([-1.0, 1.0], F32), HEAD_DIM // 2)
    cos_t = jnp.concatenate([cos, jnp.ones((TM1, HEAD_DIM), F32)], axis=0)
    sin_t = jnp.concatenate([sin * sign, jnp.zeros((TM1, HEAD_DIM), F32)], axis=0)
    return cos_t, sin_t


def kernel(x, c, ctx, c_ctx, w_mod, b_mod, g_norm1, w_in, b_in, g_q, g_k, g_mlstm, w_out,
           g_norm2, w_router, b_router, w1, b1, w2, b2, g_final):
    x2 = x.reshape(T_LAT, D_MODEL)
    ctx2 = ctx.reshape(T_CTX, D_MODEL)
    cos_t, sin_t = _rope_tables()
    assert w_mod.shape[0] == 1
    l = 0
    c_rows = jnp.concatenate(
        [c, c_ctx[None, :], jnp.zeros((MOD_ROWS - BATCH - 1, D_MODEL), F32)], axis=0)
    mod = _modulation(c_rows, w_mod[l], b_mod[l][None, :])
    sh1, sc1, gt1, sh2, sc2, gt2 = [mod[:, k * D_MODEL:(k + 1) * D_MODEL]
                                    for k in range(N_ADALN)]
    pad = IN_COLS_PAD - IN_COLS
    w_main = w_in[l].astype(BF16)
    w_gate = jnp.pad(w_in[l][:, C_G:], ((0, 0), (0, pad))).astype(BF16)
    b_main = b_in[l][None, :C_G]
    b_gate = jnp.pad(b_in[l][C_G:], (0, pad))[None, :]
    q, k, v, mq, mkt, mv, mo, gates = _in_projection(
        x2, ctx2, sh1, sc1, g_norm1[l][None, :], w_main, b_main, w_gate, b_gate,
        g_q[l][None, :], g_k[l][None, :], cos_t, sin_t)
    a_lat = _attention(q, k, v)
    hf, hb = _mlstm(mq, mkt, mv, gates)
    wr_p = jnp.pad(w_router[l], ((0, 0), (0, LANE - N_EXPERTS)))
    wr_hi = wr_p.astype(BF16)
    wr_lo = (wr_p - wr_hi.astype(F32)).astype(BF16)
    br_p = jnp.pad(b_router[l], (0, LANE - N_EXPERTS), constant_values=NEG_BIG)[None, :]
    x1, h2, top_i, top_r, top_w, tile_cnt = _out_projection(
        a_lat, hf, hb, mo, x2, w_out[l].astype(BF16), g_mlstm[l][None, :],
        gt1, sh2, sc2, g_norm2[l][None, :], wr_hi, wr_lo, br_p)
    dest, counts, blk_start, nblk, n_valid_blk, work_e, work_blk, work_n = _routing_tables(
        top_i[:, :TOP_K], top_r[:, :TOP_K], tile_cnt[:, 0, :N_EXPERTS])
    x_sorted = _dispatch(h2, dest, counts, blk_start, nblk, n_valid_blk)
    y_sorted = _moe(x_sorted, w1[l], b1[l], w2[l], b2[l], work_e, work_blk, work_n, n_valid_blk)
    out = _final(dest, x1, top_w, gt2, g_final[None, :], y_sorted)
    return out.reshape(BATCH, SEQ, D_MODEL)
```

```python
import jax
import jax.numpy as jnp
from jax import lax
from jax.experimental import pallas as pl
from jax.experimental.pallas import tpu as pltpu

F32 = jnp.float32
BF16 = jnp.bfloat16
HIGHEST = lax.Precision.HIGHEST

D_MODEL = 2048
BATCH = 2
SEQ = 4096
CTX_LEN = 256
GRID_W = 64
N_ADALN = 6
EPS = 1e-6
N_HEADS = 8
KV_HEADS = 2
GROUP = N_HEADS // KV_HEADS
HEAD_DIM = 128
AXIS_DIM = HEAD_DIM // 2
ROPE_THETA = 10000.0
ATT_WIDTH = N_HEADS * HEAD_DIM
ATT_SCALE = HEAD_DIM ** -0.5
M_HEADS = 4
M_DK = 128
M_DV = 256
M_CHUNK = 128
M_WIDTH = M_HEADS * M_DV
N_EXPERTS = 32
TOP_K = 4
D_FF = D_MODEL
SWIGLU_LIMIT = 7.0
SWIGLU_ALPHA = 1.702

T_LAT = BATCH * SEQ
T_CTX = BATCH * CTX_LEN
T_ALL = T_LAT + T_CTX
KV_W = KV_HEADS * HEAD_DIM
MQK_W = M_HEADS * M_DK
LANE = 128

C_Q = 0
C_K = C_Q + ATT_WIDTH
C_V = C_K + KV_W
C_MQ = C_V + KV_W
C_MK = C_MQ + MQK_W
C_MV = C_MK + MQK_W
C_MO = C_MV + M_WIDTH
C_G = C_MO + M_WIDTH
IN_COLS = C_G + 4 * M_HEADS
IN_COLS_PAD = C_G + LANE

VMEM_LIMIT = 56 * 1024 * 1024


def _params(n_axes):
    return pltpu.CompilerParams(
        dimension_semantics=("arbitrary",) * n_axes, vmem_limit_bytes=VMEM_LIMIT)


def _rms(x):
    return x * lax.rsqrt(jnp.mean(x * x, axis=-1, keepdims=True) + EPS)


MOD_ROWS = 8
MOD_TN = 1024


def _mod_kernel(c_ref, w_ref, b_ref, o_ref):
    c = c_ref[...]
    s = c / (1.0 + jnp.exp(-c))
    o_ref[...] = jnp.dot(s, w_ref[...], preferred_element_type=F32) + b_ref[...]


def _modulation(c_rows, w_mod, b_mod):
    n = w_mod.shape[1]
    return pl.pallas_call(
        _mod_kernel,
        grid=(n // MOD_TN,),
        in_specs=[
            pl.BlockSpec((MOD_ROWS, D_MODEL), lambda j: (0, 0)),
            pl.BlockSpec((D_MODEL, MOD_TN), lambda j: (0, j)),
            pl.BlockSpec((1, MOD_TN), lambda j: (0, j)),
        ],
        out_specs=pl.BlockSpec((MOD_ROWS, MOD_TN), lambda j: (0, j)),
        out_shape=jax.ShapeDtypeStruct((MOD_ROWS, n), F32),
        compiler_params=_params(1),
        name="adaln_mod",
    )(c_rows, w_mod, b_mod)


TM1 = 256
N_LAT_TILES1 = T_LAT // TM1
N_TILES1 = T_ALL // TM1
TILES_PER_SEQ1 = SEQ // TM1
PCH = 512
LOG2E = 1.4426950408889634
Q_SCALE = ATT_SCALE * LOG2E
V_AUG = 2 * HEAD_DIM


STEP_TILES1 = 2
ROWS1 = STEP_TILES1 * TM1
N_STEPS1 = N_TILES1 // STEP_TILES1
LAT_STEPS1 = N_LAT_TILES1 // STEP_TILES1
STEPS_PER_SEQ1 = TILES_PER_SEQ1 // STEP_TILES1
assert T_CTX == ROWS1


def _inproj_kernel(x0_ref, xb_ref, xn_ref, ctx_ref, sh_ref, sc_ref, g1_ref, w_ref, b_ref,
                   wg_ref, bg_ref, gq_ref, gk_ref, cos_ref, sin_ref,
                   q_ref, k_ref, v_ref, mq_ref, mkt_ref, mv_ref, mo_ref, g_ref, ha_scr, hb_scr):
    i = pl.program_id(0)

    def norm_mod(xv, tile):
        row = jnp.where(tile < N_LAT_TILES1, tile // TILES_PER_SEQ1, BATCH)
        y = (_rms(xv) * g1_ref[...]) * (1.0 + sc_ref[pl.ds(row, 1), :]) + sh_ref[pl.ds(row, 1), :]
        return y.astype(BF16)

    lane = lax.broadcasted_iota(jnp.int32, (TM1, HEAD_DIM), 1)
    even = (lane % 2) == 0

    def project(h_scr, half):
        h = h_scr[...]
        rs = slice(half * TM1, (half + 1) * TM1)
        cos_f = cos_ref[rs, :]
        sin_s = sin_ref[rs, :]

        def proj(c0, c1):
            return jnp.dot(h, w_ref[:, c0:c1], preferred_element_type=F32) + b_ref[:, c0:c1]

        def head_norm_rope(a, g):
            y = _rms(a) * g
            nxt = pltpu.roll(y, HEAD_DIM - 1, 1)
            prv = pltpu.roll(y, 1, 1)
            return y * cos_f + jnp.where(even, nxt, prv) * sin_s

        for c0 in range(C_Q, C_K, PCH):
            acc = proj(c0, c0 + PCH)
            for hh in range(PCH // HEAD_DIM):
                a = acc[:, hh * HEAD_DIM:(hh + 1) * HEAD_DIM]
                r = head_norm_rope(a, gq_ref[...]) * Q_SCALE
                q_ref[rs, c0 + hh * HEAD_DIM:c0 + (hh + 1) * HEAD_DIM] = r.astype(BF16)
        acc = proj(C_K, C_MQ)
        for hh in range(KV_HEADS):
            a = acc[:, hh * HEAD_DIM:(hh + 1) * HEAD_DIM]
            k_ref[rs, hh * HEAD_DIM:(hh + 1) * HEAD_DIM] = (
                head_norm_rope(a, gk_ref[...]).astype(BF16))
        for hh in range(KV_HEADS):
            v_ref[rs, hh * V_AUG:hh * V_AUG + HEAD_DIM] = (
                acc[:, KV_W + hh * HEAD_DIM:KV_W + (hh + 1) * HEAD_DIM].astype(BF16))
            v_ref[rs, hh * V_AUG + HEAD_DIM:(hh + 1) * V_AUG] = jnp.ones((TM1, HEAD_DIM), BF16)
        mq_ref[rs, :] = proj(C_MQ, C_MK).astype(BF16)
        mk = proj(C_MK, C_MV) * (M_DK ** -0.5)
        mkt_ref[:, rs] = mk.T.astype(BF16)
        for c0 in range(C_MV, C_MO, PCH):
            mv_ref[rs, c0 - C_MV:c0 - C_MV + PCH] = proj(c0, c0 + PCH).astype(BF16)
        for c0 in range(C_MO, C_G, PCH):
            mo_ref[rs, c0 - C_MO:c0 - C_MO + PCH] = proj(c0, c0 + PCH)
        g_ref[rs, :] = jnp.dot(h, wg_ref[...], preferred_element_type=F32) + bg_ref[...]

    @pl.when(i == 0)
    def _():
        ha_scr[...] = norm_mod(x0_ref[...], 0)

    tile_b = STEP_TILES1 * i + 1
    x_b = jnp.where(tile_b < N_LAT_TILES1, xb_ref[...], ctx_ref[TM1:ROWS1, :])
    hb_scr[...] = norm_mod(x_b, tile_b)
    project(ha_scr, 0)
    tile_n = STEP_TILES1 * i + 2
    x_n = jnp.where(tile_n < N_LAT_TILES1, xn_ref[...], ctx_ref[0:TM1, :])
    project(hb_scr, 1)
    ha_scr[...] = norm_mod(x_n, tile_n)


def _in_projection(x2, ctx2, sh1, sc1, g1, w_main, b_main, w_gate, b_gate, g_q, g_k, cos_t, sin_t):
    last_lat = N_LAT_TILES1 - 1
    rope_idx = lambda i: (jnp.where(i < LAT_STEPS1, i % STEPS_PER_SEQ1, STEPS_PER_SEQ1), 0)
    full = lambda shape: pl.BlockSpec(shape, lambda i: (0, 0))
    row_blk = lambda w: pl.BlockSpec((ROWS1, w), lambda i: (i, 0))
    out_shapes = [
        jax.ShapeDtypeStruct((T_ALL, ATT_WIDTH), BF16),
        jax.ShapeDtypeStruct((T_ALL, KV_W), BF16),
        jax.ShapeDtypeStruct((T_ALL, KV_HEADS * V_AUG), BF16),
        jax.ShapeDtypeStruct((T_ALL, MQK_W), BF16),
        jax.ShapeDtypeStruct((MQK_W, T_ALL), BF16),
        jax.ShapeDtypeStruct((T_ALL, M_WIDTH), BF16),
        jax.ShapeDtypeStruct((T_ALL, M_WIDTH), F32),
        jax.ShapeDtypeStruct((T_ALL, LANE), F32),
    ]
    out_specs = [
        row_blk(ATT_WIDTH), row_blk(KV_W), row_blk(KV_HEADS * V_AUG), row_blk(MQK_W),
        pl.BlockSpec((MQK_W, ROWS1), lambda i: (0, i)),
        row_blk(M_WIDTH), row_blk(M_WIDTH), row_blk(LANE),
    ]
    x_tile = lambda idx, **kw: pl.BlockSpec((TM1, D_MODEL), idx, **kw)
    return pl.pallas_call(
        _inproj_kernel,
        grid=(N_STEPS1,),
        in_specs=[
            x_tile(lambda i: (0, 0), pipeline_mode=pl.Buffered(1)),
            x_tile(lambda i: (jnp.minimum(STEP_TILES1 * i + 1, last_lat), 0)),
            x_tile(lambda i: (jnp.minimum(STEP_TILES1 * i + 2, last_lat), 0)),
            full((T_CTX, D_MODEL)),
            full((MOD_ROWS, D_MODEL)), full((MOD_ROWS, D_MODEL)), full((1, D_MODEL)),
            pl.BlockSpec((D_MODEL, C_G), lambda i: (0, 0), pipeline_mode=pl.Buffered(1)),
            full((1, C_G)), full((D_MODEL, LANE)), full((1, LANE)),
            full((1, HEAD_DIM)), full((1, HEAD_DIM)),
            pl.BlockSpec((ROWS1, HEAD_DIM), rope_idx),
            pl.BlockSpec((ROWS1, HEAD_DIM), rope_idx),
        ],
        out_specs=out_specs,
        out_shape=out_shapes,
        scratch_shapes=[pltpu.VMEM((TM1, D_MODEL), BF16), pltpu.VMEM((TM1, D_MODEL), BF16)],
        compiler_params=_params(1),
        name="in_proj",
    )(x2, x2, x2, ctx2, sh1, sc1, g1, w_main, b_main, w_gate, b_gate, g_q, g_k, cos_t, sin_t)


TQ = 512
CK = 512
GQ_W = GROUP * HEAD_DIM


def _attn_kernel(q_ref, kl_ref, vl_ref, kc_ref, vc_ref, o_ref):
    chunks = [(kl_ref, vl_ref, c * CK, CK) for c in range(SEQ // CK)]
    chunks.append((kc_ref, vc_ref, 0, CTX_LEN))
    for g in range(GROUP):
        qh = q_ref[:, g * HEAD_DIM:(g + 1) * HEAD_DIM]
        m = jnp.full((TQ, 1), -jnp.inf, F32)
        acc = jnp.zeros((TQ, V_AUG), F32)
        for kr, vr, st, sz in chunks:
            s = lax.dot_general(qh, kr[st:st + sz, :], (((1,), (1,)), ((), ())),
                                preferred_element_type=F32)
            m_new = jnp.maximum(m, jnp.max(s, axis=-1, keepdims=True))
            p = jnp.exp2(s - m_new)
            alpha = jnp.exp2(m - m_new)
            acc = alpha * acc + jnp.dot(p.astype(BF16), vr[st:st + sz, :],
                                        preferred_element_type=F32)
            m = m_new
        o_ref[:, g * HEAD_DIM:(g + 1) * HEAD_DIM] = (
            acc[:, :HEAD_DIM] / acc[:, HEAD_DIM:]).astype(BF16)


def _attention(q, k, v):
    nq = SEQ // TQ
    ctx_blk0 = T_LAT // CTX_LEN
    lat_k = pl.BlockSpec((SEQ, HEAD_DIM), lambda b, h, i: (b, h))
    lat_v = pl.BlockSpec((SEQ, V_AUG), lambda b, h, i: (b, h))
    ctx_k = pl.BlockSpec((CTX_LEN, HEAD_DIM), lambda b, h, i: (ctx_blk0 + b, h))
    ctx_v = pl.BlockSpec((CTX_LEN, V_AUG), lambda b, h, i: (ctx_blk0 + b, h))
    q_blk = pl.BlockSpec((TQ, GQ_W), lambda b, h, i: (b * nq + i, h))
    return pl.pallas_call(
        _attn_kernel,
        grid=(BATCH, KV_HEADS, nq),
        in_specs=[q_blk, lat_k, lat_v, ctx_k, ctx_v],
        out_specs=q_blk,
        out_shape=jax.ShapeDtypeStruct((T_LAT, ATT_WIDTH), BF16),
        compiler_params=_params(3),
        name="gqa_attention",
    )(q, k, v, k, v)


N_CHUNKS = SEQ // M_CHUNK
CTX_CHUNKS = CTX_LEN // M_CHUNK
N_CHAINS = 2 * M_HEADS


def _log_sigmoid(x):
    return jnp.minimum(x, 0.0) - jnp.log(1.0 + jnp.exp(-jnp.abs(x)))


def _mlstm_kernel(qf_ref, qb_ref, qc_ref, kf_ref, kb_ref, kc_ref, vf_ref, vb_ref, vc_ref,
                  gf_ref, gb_ref, gc_ref, hf_ref, hb_ref, *state):
    c_scr = state[0:N_CHAINS]
    n_scr = state[N_CHAINS:2 * N_CHAINS]
    m_scr = state[2 * N_CHAINS:3 * N_CHAINS]
    j = pl.program_id(1)
    ri = lax.broadcasted_iota(jnp.int32, (M_CHUNK, M_CHUNK), 0)
    ci = lax.broadcasted_iota(jnp.int32, (M_CHUNK, M_CHUNK), 1)
    lower = ci <= ri
    upper = ci >= ri
    tri_l = lower.astype(F32)
    tri_u = upper.astype(F32)

    def gate_tables(g, rev):
        gt = g.T
        mcol, mrow = (tri_u, tri_l) if rev else (tri_l, tri_u)
        bcol = jnp.dot(mcol, _log_sigmoid(g), precision=HIGHEST, preferred_element_type=F32)
        brow = jnp.dot(_log_sigmoid(gt), mrow, precision=HIGHEST, preferred_element_type=F32)
        return gt, bcol, brow

    def chain_step(chain, q, kt, v, tabs, rev, h_out):
        gt, bcol, brow = tabs
        icol = chain
        fcol = N_CHAINS + chain
        last = 0 if rev else M_CHUNK - 1
        b_col = bcol[:, fcol:fcol + 1]
        b_row = brow[fcol:fcol + 1, :]
        i_row = gt[icol:icol + 1, :]
        tot = brow[fcol:fcol + 1, last:last + 1]
        m_old = m_scr[chain][:1, :1]
        c_old = c_scr[chain][...]
        n_old = n_scr[chain][...]
        v_ones = jnp.concatenate([v, jnp.ones((M_CHUNK, LANE), BF16)], axis=1)
        c_row = i_row - b_row
        if h_out is not None:
            mask = upper if rev else lower
            mm = jnp.maximum(m_old, jnp.max(jnp.where(mask, c_row, -jnp.inf), axis=-1, keepdims=True))
            w_intra = jnp.exp(jnp.where(mask, c_row - mm, -jnp.inf))
            w_inter = jnp.exp(m_old - mm)
            s = jnp.dot(q, kt, preferred_element_type=F32) * w_intra
            qc = jnp.dot(q, c_old.astype(BF16), preferred_element_type=F32)
            qn = jnp.dot(q, n_old.astype(BF16), preferred_element_type=F32)[:, :1]
            sv = jnp.dot(s.astype(BF16), v_ones, preferred_element_type=F32)
            num = sv[:, :M_DV] + w_inter * qc
            den = sv[:, M_DV:M_DV + 1] + w_inter * qn
            h_out(num / jnp.maximum(jnp.abs(den), jnp.exp(-(b_col + mm))))
        g_row = tot + c_row
        m_new = jnp.maximum(tot + m_old, jnp.max(g_row, axis=-1, keepdims=True))
        w_state = jnp.exp(g_row - m_new)
        decay = jnp.exp(tot + m_old - m_new)
        kw = (kt.astype(F32) * w_state).astype(BF16)
        kv = jnp.dot(kw, v_ones, preferred_element_type=F32)
        c_scr[chain][...] = decay * c_old + kv[:, :M_DV]
        n_scr[chain][...] = decay * n_old + kv[:, M_DV:]
        m_scr[chain][...] = jnp.broadcast_to(m_new, (8, LANE))

    def run_chunk(q_ref, kt_ref, v_ref, g_ref, r0, rev, h_ref):
        tabs = gate_tables(g_ref[r0:r0 + M_CHUNK, :], rev)
        for hd in range(M_HEADS):
            chain = (M_HEADS if rev else 0) + hd
            q = q_ref[r0:r0 + M_CHUNK, hd * M_DK:(hd + 1) * M_DK]
            kt = kt_ref[hd * M_DK:(hd + 1) * M_DK, r0:r0 + M_CHUNK]
            v = v_ref[r0:r0 + M_CHUNK, hd * M_DV:(hd + 1) * M_DV]
            if h_ref is None:
                h_out = None
            else:
                def h_out(hv, hd=hd):
                    h_ref[:, hd * M_DV:(hd + 1) * M_DV] = hv
            chain_step(chain, q, kt, v, tabs, rev, h_out)

    @pl.when(j == 0)
    def _():
        for ref in state:
            ref[...] = jnp.zeros_like(ref)
        for cc in range(CTX_CHUNKS):
            run_chunk(qc_ref, kc_ref, vc_ref, gc_ref, cc * M_CHUNK, False, None)
            run_chunk(qc_ref, kc_ref, vc_ref, gc_ref, (CTX_CHUNKS - 1 - cc) * M_CHUNK, True, None)

    run_chunk(qf_ref, kf_ref, vf_ref, gf_ref, 0, False, hf_ref)
    run_chunk(qb_ref, kb_ref, vb_ref, gb_ref, 0, True, hb_ref)


def _mlstm(mq, mkt, mv, gates):
    ctx_blk0 = T_LAT // CTX_LEN
    fwd = lambda b, j: b * N_CHUNKS + j
    bwd = lambda b, j: b * N_CHUNKS + N_CHUNKS - 1 - j
    ctx = lambda b, j: ctx_blk0 + b

    def rows(width, tile, idx):
        return pl.BlockSpec((tile, width), lambda b, j: (idx(b, j), 0))

    def cols(tile, idx):
        return pl.BlockSpec((MQK_W, tile), lambda b, j: (0, idx(b, j)))

    h_shape = jax.ShapeDtypeStruct((T_LAT, M_WIDTH), F32)
    return pl.pallas_call(
        _mlstm_kernel,
        grid=(BATCH, N_CHUNKS),
        in_specs=[
            rows(MQK_W, M_CHUNK, fwd), rows(MQK_W, M_CHUNK, bwd), rows(MQK_W, CTX_LEN, ctx),
            cols(M_CHUNK, fwd), cols(M_CHUNK, bwd), cols(CTX_LEN, ctx),
            rows(M_WIDTH, M_CHUNK, fwd), rows(M_WIDTH, M_CHUNK, bwd), rows(M_WIDTH, CTX_LEN, ctx),
            rows(LANE, M_CHUNK, fwd), rows(LANE, M_CHUNK, bwd), rows(LANE, CTX_LEN, ctx),
        ],
        out_specs=[rows(M_WIDTH, M_CHUNK, fwd), rows(M_WIDTH, M_CHUNK, bwd)],
        out_shape=[h_shape, h_shape],
        scratch_shapes=([pltpu.VMEM((M_DK, M_DV), F32)] * N_CHAINS
                        + [pltpu.VMEM((M_DK, LANE), F32)] * N_CHAINS
                        + [pltpu.VMEM((8, LANE), F32)] * N_CHAINS),
        compiler_params=_params(2),
        name="mlstm_scan",
    )(mq, mq, mq, mkt, mkt, mkt, mv, mv, mv, gates, gates, gates)


TM4 = 256
TILES_PER_SEQ4 = SEQ // TM4
TILES_PER_STEP4 = 2
NEG_BIG = -1e30


def _outproj_kernel(a_ref, hf_ref, hb_ref, mo_ref, x_ref, w_ref, gm_ref, gt1_ref, sh2_ref,
                    sc2_ref, g2_ref, wr_ref, br_ref,
                    x1_ref, h2_ref, ti_ref, tr_ref, tw_ref, tc_ref):
    for hv in range(TILES_PER_STEP4):
        _outproj_tile(hv, a_ref, hf_ref, hb_ref, mo_ref, x_ref, w_ref, gm_ref, gt1_ref, sh2_ref,
                      sc2_ref, g2_ref, wr_ref, br_ref,
                      x1_ref, h2_ref, ti_ref, tr_ref, tw_ref, tc_ref)


def _outproj_tile(hv, a_ref, hf_ref, hb_ref, mo_ref, x_ref, w_ref, gm_ref, gt1_ref, sh2_ref,
                  sc2_ref, g2_ref, wr_ref, br_ref,
                  x1_ref, h2_ref, ti_ref, tr_ref, tw_ref, tc_ref):
    i = pl.program_id(0)
    row = (i * TILES_PER_STEP4 + hv) // TILES_PER_SEQ4
    rs = slice(hv * TM4, (hv + 1) * TM4)
    acc = jnp.dot(a_ref[rs, :], w_ref[0:ATT_WIDTH, :], preferred_element_type=F32)
    for hd in range(M_HEADS):
        sl = slice(hd * M_DV, (hd + 1) * M_DV)
        hn = _rms(hf_ref[rs, sl] + hb_ref[rs, sl]) * gm_ref[:, sl]
        mo = mo_ref[rs, sl]
        ym = hn / (1.0 + jnp.exp(-mo))
        acc += jnp.dot(ym.astype(BF16), w_ref[ATT_WIDTH + hd * M_DV:ATT_WIDTH + (hd + 1) * M_DV, :],
                       preferred_element_type=F32)
    x1 = x_ref[rs, :] + gt1_ref[pl.ds(row, 1), :] * acc
    x1_ref[rs, :] = x1
    h2 = (_rms(x1) * g2_ref[...]) * (1.0 + sc2_ref[pl.ds(row, 1), :]) + sh2_ref[pl.ds(row, 1), :]
    h2_ref[rs, :] = h2
    h_hi = h2.astype(BF16)
    h_lo = (h2 - h_hi.astype(F32)).astype(BF16)
    hi_prod = jnp.dot(h_hi, wr_ref[...], preferred_element_type=F32)
    lo_prod = jnp.dot(h_lo, wr_ref[:, :LANE], preferred_element_type=F32)
    logits = hi_prod[:, :LANE] + lo_prod + hi_prod[:, LANE:] + br_ref[...]
    lane = lax.broadcasted_iota(jnp.int32, (TM4, LANE), 1)
    vals, idxs = [], []
    for _ in range(TOP_K):
        mx = jnp.max(logits, axis=-1, keepdims=True)
        ik = jnp.min(jnp.where(logits == mx, lane, LANE), axis=-1, keepdims=True)
        vals.append(mx)
        idxs.append(ik)
        logits = jnp.where(lane == ik, -jnp.inf, logits)
    es = [jnp.exp(vk - vals[0]) for vk in vals]
    tot = es[0] + es[1] + es[2] + es[3]
    chosen = jnp.zeros((TM4, LANE), F32)
    for kk in range(TOP_K):
        chosen = jnp.where(lane == idxs[kk], 1.0, chosen)
    ri = lax.broadcasted_iota(jnp.int32, (TM4, TM4), 0)
    ci = lax.broadcasted_iota(jnp.int32, (TM4, TM4), 1)
    before = jnp.where(ci < ri, 1.0, 0.0).astype(BF16)
    earlier = jnp.dot(before, chosen.astype(BF16), preferred_element_type=F32)
    ti = jnp.zeros((TM4, LANE), jnp.int32)
    tr = jnp.zeros((TM4, LANE), jnp.int32)
    tw = jnp.zeros((TM4, LANE), F32)
    for kk in range(TOP_K):
        rk = jnp.sum(jnp.where(lane == idxs[kk], earlier, 0.0), axis=-1, keepdims=True)
        ti = jnp.where(lane == kk, idxs[kk], ti)
        tr = jnp.where(lane == kk, rk.astype(jnp.int32), tr)
        tw = jnp.where(lane == kk, es[kk] / tot, tw)
    ti_ref[rs, :] = ti
    tr_ref[rs, :] = tr
    tw_ref[rs, :] = tw
    tc_ref[hv] = jnp.sum(chosen, axis=0, keepdims=True).astype(jnp.int32)


def _out_projection(a_lat, hf, hb, mo, x2, w_out_b, g_mlstm, gt1, sh2, sc2, g2, wr_both, br_p):
    full = lambda shape: pl.BlockSpec(shape, lambda i: (0, 0))
    rows = TILES_PER_STEP4 * TM4
    row_blk = lambda w: pl.BlockSpec((rows, w), lambda i: (i, 0))
    return pl.pallas_call(
        _outproj_kernel,
        grid=(T_LAT // rows,),
        in_specs=[
            row_blk(ATT_WIDTH), row_blk(M_WIDTH), row_blk(M_WIDTH), row_blk(M_WIDTH),
            row_blk(D_MODEL), full((ATT_WIDTH + M_WIDTH, D_MODEL)), full((1, M_WIDTH)),
            full((MOD_ROWS, D_MODEL)), full((MOD_ROWS, D_MODEL)), full((MOD_ROWS, D_MODEL)),
            full((1, D_MODEL)), full((D_MODEL, 2 * LANE)), full((1, LANE)),
        ],
        out_specs=[row_blk(D_MODEL), row_blk(D_MODEL), row_blk(LANE), row_blk(LANE), row_blk(LANE),
                   pl.BlockSpec((TILES_PER_STEP4, 1, LANE), lambda i: (i, 0, 0))],
        out_shape=[
            jax.ShapeDtypeStruct((T_LAT, D_MODEL), F32),
            jax.ShapeDtypeStruct((T_LAT, D_MODEL), F32),
            jax.ShapeDtypeStruct((T_LAT, LANE), jnp.int32),
            jax.ShapeDtypeStruct((T_LAT, LANE), jnp.int32),
            jax.ShapeDtypeStruct((T_LAT, LANE), F32),
            jax.ShapeDtypeStruct((T_LAT // TM4, 1, LANE), jnp.int32),
        ],
        compiler_params=_params(1),
        name="out_proj_router",
    )(a_lat, hf, hb, mo, x2, w_out_b, g_mlstm, gt1, sh2, sc2, g2, wr_both, br_p)


MB = 128
N_ASSIGN = T_LAT * TOP_K
N_BLOCKS = N_ASSIGN // MB + N_EXPERTS
P_ROWS = N_BLOCKS * MB
N_PADS = P_ROWS - N_ASSIGN
SB = 10
W_MAX = N_EXPERTS + (N_BLOCKS - N_EXPERTS) // SB
XROWS = SB * MB


def _routing_tables(top_idx, tile_rank, tile_counts):
    n_tiles = tile_counts.shape[0]
    counts = jnp.sum(tile_counts, axis=0)
    tile_base = jnp.cumsum(tile_counts, axis=0) - tile_counts
    nblk = (counts + MB - 1) // MB
    blk_start = jnp.cumsum(nblk) - nblk
    n_valid_blk = jnp.sum(nblk)
    offs = blk_start[None, :] * MB + tile_base
    e4 = top_idx.reshape(n_tiles, -1, TOP_K, 1)
    hit = e4 == jnp.arange(N_EXPERTS, dtype=jnp.int32)
    dest = jnp.sum(jnp.where(hit, offs[:, None, None, :], 0), axis=-1)
    dest = (dest.reshape(T_LAT, TOP_K) + tile_rank).reshape(N_ASSIGN).astype(jnp.int32)
    nwork = (nblk + SB - 1) // SB
    wend = jnp.cumsum(nwork)
    wstart = wend - nwork
    n_items = wend[-1]
    wid = jnp.arange(W_MAX, dtype=jnp.int32)
    valid = wid < n_items
    we = jnp.minimum(jnp.sum((wend[None, :] <= wid[:, None]).astype(jnp.int32), axis=1),
                     N_EXPERTS - 1)
    local = wid - wstart[we]
    work_blk = jnp.where(valid, blk_start[we] + local * SB, 0)
    work_n = jnp.where(valid, jnp.clip(nblk[we] - local * SB, 0, SB), 0)
    last_e = we[jnp.maximum(n_items - 1, 0)]
    work_e = jnp.where(valid, we, last_e)
    i32 = lambda a: a.astype(jnp.int32)
    return (dest, i32(counts), i32(blk_start), i32(nblk), i32(n_valid_blk)[None],
            i32(work_e), i32(work_blk), i32(work_n))


TMD = 256
ROW_GROUP = 8


def _dispatch_kernel(dest_ref, cnt_ref, bs_ref, nb_ref, nvb_ref, h2_ref, x_hbm, zrow, sem, psem):
    i = pl.program_id(0)

    @pl.when(i == 0)
    def _():
        zrow[...] = jnp.zeros_like(zrow)

        def zero_row(r, carry):
            pltpu.make_async_copy(zrow.at[pl.ds(0, 1)], x_hbm.at[pl.ds(r, 1)], psem).start()
            return carry

        def expert(e, carry):
            first = bs_ref[e] * MB
            lax.fori_loop(first + cnt_ref[e], first + nb_ref[e] * MB, zero_row, 0)
            return carry
        lax.fori_loop(0, N_EXPERTS, expert, 0)
        lax.fori_loop(nvb_ref[0] * MB, P_ROWS, zero_row, 0)

    base = i * (TMD * TOP_K)

    def issue(tg, carry):
        t0 = pl.multiple_of(tg * ROW_GROUP, ROW_GROUP)
        for j in range(ROW_GROUP):
            for kk in range(TOP_K):
                d = dest_ref[base + (t0 + j) * TOP_K + kk]
                pltpu.make_async_copy(h2_ref.at[pl.ds(t0 + j, 1)], x_hbm.at[pl.ds(d, 1)],
                                      sem).start(priority=kk % 2)
        return carry
    lax.fori_loop(0, TMD // ROW_GROUP, issue, 0)
    for kk in range(TOP_K):
        pltpu.make_async_copy(h2_ref, x_hbm.at[pl.ds(0, TMD)], sem).wait()

    @pl.when(i == 0)
    def _():
        for _ in range(N_PADS // TMD):
            pltpu.make_async_copy(h2_ref, x_hbm.at[pl.ds(0, TMD)], psem).wait()


def _dispatch(h2, dest, counts, blk_start, nblk, n_valid_blk):
    grid_spec = pltpu.PrefetchScalarGridSpec(
        num_scalar_prefetch=5,
        grid=(T_LAT // TMD,),
        in_specs=[pl.BlockSpec((TMD, D_MODEL), lambda i, *_: (i, 0))],
        out_specs=pl.BlockSpec(memory_space=pl.ANY),
        scratch_shapes=[
            pltpu.VMEM((8, D_MODEL), F32),
            pltpu.SemaphoreType.DMA(()),
            pltpu.SemaphoreType.DMA(()),
        ],
    )
    return pl.pallas_call(
        _dispatch_kernel,
        grid_spec=grid_spec,
        out_shape=jax.ShapeDtypeStruct((P_ROWS, D_MODEL), F32),
        compiler_params=_params(1),
        name="moe_dispatch",
    )(dest, counts, blk_start, nblk, n_valid_blk, h2)


TF = 512
NF = D_FF // TF
BIG_BLKS = 4
TF2 = TF // 2
W_PRIORITY = 1
W_AHEAD = 2
W_SLOTS = W_AHEAD + 1
MOE_VMEM_LIMIT = 62 * 1024 * 1024


def _moe_kernel(we_ref, wb_ref, wn_ref, nvb_ref,
                x_hbm, w1_hbm, w2_hbm, w1g_ref, w2a_ref, b1g_ref, b1u_ref, b2_ref,
                y_hbm, xbuf, xb16, acc, w1u_buf, w2b_buf, xsem, ysem, wsem):
    w = pl.program_id(0)
    f = pl.program_id(1)
    nsub = wn_ref[w]
    r0 = wb_ref[w] * MB

    step = w * NF + f
    wslot = step % W_SLOTS

    def w_copies(item, ff, slot_):
        e = we_ref[item]
        up = pltpu.make_async_copy(w1_hbm.at[e, :, pl.ds(D_FF + ff * TF, TF)],
                                   w1u_buf.at[slot_], wsem.at[slot_])
        dn = pltpu.make_async_copy(w2_hbm.at[e, pl.ds(ff * TF + TF2, TF2), :],
                                   w2b_buf.at[slot_], wsem.at[slot_])
        return up, dn

    def start_w(item, ff, slot_):
        for cp in w_copies(item, ff, slot_):
            cp.start(priority=W_PRIORITY)

    @pl.when(jnp.logical_and(step == 0, nsub > 0))
    def _():
        for s in range(W_AHEAD):
            start_w(0, s, s)

    last_f = f == NF - 1
    nxt_item = w + (f + W_AHEAD) // NF
    nxt_f = (f + W_AHEAD) % NF

    @pl.when(jnp.logical_and(nxt_item < W_MAX, wn_ref[jnp.minimum(nxt_item, W_MAX - 1)] > 0))
    def _():
        start_w(nxt_item, nxt_f, (step + W_AHEAD) % W_SLOTS)

    @pl.when(nsub > 0)
    def _():
        for cp in w_copies(w, f, wslot):
            cp.wait()

    def x_copy(item, s):
        src = x_hbm.at[pl.ds(wb_ref[item] * MB + s * MB, MB)]
        return pltpu.make_async_copy(src, xbuf.at[pl.ds(s * MB, MB)], xsem)

    def start_x(item):
        def body(s, carry):
            x_copy(item, s).start()
            return carry
        lax.fori_loop(0, wn_ref[item], body, 0)

    @pl.when(jnp.logical_and(w == 0, f == 0))
    def _():
        start_x(0)

    @pl.when(f == 0)
    def _():
        def wait(s, carry):
            x_copy(w, s).wait()
            return carry
        lax.fori_loop(0, nsub, wait, 0)

        def body(s, carry):
            rs = pl.ds(pl.multiple_of(s * MB, MB), MB)
            xb16[rs, :] = xbuf[rs, :].astype(BF16)
            acc[rs, :] = jnp.broadcast_to(b2_ref[...], (MB, D_MODEL))
            return carry
        lax.fori_loop(0, nsub, body, 0)

    @pl.when(jnp.logical_and(f == 1, w + 1 < W_MAX))
    def _():
        start_x(w + 1)

    def y_copy(s):
        rs = pl.ds(pl.multiple_of(s * MB, MB), MB)
        return pltpu.make_async_copy(acc.at[rs], y_hbm.at[pl.ds(r0 + s * MB, MB)], ysem)

    def chunk(row, m):
        rs = pl.ds(row, m)
        xb = xb16[rs, :]
        gate = jnp.dot(xb, w1g_ref[...], preferred_element_type=F32) + b1g_ref[...]
        up = jnp.dot(xb, w1u_buf[wslot], preferred_element_type=F32) + b1u_ref[...]
        gate = jnp.minimum(gate, SWIGLU_LIMIT)
        up = jnp.clip(up, -SWIGLU_LIMIT, SWIGLU_LIMIT)
        glu = gate / (1.0 + jnp.exp(-SWIGLU_ALPHA * gate))
        act = (up + 1.0) * glu
        acc[rs, :] += (jnp.dot(act[:, :TF2], w2a_ref[...], preferred_element_type=F32)
                       + jnp.dot(act[:, TF2:], w2b_buf[wslot], preferred_element_type=F32))

    def write_back(first_blk, n_blk):
        @pl.when(last_f)
        def _():
            for s in range(n_blk):
                y_copy(first_blk + s).start()

    nbig = nsub // BIG_BLKS

    def big(c, carry):
        chunk(pl.multiple_of(c * (BIG_BLKS * MB), BIG_BLKS * MB), BIG_BLKS * MB)
        write_back(c * BIG_BLKS, BIG_BLKS)
        return carry
    lax.fori_loop(0, nbig, big, 0)

    done = nbig * BIG_BLKS
    size = BIG_BLKS // 2
    while size >= 1:
        has = ((nsub - done) // size) % 2 == 1
        first = done + ((nsub - done) // (2 * size)) * (2 * size)

        @pl.when(has)
        def _(first=first, size=size):
            chunk(pl.multiple_of(first * MB, size * MB), size * MB)
            write_back(first, size)
        size //= 2

    @pl.when(last_f)
    def _():
        def wait(s, carry):
            y_copy(s).wait()
            return carry
        lax.fori_loop(0, nsub, wait, 0)

    @pl.when(jnp.logical_and(w == W_MAX - 1, f == NF - 1))
    def _():
        acc[0:MB, :] = jnp.zeros((MB, D_MODEL), F32)

        def tail_copy(bk):
            return pltpu.make_async_copy(acc.at[0:MB], y_hbm.at[pl.ds(bk * MB, MB)], ysem)

        def start(bk, carry):
            tail_copy(bk).start()
            return carry
        lax.fori_loop(nvb_ref[0], N_BLOCKS, start, 0)

        def wait(bk, carry):
            tail_copy(bk).wait()
            return carry
        lax.fori_loop(nvb_ref[0], N_BLOCKS, wait, 0)


def _moe(x_sorted, w1, b1, w2, b2, work_e, work_blk, work_n, n_valid_blk):
    b1r = b1.reshape(N_EXPERTS, 1, 2 * D_FF)
    b2r = b2.reshape(N_EXPERTS, 1, D_MODEL)
    grid_spec = pltpu.PrefetchScalarGridSpec(
        num_scalar_prefetch=4,
        grid=(W_MAX, NF),
        in_specs=[
            pl.BlockSpec(memory_space=pl.ANY),
            pl.BlockSpec(memory_space=pl.ANY),
            pl.BlockSpec(memory_space=pl.ANY),
            pl.BlockSpec((None, D_MODEL, TF), lambda w, f, we, *_: (we[w], 0, f)),
            pl.BlockSpec((None, TF2, D_MODEL), lambda w, f, we, *_: (we[w], 2 * f, 0)),
            pl.BlockSpec((None, 1, TF), lambda w, f, we, *_: (we[w], 0, f)),
            pl.BlockSpec((None, 1, TF), lambda w, f, we, *_: (we[w], 0, NF + f)),
            pl.BlockSpec((None, 1, D_MODEL), lambda w, f, we, *_: (we[w], 0, 0)),
        ],
        out_specs=pl.BlockSpec(memory_space=pl.ANY),
        scratch_shapes=[
            pltpu.VMEM((XROWS, D_MODEL), F32),
            pltpu.VMEM((XROWS, D_MODEL), BF16),
            pltpu.VMEM((XROWS, D_MODEL), F32),
            pltpu.VMEM((W_SLOTS, D_MODEL, TF), F32),
            pltpu.VMEM((W_SLOTS, TF2, D_MODEL), F32),
            pltpu.SemaphoreType.DMA(()),
            pltpu.SemaphoreType.DMA(()),
            pltpu.SemaphoreType.DMA((W_SLOTS,)),
        ],
    )
    return pl.pallas_call(
        _moe_kernel,
        grid_spec=grid_spec,
        out_shape=jax.ShapeDtypeStruct((P_ROWS, D_MODEL), F32),
        compiler_params=pltpu.CompilerParams(
            dimension_semantics=("arbitrary", "arbitrary"), vmem_limit_bytes=MOE_VMEM_LIMIT),
        name="moe_experts",
    )(work_e, work_blk, work_n, n_valid_blk, x_sorted, w1, w2, w1, w2, b1r, b1r, b2r)


TM7 = 256
N_TILES7 = T_LAT // TM7
TILES_PER_SEQ7 = SEQ // TM7


def _final_kernel(pos_ref, x1_ref, tw_ref, gt2_ref, gf_ref, y_hbm, o_ref, ybuf, sem):
    i = pl.program_id(0)
    slot = i % 2

    def start_gather(tile, slot_):
        base = tile * (TM7 * TOP_K)

        def body(tg, carry):
            t0 = pl.multiple_of(tg * ROW_GROUP, ROW_GROUP)
            for j in range(ROW_GROUP):
                for kk in range(TOP_K):
                    p = pos_ref[base + (t0 + j) * TOP_K + kk]
                    pltpu.make_async_copy(y_hbm.at[pl.ds(p, 1)],
                                          ybuf.at[slot_, kk, pl.ds(t0 + j, 1)],
                                          sem.at[slot_]).start(priority=kk % 2)
            return carry
        lax.fori_loop(0, TM7 // ROW_GROUP, body, 0)

    @pl.when(i == 0)
    def _():
        start_gather(0, 0)

    @pl.when(i + 1 < N_TILES7)
    def _():
        start_gather(i + 1, 1 - slot)

    for kk in range(TOP_K):
        pltpu.make_async_copy(y_hbm.at[pl.ds(0, TM7)], ybuf.at[slot, kk], sem.at[slot]).wait()

    row = i // TILES_PER_SEQ7
    tw = tw_ref[...]
    mix = jnp.zeros((TM7, D_MODEL), F32)
    for kk in range(TOP_K):
        mix += ybuf[slot, kk] * tw[:, kk:kk + 1]
    xo = x1_ref[...] + gt2_ref[pl.ds(row, 1), :] * mix
    o_ref[...] = _rms(xo) * gf_ref[...]


def _final(dest, x1, top_w, gt2, g_final, y_sorted):
    grid_spec = pltpu.PrefetchScalarGridSpec(
        num_scalar_prefetch=1,
        grid=(N_TILES7,),
        in_specs=[
            pl.BlockSpec((TM7, D_MODEL), lambda i, *_: (i, 0)),
            pl.BlockSpec((TM7, LANE), lambda i, *_: (i, 0)),
            pl.BlockSpec((MOD_ROWS, D_MODEL), lambda i, *_: (0, 0)),
            pl.BlockSpec((1, D_MODEL), lambda i, *_: (0, 0)),
            pl.BlockSpec(memory_space=pl.ANY),
        ],
        out_specs=pl.BlockSpec((TM7, D_MODEL), lambda i, *_: (i, 0)),
        scratch_shapes=[
            pltpu.VMEM((2, TOP_K, TM7, D_MODEL), F32),
            pltpu.SemaphoreType.DMA((2,)),
        ],
    )
    return pl.pallas_call(
        _final_kernel,
        grid_spec=grid_spec,
        out_shape=jax.ShapeDtypeStruct((T_LAT, D_MODEL), F32),
        compiler_params=_params(1),
        name="combine_final_norm",
    )(dest, x1, top_w, gt2, g_final, y_sorted)


def _rope_tables():
    rows = SEQ // GRID_W
    row = jnp.repeat(jnp.arange(rows, dtype=F32), GRID_W)
    col = jnp.tile(jnp.arange(GRID_W, dtype=F32), rows)
    inv = ROPE_THETA ** (-jnp.arange(0, AXIS_DIM, 2, dtype=F32) / AXIS_DIM)
    ang = jnp.concatenate([row[:, None] * inv, col[:, None] * inv], axis=-1)
    cos = jnp.repeat(jnp.cos(ang), 2, axis=-1)
    sin = jnp.repeat(jnp.sin(ang), 2, axis=-1)
    sign = jnp.tile(jnp.array([-1.0, 1.0], F32), HEAD_DIM // 2)
    cos_t = jnp.concatenate([cos, jnp.ones((ROWS1, HEAD_DIM), F32)], axis=0)
    sin_t = jnp.concatenate([sin * sign, jnp.zeros((ROWS1, HEAD_DIM), F32)], axis=0)
    return cos_t, sin_t


def kernel(x, c, ctx, c_ctx, w_mod, b_mod, g_norm1, w_in, b_in, g_q, g_k, g_mlstm, w_out,
           g_norm2, w_router, b_router, w1, b1, w2, b2, g_final):
    x2 = x.reshape(T_LAT, D_MODEL)
    ctx2 = ctx.reshape(T_CTX, D_MODEL)
    cos_t, sin_t = _rope_tables()
    assert w_mod.shape[0] == 1
    l = 0
    c_rows = jnp.concatenate(
        [c, c_ctx[None, :], jnp.zeros((MOD_ROWS - BATCH - 1, D_MODEL), F32)], axis=0)
    mod = _modulation(c_rows, w_mod[l], b_mod[l][None, :])
    sh1, sc1, gt1, sh2, sc2, gt2 = [mod[:, k * D_MODEL:(k + 1) * D_MODEL]
                                    for k in range(N_ADALN)]
    pad = IN_COLS_PAD - IN_COLS
    w_main = w_in[l].astype(BF16)
    w_gate = jnp.pad(w_in[l][:, C_G:], ((0, 0), (0, pad))).astype(BF16)
    b_main = b_in[l][None, :C_G]
    b_gate = jnp.pad(b_in[l][C_G:], (0, pad))[None, :]
    q, k, v, mq, mkt, mv, mo, gates = _in_projection(
        x2, ctx2, sh1, sc1, g_norm1[l][None, :], w_main, b_main, w_gate, b_gate,
        g_q[l][None, :], g_k[l][None, :], cos_t, sin_t)
    a_lat = _attention(q, k, v)
    hf, hb = _mlstm(mq, mkt, mv, gates)
    wr_p = jnp.pad(w_router[l], ((0, 0), (0, LANE - N_EXPERTS)))
    wr_hi = wr_p.astype(BF16)
    wr_lo = (wr_p - wr_hi.astype(F32)).astype(BF16)
    br_p = jnp.pad(b_router[l], (0, LANE - N_EXPERTS), constant_values=NEG_BIG)[None, :]
    x1, h2, top_i, top_r, top_w, tile_cnt = _out_projection(
        a_lat, hf, hb, mo, x2, w_out[l].astype(BF16), g_mlstm[l][None, :],
        gt1, sh2, sc2, g_norm2[l][None, :], jnp.concatenate([wr_hi, wr_lo], axis=1), br_p)
    dest, counts, blk_start, nblk, n_valid_blk, work_e, work_blk, work_n = _routing_tables(
        top_i[:, :TOP_K], top_r[:, :TOP_K], tile_cnt[:, 0, :N_EXPERTS])
    x_sorted = _dispatch(h2, dest, counts, blk_start, nblk, n_valid_blk)
    y_sorted = _moe(x_sorted, w1[l], b1[l], w2[l], b2[l], work_e, work_blk, work_n, n_valid_blk)
    out = _final(dest, x1, top_w, gt2, g_final[None, :], y_sorted)
    return out.reshape(BATCH, SEQ, D_MODEL)
```

```python
import jax
import jax.numpy as jnp
from jax import lax
from jax.experimental import pallas as pl
from jax.experimental.pallas import tpu as pltpu

F32 = jnp.float32
BF16 = jnp.bfloat16

D_MODEL = 2048
BATCH = 2
SEQ = 4096
CTX_LEN = 256
GRID_W = 64
N_ADALN = 6
EPS = 1e-6
N_HEADS = 8
KV_HEADS = 2
GROUP = N_HEADS // KV_HEADS
HEAD_DIM = 128
AXIS_DIM = HEAD_DIM // 2
ROPE_THETA = 10000.0
ATT_WIDTH = N_HEADS * HEAD_DIM
ATT_SCALE = HEAD_DIM ** -0.5
M_HEADS = 4
M_DK = 128
M_DV = 256
M_CHUNK = 128
M_WIDTH = M_HEADS * M_DV
N_EXPERTS = 32
TOP_K = 4
D_FF = D_MODEL
SWIGLU_LIMIT = 7.0
SWIGLU_ALPHA = 1.702

T_LAT = BATCH * SEQ
T_CTX = BATCH * CTX_LEN
T_ALL = T_LAT + T_CTX
KV_W = KV_HEADS * HEAD_DIM
MQK_W = M_HEADS * M_DK
LANE = 128

C_Q = 0
C_K = C_Q + ATT_WIDTH
C_V = C_K + KV_W
C_MQ = C_V + KV_W
C_MK = C_MQ + MQK_W
C_MV = C_MK + MQK_W
C_MO = C_MV + M_WIDTH
C_G = C_MO + M_WIDTH
IN_COLS = C_G + 4 * M_HEADS
IN_COLS_PAD = C_G + LANE

VMEM_LIMIT = 56 * 1024 * 1024


def _params(n_axes):
    return pltpu.CompilerParams(
        dimension_semantics=("arbitrary",) * n_axes, vmem_limit_bytes=VMEM_LIMIT)


def _rms(x):
    return x * lax.rsqrt(jnp.mean(x * x, axis=-1, keepdims=True) + EPS)


MOD_ROWS = 8
MOD_TN = 1024


def _mod_kernel(c_ref, w_ref, b_ref, o_ref):
    c = c_ref[...]
    s = c / (1.0 + jnp.exp(-c))
    o_ref[...] = jnp.dot(s, w_ref[...], preferred_element_type=F32) + b_ref[...]


def _modulation(c_rows, w_mod, b_mod):
    n = w_mod.shape[1]
    return pl.pallas_call(
        _mod_kernel,
        grid=(n // MOD_TN,),
        in_specs=[
            pl.BlockSpec((MOD_ROWS, D_MODEL), lambda j: (0, 0)),
            pl.BlockSpec((D_MODEL, MOD_TN), lambda j: (0, j)),
            pl.BlockSpec((1, MOD_TN), lambda j: (0, j)),
        ],
        out_specs=pl.BlockSpec((MOD_ROWS, MOD_TN), lambda j: (0, j)),
        out_shape=jax.ShapeDtypeStruct((MOD_ROWS, n), F32),
        compiler_params=_params(1),
        name="adaln_mod",
    )(c_rows, w_mod, b_mod)


TM1 = 256
N_LAT_TILES1 = T_LAT // TM1
N_TILES1 = T_ALL // TM1
TILES_PER_SEQ1 = SEQ // TM1
PCH = 512
LOG2E = 1.4426950408889634
Q_SCALE = ATT_SCALE * LOG2E
V_AUG = 2 * HEAD_DIM


STEP_TILES1 = 2
ROWS1 = STEP_TILES1 * TM1
N_STEPS1 = N_TILES1 // STEP_TILES1
LAT_STEPS1 = N_LAT_TILES1 // STEP_TILES1
STEPS_PER_SEQ1 = TILES_PER_SEQ1 // STEP_TILES1
assert T_CTX == ROWS1


def _inproj_kernel(x0_ref, xb_ref, xn_ref, ctx_ref, sh_ref, sc_ref, g1_ref, w_ref, b_ref,
                   wg_ref, bg_ref, gq_ref, gk_ref, cos_ref, sin_ref,
                   q_ref, k_ref, v_ref, mq_ref, mkt_ref, mv_ref, mo_ref, g_ref, ha_scr, hb_scr):
    i = pl.program_id(0)

    def norm_mod(xv, tile):
        row = jnp.where(tile < N_LAT_TILES1, tile // TILES_PER_SEQ1, BATCH)
        y = (_rms(xv) * g1_ref[...]) * (1.0 + sc_ref[pl.ds(row, 1), :]) + sh_ref[pl.ds(row, 1), :]
        return y.astype(BF16)

    lane = lax.broadcasted_iota(jnp.int32, (TM1, HEAD_DIM), 1)
    even = (lane % 2) == 0

    def project(h_scr, half):
        h = h_scr[...]
        rs = slice(half * TM1, (half + 1) * TM1)
        cos_f = cos_ref[rs, :]
        sin_s = sin_ref[rs, :]

        def proj(c0, c1):
            return jnp.dot(h, w_ref[:, c0:c1], preferred_element_type=F32) + b_ref[:, c0:c1]

        def head_norm_rope(a, g):
            y = _rms(a) * g
            nxt = pltpu.roll(y, HEAD_DIM - 1, 1)
            prv = pltpu.roll(y, 1, 1)
            return y * cos_f + jnp.where(even, nxt, prv) * sin_s

        for c0 in range(C_Q, C_K, PCH):
            acc = proj(c0, c0 + PCH)
            for hh in range(PCH // HEAD_DIM):
                a = acc[:, hh * HEAD_DIM:(hh + 1) * HEAD_DIM]
                r = head_norm_rope(a, gq_ref[...]) * Q_SCALE
                q_ref[rs, c0 + hh * HEAD_DIM:c0 + (hh + 1) * HEAD_DIM] = r.astype(BF16)
        acc = proj(C_K, C_MQ)
        for hh in range(KV_HEADS):
            a = acc[:, hh * HEAD_DIM:(hh + 1) * HEAD_DIM]
            k_ref[rs, hh * HEAD_DIM:(hh + 1) * HEAD_DIM] = (
                head_norm_rope(a, gk_ref[...]).astype(BF16))
        for hh in range(KV_HEADS):
            v_ref[rs, hh * V_AUG:hh * V_AUG + HEAD_DIM] = (
                acc[:, KV_W + hh * HEAD_DIM:KV_W + (hh + 1) * HEAD_DIM].astype(BF16))
            v_ref[rs, hh * V_AUG + HEAD_DIM:(hh + 1) * V_AUG] = jnp.ones((TM1, HEAD_DIM), BF16)
        mq_ref[rs, :] = proj(C_MQ, C_MK).astype(BF16)
        mk = proj(C_MK, C_MV) * (M_DK ** -0.5)
        mkt_ref[:, rs] = mk.T.astype(BF16)
        for c0 in range(C_MV, C_MO, PCH):
            mv_ref[rs, c0 - C_MV:c0 - C_MV + PCH] = proj(c0, c0 + PCH).astype(BF16)
        for c0 in range(C_MO, C_G, PCH):
            mo_ref[rs, c0 - C_MO:c0 - C_MO + PCH] = proj(c0, c0 + PCH)
        g_ref[rs, :] = jnp.dot(h, wg_ref[...], preferred_element_type=F32) + bg_ref[...]

    @pl.when(i == 0)
    def _():
        ha_scr[...] = norm_mod(x0_ref[...], 0)

    tile_b = STEP_TILES1 * i + 1
    x_b = jnp.where(tile_b < N_LAT_TILES1, xb_ref[...], ctx_ref[TM1:ROWS1, :])
    hb_scr[...] = norm_mod(x_b, tile_b)
    project(ha_scr, 0)
    tile_n = STEP_TILES1 * i + 2
    x_n = jnp.where(tile_n < N_LAT_TILES1, xn_ref[...], ctx_ref[0:TM1, :])
    project(hb_scr, 1)
    ha_scr[...] = norm_mod(x_n, tile_n)


def _in_projection(x2, ctx2, sh1, sc1, g1, w_main, b_main, w_gate, b_gate, g_q, g_k, cos_t, sin_t):
    last_lat = N_LAT_TILES1 - 1
    rope_idx = lambda i: (jnp.where(i < LAT_STEPS1, i % STEPS_PER_SEQ1, STEPS_PER_SEQ1), 0)
    full = lambda shape: pl.BlockSpec(shape, lambda i: (0, 0))
    row_blk = lambda w: pl.BlockSpec((ROWS1, w), lambda i: (i, 0))
    out_shapes = [
        jax.ShapeDtypeStruct((T_ALL, ATT_WIDTH), BF16),
        jax.ShapeDtypeStruct((T_ALL, KV_W), BF16),
        jax.ShapeDtypeStruct((T_ALL, KV_HEADS * V_AUG), BF16),
        jax.ShapeDtypeStruct((T_ALL, MQK_W), BF16),
        jax.ShapeDtypeStruct((MQK_W, T_ALL), BF16),
        jax.ShapeDtypeStruct((T_ALL, M_WIDTH), BF16),
        jax.ShapeDtypeStruct((T_ALL, M_WIDTH), F32),
        jax.ShapeDtypeStruct((T_ALL, LANE), F32),
    ]
    out_specs = [
        row_blk(ATT_WIDTH), row_blk(KV_W), row_blk(KV_HEADS * V_AUG), row_blk(MQK_W),
        pl.BlockSpec((MQK_W, ROWS1), lambda i: (0, i)),
        row_blk(M_WIDTH), row_blk(M_WIDTH), row_blk(LANE),
    ]
    x_tile = lambda idx, **kw: pl.BlockSpec((TM1, D_MODEL), idx, **kw)
    return pl.pallas_call(
        _inproj_kernel,
        grid=(N_STEPS1,),
        in_specs=[
            x_tile(lambda i: (0, 0), pipeline_mode=pl.Buffered(1)),
            x_tile(lambda i: (jnp.minimum(STEP_TILES1 * i + 1, last_lat), 0)),
            x_tile(lambda i: (jnp.minimum(STEP_TILES1 * i + 2, last_lat), 0)),
            full((T_CTX, D_MODEL)),
            full((MOD_ROWS, D_MODEL)), full((MOD_ROWS, D_MODEL)), full((1, D_MODEL)),
            pl.BlockSpec((D_MODEL, C_G), lambda i: (0, 0), pipeline_mode=pl.Buffered(1)),
            full((1, C_G)), full((D_MODEL, LANE)), full((1, LANE)),
            full((1, HEAD_DIM)), full((1, HEAD_DIM)),
            pl.BlockSpec((ROWS1, HEAD_DIM), rope_idx),
            pl.BlockSpec((ROWS1, HEAD_DIM), rope_idx),
        ],
        out_specs=out_specs,
        out_shape=out_shapes,
        scratch_shapes=[pltpu.VMEM((TM1, D_MODEL), BF16), pltpu.VMEM((TM1, D_MODEL), BF16)],
        compiler_params=_params(1),
        name="in_proj",
    )(x2, x2, x2, ctx2, sh1, sc1, g1, w_main, b_main, w_gate, b_gate, g_q, g_k, cos_t, sin_t)


TQ = 512
CK = 512
GQ_W = GROUP * HEAD_DIM


def _attn_kernel(q_ref, kl_ref, vl_ref, kc_ref, vc_ref, o_ref):
    chunks = [(kl_ref, vl_ref, c * CK, CK) for c in range(SEQ // CK)]
    chunks.append((kc_ref, vc_ref, 0, CTX_LEN))
    for g in range(GROUP):
        qh = q_ref[:, g * HEAD_DIM:(g + 1) * HEAD_DIM]
        m = jnp.full((TQ, 1), -jnp.inf, F32)
        acc = jnp.zeros((TQ, V_AUG), F32)
        for kr, vr, st, sz in chunks:
            s = lax.dot_general(qh, kr[st:st + sz, :], (((1,), (1,)), ((), ())),
                                preferred_element_type=F32)
            m_new = jnp.maximum(m, jnp.max(s, axis=-1, keepdims=True))
            p = jnp.exp2(s - m_new)
            alpha = jnp.exp2(m - m_new)
            acc = alpha * acc + jnp.dot(p.astype(BF16), vr[st:st + sz, :],
                                        preferred_element_type=F32)
            m = m_new
        o_ref[:, g * HEAD_DIM:(g + 1) * HEAD_DIM] = (
            acc[:, :HEAD_DIM] / acc[:, HEAD_DIM:]).astype(BF16)


def _attention(q, k, v):
    nq = SEQ // TQ
    ctx_blk0 = T_LAT // CTX_LEN
    lat_k = pl.BlockSpec((SEQ, HEAD_DIM), lambda b, h, i: (b, h))
    lat_v = pl.BlockSpec((SEQ, V_AUG), lambda b, h, i: (b, h))
    ctx_k = pl.BlockSpec((CTX_LEN, HEAD_DIM), lambda b, h, i: (ctx_blk0 + b, h))
    ctx_v = pl.BlockSpec((CTX_LEN, V_AUG), lambda b, h, i: (ctx_blk0 + b, h))
    q_blk = pl.BlockSpec((TQ, GQ_W), lambda b, h, i: (b * nq + i, h))
    return pl.pallas_call(
        _attn_kernel,
        grid=(BATCH, KV_HEADS, nq),
        in_specs=[q_blk, lat_k, lat_v, ctx_k, ctx_v],
        out_specs=q_blk,
        out_shape=jax.ShapeDtypeStruct((T_LAT, ATT_WIDTH), BF16),
        compiler_params=_params(3),
        name="gqa_attention",
    )(q, k, v, k, v)


N_CHUNKS = SEQ // M_CHUNK
CTX_CHUNKS = CTX_LEN // M_CHUNK
N_CHAINS = 2 * M_HEADS


def _log_sigmoid(x):
    return jnp.minimum(x, 0.0) - jnp.log(1.0 + jnp.exp(-jnp.abs(x)))


def _split3(x):
    hi = x.astype(BF16)
    r1 = x - hi.astype(F32)
    mid = r1.astype(BF16)
    lo = (r1 - mid.astype(F32)).astype(BF16)
    return hi, mid, lo


def _mlstm_kernel(qf_ref, qb_ref, qc_ref, kf_ref, kb_ref, kc_ref, vf_ref, vb_ref, vc_ref,
                  gf_ref, gb_ref, gc_ref, hf_ref, hb_ref, *state):
    c_scr = state[0:N_CHAINS]
    n_scr = state[N_CHAINS:2 * N_CHAINS]
    m_scr = state[2 * N_CHAINS:3 * N_CHAINS]
    j = pl.program_id(1)
    ri = lax.broadcasted_iota(jnp.int32, (M_CHUNK, M_CHUNK), 0)
    ci = lax.broadcasted_iota(jnp.int32, (M_CHUNK, M_CHUNK), 1)
    lower = ci <= ri
    upper = ci >= ri
    tri_l = jnp.where(lower, 1.0, 0.0).astype(BF16)
    tri_u = jnp.where(upper, 1.0, 0.0).astype(BF16)

    def gate_tables(g, rev):
        gt = g.T
        mcol, mrow = (tri_u, tri_l) if rev else (tri_l, tri_u)
        c_hi, c_mid, c_lo = _split3(_log_sigmoid(g))
        b3 = jnp.dot(mcol, jnp.concatenate([c_hi, c_mid, c_lo], axis=1),
                     preferred_element_type=F32)
        bcol = b3[:, :LANE] + b3[:, LANE:2 * LANE] + b3[:, 2 * LANE:]
        r_hi, r_mid, r_lo = _split3(_log_sigmoid(gt))
        r3 = jnp.dot(jnp.concatenate([r_hi, r_mid, r_lo], axis=0), mrow,
                     preferred_element_type=F32)
        brow = r3[:M_CHUNK] + r3[M_CHUNK:2 * M_CHUNK] + r3[2 * M_CHUNK:]
        return gt, bcol, brow

    def chain_step(chain, q, kt, v, tabs, rev, h_out):
        gt, bcol, brow = tabs
        icol = chain
        fcol = N_CHAINS + chain
        last = 0 if rev else M_CHUNK - 1
        b_col = bcol[:, fcol:fcol + 1]
        b_row = brow[fcol:fcol + 1, :]
        i_row = gt[icol:icol + 1, :]
        tot = brow[fcol:fcol + 1, last:last + 1]
        m_old = m_scr[chain][:1, :1]
        c_old = c_scr[chain][...]
        n_old = n_scr[chain][...]
        v_ones = jnp.concatenate([v, jnp.ones((M_CHUNK, LANE), BF16)], axis=1)
        c_row = i_row - b_row
        if h_out is not None:
            mask = upper if rev else lower
            mm = jnp.maximum(m_old, jnp.max(jnp.where(mask, c_row, -jnp.inf), axis=-1, keepdims=True))
            w_intra = jnp.exp(jnp.where(mask, c_row - mm, -jnp.inf))
            w_inter = jnp.exp(m_old - mm)
            s = jnp.dot(q, kt, preferred_element_type=F32) * w_intra
            qc = jnp.dot(q, c_old.astype(BF16), preferred_element_type=F32)
            qn = jnp.dot(q, n_old.astype(BF16), preferred_element_type=F32)[:, :1]
            sv = jnp.dot(s.astype(BF16), v_ones, preferred_element_type=F32)
            num = sv[:, :M_DV] + w_inter * qc
            den = sv[:, M_DV:M_DV + 1] + w_inter * qn
            h_out(num / jnp.maximum(jnp.abs(den), jnp.exp(-(b_col + mm))))
        g_row = tot + c_row
        m_new = jnp.maximum(tot + m_old, jnp.max(g_row, axis=-1, keepdims=True))
        w_state = jnp.exp(g_row - m_new)
        decay = jnp.exp(tot + m_old - m_new)
        kw = (kt.astype(F32) * w_state).astype(BF16)
        kv = jnp.dot(kw, v_ones, preferred_element_type=F32)
        c_scr[chain][...] = decay * c_old + kv[:, :M_DV]
        n_scr[chain][...] = decay * n_old + kv[:, M_DV:]
        m_scr[chain][...] = jnp.broadcast_to(m_new, (8, LANE))

    def run_chunk(q_ref, kt_ref, v_ref, g_ref, r0, rev, h_ref):
        tabs = gate_tables(g_ref[r0:r0 + M_CHUNK, :], rev)
        for hd in range(M_HEADS):
            chain = (M_HEADS if rev else 0) + hd
            q = q_ref[r0:r0 + M_CHUNK, hd * M_DK:(hd + 1) * M_DK]
            kt = kt_ref[hd * M_DK:(hd + 1) * M_DK, r0:r0 + M_CHUNK]
            v = v_ref[r0:r0 + M_CHUNK, hd * M_DV:(hd + 1) * M_DV]
            if h_ref is None:
                h_out = None
            else:
                def h_out(hv, hd=hd):
                    h_ref[:, hd * M_DV:(hd + 1) * M_DV] = hv
            chain_step(chain, q, kt, v, tabs, rev, h_out)

    @pl.when(j == 0)
    def _():
        for ref in state:
            ref[...] = jnp.zeros_like(ref)
        for cc in range(CTX_CHUNKS):
            run_chunk(qc_ref, kc_ref, vc_ref, gc_ref, cc * M_CHUNK, False, None)
            run_chunk(qc_ref, kc_ref, vc_ref, gc_ref, (CTX_CHUNKS - 1 - cc) * M_CHUNK, True, None)

    run_chunk(qf_ref, kf_ref, vf_ref, gf_ref, 0, False, hf_ref)
    run_chunk(qb_ref, kb_ref, vb_ref, gb_ref, 0, True, hb_ref)


def _mlstm(mq, mkt, mv, gates):
    ctx_blk0 = T_LAT // CTX_LEN
    fwd = lambda b, j: b * N_CHUNKS + j
    bwd = lambda b, j: b * N_CHUNKS + N_CHUNKS - 1 - j
    ctx = lambda b, j: ctx_blk0 + b

    def rows(width, tile, idx):
        return pl.BlockSpec((tile, width), lambda b, j: (idx(b, j), 0))

    def cols(tile, idx):
        return pl.BlockSpec((MQK_W, tile), lambda b, j: (0, idx(b, j)))

    h_shape = jax.ShapeDtypeStruct((T_LAT, M_WIDTH), F32)
    return pl.pallas_call(
        _mlstm_kernel,
        grid=(BATCH, N_CHUNKS),
        in_specs=[
            rows(MQK_W, M_CHUNK, fwd), rows(MQK_W, M_CHUNK, bwd), rows(MQK_W, CTX_LEN, ctx),
            cols(M_CHUNK, fwd), cols(M_CHUNK, bwd), cols(CTX_LEN, ctx),
            rows(M_WIDTH, M_CHUNK, fwd), rows(M_WIDTH, M_CHUNK, bwd), rows(M_WIDTH, CTX_LEN, ctx),
            rows(LANE, M_CHUNK, fwd), rows(LANE, M_CHUNK, bwd), rows(LANE, CTX_LEN, ctx),
        ],
        out_specs=[rows(M_WIDTH, M_CHUNK, fwd), rows(M_WIDTH, M_CHUNK, bwd)],
        out_shape=[h_shape, h_shape],
        scratch_shapes=([pltpu.VMEM((M_DK, M_DV), F32)] * N_CHAINS
                        + [pltpu.VMEM((M_DK, LANE), F32)] * N_CHAINS
                        + [pltpu.VMEM((8, LANE), F32)] * N_CHAINS),
        compiler_params=_params(2),
        name="mlstm_scan",
    )(mq, mq, mq, mkt, mkt, mkt, mv, mv, mv, gates, gates, gates)


TM4 = 256
TILES_PER_SEQ4 = SEQ // TM4
TILES_PER_STEP4 = 2
NEG_BIG = -1e30


def _outproj_kernel(a_ref, hf_ref, hb_ref, mo_ref, x_ref, w_ref, gm_ref, gt1_ref, sh2_ref,
                    sc2_ref, g2_ref, wr_ref, br_ref,
                    x1_ref, h2_ref, ti_ref, tr_ref, tw_ref, tc_ref):
    for hv in range(TILES_PER_STEP4):
        _outproj_tile(hv, a_ref, hf_ref, hb_ref, mo_ref, x_ref, w_ref, gm_ref, gt1_ref, sh2_ref,
                      sc2_ref, g2_ref, wr_ref, br_ref,
                      x1_ref, h2_ref, ti_ref, tr_ref, tw_ref, tc_ref)


def _outproj_tile(hv, a_ref, hf_ref, hb_ref, mo_ref, x_ref, w_ref, gm_ref, gt1_ref, sh2_ref,
                  sc2_ref, g2_ref, wr_ref, br_ref,
                  x1_ref, h2_ref, ti_ref, tr_ref, tw_ref, tc_ref):
    i = pl.program_id(0)
    row = (i * TILES_PER_STEP4 + hv) // TILES_PER_SEQ4
    rs = slice(hv * TM4, (hv + 1) * TM4)
    acc = jnp.dot(a_ref[rs, :], w_ref[0:ATT_WIDTH, :], preferred_element_type=F32)
    for hd in range(M_HEADS):
        sl = slice(hd * M_DV, (hd + 1) * M_DV)
        hn = _rms(hf_ref[rs, sl] + hb_ref[rs, sl]) * gm_ref[:, sl]
        mo = mo_ref[rs, sl]
        ym = hn / (1.0 + jnp.exp(-mo))
        acc += jnp.dot(ym.astype(BF16), w_ref[ATT_WIDTH + hd * M_DV:ATT_WIDTH + (hd + 1) * M_DV, :],
                       preferred_element_type=F32)
    x1 = x_ref[rs, :] + gt1_ref[pl.ds(row, 1), :] * acc
    x1_ref[rs, :] = x1
    h2 = (_rms(x1) * g2_ref[...]) * (1.0 + sc2_ref[pl.ds(row, 1), :]) + sh2_ref[pl.ds(row, 1), :]
    h2_ref[rs, :] = h2
    h_hi = h2.astype(BF16)
    h_lo = (h2 - h_hi.astype(F32)).astype(BF16)
    hi_prod = jnp.dot(h_hi, wr_ref[...], preferred_element_type=F32)
    lo_prod = jnp.dot(h_lo, wr_ref[:, :LANE], preferred_element_type=F32)
    logits = hi_prod[:, :LANE] + lo_prod + hi_prod[:, LANE:] + br_ref[...]
    lane = lax.broadcasted_iota(jnp.int32, (TM4, LANE), 1)
    vals, idxs = [], []
    for _ in range(TOP_K):
        mx = jnp.max(logits, axis=-1, keepdims=True)
        ik = jnp.min(jnp.where(logits == mx, lane, LANE), axis=-1, keepdims=True)
        vals.append(mx)
        idxs.append(ik)
        logits = jnp.where(lane == ik, -jnp.inf, logits)
    es = [jnp.exp(vk - vals[0]) for vk in vals]
    tot = es[0] + es[1] + es[2] + es[3]
    chosen = jnp.zeros((TM4, LANE), F32)
    for kk in range(TOP_K):
        chosen = jnp.where(lane == idxs[kk], 1.0, chosen)
    ri = lax.broadcasted_iota(jnp.int32, (TM4, TM4), 0)
    ci = lax.broadcasted_iota(jnp.int32, (TM4, TM4), 1)
    before = jnp.where(ci < ri, 1.0, 0.0).astype(BF16)
    earlier = jnp.dot(before, chosen.astype(BF16), preferred_element_type=F32)
    ti = jnp.zeros((TM4, LANE), jnp.int32)
    tr = jnp.zeros((TM4, LANE), jnp.int32)
    tw = jnp.zeros((TM4, LANE), F32)
    for kk in range(TOP_K):
        rk = jnp.sum(jnp.where(lane == idxs[kk], earlier, 0.0), axis=-1, keepdims=True)
        ti = jnp.where(lane == kk, idxs[kk], ti)
        tr = jnp.where(lane == kk, rk.astype(jnp.int32), tr)
        tw = jnp.where(lane == kk, es[kk] / tot, tw)
    ti_ref[rs, :] = ti
    tr_ref[rs, :] = tr
    tw_ref[rs, :] = tw
    tc_ref[hv] = jnp.sum(chosen, axis=0, keepdims=True).astype(jnp.int32)


def _out_projection(a_lat, hf, hb, mo, x2, w_out_b, g_mlstm, gt1, sh2, sc2, g2, wr_both, br_p):
    full = lambda shape: pl.BlockSpec(shape, lambda i: (0, 0))
    rows = TILES_PER_STEP4 * TM4
    row_blk = lambda w: pl.BlockSpec((rows, w), lambda i: (i, 0))
    return pl.pallas_call(
        _outproj_kernel,
        grid=(T_LAT // rows,),
        in_specs=[
            row_blk(ATT_WIDTH), row_blk(M_WIDTH), row_blk(M_WIDTH), row_blk(M_WIDTH),
            row_blk(D_MODEL), full((ATT_WIDTH + M_WIDTH, D_MODEL)), full((1, M_WIDTH)),
            full((MOD_ROWS, D_MODEL)), full((MOD_ROWS, D_MODEL)), full((MOD_ROWS, D_MODEL)),
            full((1, D_MODEL)), full((D_MODEL, 2 * LANE)), full((1, LANE)),
        ],
        out_specs=[row_blk(D_MODEL), row_blk(D_MODEL), row_blk(LANE), row_blk(LANE), row_blk(LANE),
                   pl.BlockSpec((TILES_PER_STEP4, 1, LANE), lambda i: (i, 0, 0))],
        out_shape=[
            jax.ShapeDtypeStruct((T_LAT, D_MODEL), F32),
            jax.ShapeDtypeStruct((T_LAT, D_MODEL), F32),
            jax.ShapeDtypeStruct((T_LAT, LANE), jnp.int32),
            jax.ShapeDtypeStruct((T_LAT, LANE), jnp.int32),
            jax.ShapeDtypeStruct((T_LAT, LANE), F32),
            jax.ShapeDtypeStruct((T_LAT // TM4, 1, LANE), jnp.int32),
        ],
        compiler_params=_params(1),
        name="out_proj_router",
    )(a_lat, hf, hb, mo, x2, w_out_b, g_mlstm, gt1, sh2, sc2, g2, wr_both, br_p)


MB = 128
N_ASSIGN = T_LAT * TOP_K
N_BLOCKS = N_ASSIGN // MB + N_EXPERTS
P_ROWS = N_BLOCKS * MB
N_PADS = P_ROWS - N_ASSIGN
SB = 10
W_MAX = N_EXPERTS + (N_BLOCKS - N_EXPERTS) // SB
XROWS = SB * MB


def _routing_tables(top_idx, tile_rank, tile_counts):
    n_tiles = tile_counts.shape[0]
    counts = jnp.sum(tile_counts, axis=0)
    tile_base = jnp.cumsum(tile_counts, axis=0) - tile_counts
    nblk = (counts + MB - 1) // MB
    blk_start = jnp.cumsum(nblk) - nblk
    n_valid_blk = jnp.sum(nblk)
    offs = blk_start[None, :] * MB + tile_base
    e4 = top_idx.reshape(n_tiles, -1, TOP_K, 1)
    hit = e4 == jnp.arange(N_EXPERTS, dtype=jnp.int32)
    dest = jnp.sum(jnp.where(hit, offs[:, None, None, :], 0), axis=-1)
    dest = (dest.reshape(T_LAT, TOP_K) + tile_rank).reshape(N_ASSIGN).astype(jnp.int32)
    nwork = (nblk + SB - 1) // SB
    wend = jnp.cumsum(nwork)
    wstart = wend - nwork
    n_items = wend[-1]
    wid = jnp.arange(W_MAX, dtype=jnp.int32)
    valid = wid < n_items
    we = jnp.minimum(jnp.sum((wend[None, :] <= wid[:, None]).astype(jnp.int32), axis=1),
                     N_EXPERTS - 1)
    local = wid - wstart[we]
    work_blk = jnp.where(valid, blk_start[we] + local * SB, 0)
    work_n = jnp.where(valid, jnp.clip(nblk[we] - local * SB, 0, SB), 0)
    last_e = we[jnp.maximum(n_items - 1, 0)]
    work_e = jnp.where(valid, we, last_e)
    i32 = lambda a: a.astype(jnp.int32)
    return (dest, i32(counts), i32(blk_start), i32(nblk), i32(n_valid_blk)[None],
            i32(work_e), i32(work_blk), i32(work_n))


TMD = 256
ROW_GROUP = 8


def _dispatch_kernel(dest_ref, cnt_ref, bs_ref, nb_ref, nvb_ref, h2_ref, x_hbm, zrow, sem, psem):
    i = pl.program_id(0)

    @pl.when(i == 0)
    def _():
        zrow[...] = jnp.zeros_like(zrow)

        def zero_row(r, carry):
            pltpu.make_async_copy(zrow.at[pl.ds(0, 1)], x_hbm.at[pl.ds(r, 1)], psem).start()
            return carry

        def expert(e, carry):
            first = bs_ref[e] * MB
            lax.fori_loop(first + cnt_ref[e], first + nb_ref[e] * MB, zero_row, 0)
            return carry
        lax.fori_loop(0, N_EXPERTS, expert, 0)
        lax.fori_loop(nvb_ref[0] * MB, P_ROWS, zero_row, 0)

    base = i * (TMD * TOP_K)

    def issue(tg, carry):
        t0 = pl.multiple_of(tg * ROW_GROUP, ROW_GROUP)
        for j in range(ROW_GROUP):
            for kk in range(TOP_K):
                d = dest_ref[base + (t0 + j) * TOP_K + kk]
                pltpu.make_async_copy(h2_ref.at[pl.ds(t0 + j, 1)], x_hbm.at[pl.ds(d, 1)],
                                      sem).start(priority=kk % 2)
        return carry
    lax.fori_loop(0, TMD // ROW_GROUP, issue, 0)
    for kk in range(TOP_K):
        pltpu.make_async_copy(h2_ref, x_hbm.at[pl.ds(0, TMD)], sem).wait()

    @pl.when(i == 0)
    def _():
        for _ in range(N_PADS // TMD):
            pltpu.make_async_copy(h2_ref, x_hbm.at[pl.ds(0, TMD)], psem).wait()


def _dispatch(h2, dest, counts, blk_start, nblk, n_valid_blk):
    grid_spec = pltpu.PrefetchScalarGridSpec(
        num_scalar_prefetch=5,
        grid=(T_LAT // TMD,),
        in_specs=[pl.BlockSpec((TMD, D_MODEL), lambda i, *_: (i, 0))],
        out_specs=pl.BlockSpec(memory_space=pl.ANY),
        scratch_shapes=[
            pltpu.VMEM((8, D_MODEL), F32),
            pltpu.SemaphoreType.DMA(()),
            pltpu.SemaphoreType.DMA(()),
        ],
    )
    return pl.pallas_call(
        _dispatch_kernel,
        grid_spec=grid_spec,
        out_shape=jax.ShapeDtypeStruct((P_ROWS, D_MODEL), F32),
        compiler_params=_params(1),
        name="moe_dispatch",
    )(dest, counts, blk_start, nblk, n_valid_blk, h2)


TF = 512
NF = D_FF // TF
BIG_BLKS = 4
TF2 = TF // 2
W_PRIORITY = 1
W_AHEAD = 2
W_SLOTS = W_AHEAD + 1
MOE_VMEM_LIMIT = 62 * 1024 * 1024


def _moe_kernel(we_ref, wb_ref, wn_ref, nvb_ref,
                x_hbm, w1_hbm, w2_hbm, w1g_ref, w2a_ref, b1g_ref, b1u_ref, b2_ref,
                y_hbm, xbuf, xb16, acc, w1u_buf, w2b_buf, xsem, ysem, wsem):
    w = pl.program_id(0)
    f = pl.program_id(1)
    nsub = wn_ref[w]
    r0 = wb_ref[w] * MB

    step = w * NF + f
    wslot = step % W_SLOTS

    def w_copies(item, ff, slot_):
        e = we_ref[item]
        up = pltpu.make_async_copy(w1_hbm.at[e, :, pl.ds(D_FF + ff * TF, TF)],
                                   w1u_buf.at[slot_], wsem.at[slot_])
        dn = pltpu.make_async_copy(w2_hbm.at[e, pl.ds(ff * TF + TF2, TF2), :],
                                   w2b_buf.at[slot_], wsem.at[slot_])
        return up, dn

    def start_w(item, ff, slot_):
        for cp in w_copies(item, ff, slot_):
            cp.start(priority=W_PRIORITY)

    @pl.when(jnp.logical_and(step == 0, nsub > 0))
    def _():
        for s in range(W_AHEAD):
            start_w(0, s, s)

    last_f = f == NF - 1
    nxt_item = w + (f + W_AHEAD) // NF
    nxt_f = (f + W_AHEAD) % NF

    @pl.when(jnp.logical_and(nxt_item < W_MAX, wn_ref[jnp.minimum(nxt_item, W_MAX - 1)] > 0))
    def _():
        start_w(nxt_item, nxt_f, (step + W_AHEAD) % W_SLOTS)

    @pl.when(nsub > 0)
    def _():
        for cp in w_copies(w, f, wslot):
            cp.wait()

    def x_copy(item, s):
        src = x_hbm.at[pl.ds(wb_ref[item] * MB + s * MB, MB)]
        return pltpu.make_async_copy(src, xbuf.at[pl.ds(s * MB, MB)], xsem)

    def start_x(item):
        def body(s, carry):
            x_copy(item, s).start()
            return carry
        lax.fori_loop(0, wn_ref[item], body, 0)

    @pl.when(jnp.logical_and(w == 0, f == 0))
    def _():
        start_x(0)

    @pl.when(f == 0)
    def _():
        def wait(s, carry):
            x_copy(w, s).wait()
            return carry
        lax.fori_loop(0, nsub, wait, 0)

        def body(s, carry):
            rs = pl.ds(pl.multiple_of(s * MB, MB), MB)
            xb16[rs, :] = xbuf[rs, :].astype(BF16)
            acc[rs, :] = jnp.broadcast_to(b2_ref[...], (MB, D_MODEL))
            return carry
        lax.fori_loop(0, nsub, body, 0)

    @pl.when(jnp.logical_and(f == 1, w + 1 < W_MAX))
    def _():
        start_x(w + 1)

    def y_copy(s):
        rs = pl.ds(pl.multiple_of(s * MB, MB), MB)
        return pltpu.make_async_copy(acc.at[rs], y_hbm.at[pl.ds(r0 + s * MB, MB)], ysem)

    def chunk(row, m):
        rs = pl.ds(row, m)
        xb = xb16[rs, :]
        gate = jnp.dot(xb, w1g_ref[...], preferred_element_type=F32) + b1g_ref[...]
        up = jnp.dot(xb, w1u_buf[wslot], preferred_element_type=F32) + b1u_ref[...]
        gate = jnp.minimum(gate, SWIGLU_LIMIT)
        up = jnp.clip(up, -SWIGLU_LIMIT, SWIGLU_LIMIT)
        glu = gate / (1.0 + jnp.exp(-SWIGLU_ALPHA * gate))
        act = (up + 1.0) * glu
        acc[rs, :] += (jnp.dot(act[:, :TF2], w2a_ref[...], preferred_element_type=F32)
                       + jnp.dot(act[:, TF2:], w2b_buf[wslot], preferred_element_type=F32))

    def write_back(first_blk, n_blk):
        @pl.when(last_f)
        def _():
            for s in range(n_blk):
                y_copy(first_blk + s).start()

    nbig = nsub // BIG_BLKS

    def big(c, carry):
        chunk(pl.multiple_of(c * (BIG_BLKS * MB), BIG_BLKS * MB), BIG_BLKS * MB)
        write_back(c * BIG_BLKS, BIG_BLKS)
        return carry
    lax.fori_loop(0, nbig, big, 0)

    done = nbig * BIG_BLKS
    size = BIG_BLKS // 2
    while size >= 1:
        has = ((nsub - done) // size) % 2 == 1
        first = done + ((nsub - done) // (2 * size)) * (2 * size)

        @pl.when(has)
        def _(first=first, size=size):
            chunk(pl.multiple_of(first * MB, size * MB), size * MB)
            write_back(first, size)
        size //= 2

    @pl.when(last_f)
    def _():
        def wait(s, carry):
            y_copy(s).wait()
            return carry
        lax.fori_loop(0, nsub, wait, 0)

    @pl.when(jnp.logical_and(w == W_MAX - 1, f == NF - 1))
    def _():
        acc[0:MB, :] = jnp.zeros((MB, D_MODEL), F32)

        def tail_copy(bk):
            return pltpu.make_async_copy(acc.at[0:MB], y_hbm.at[pl.ds(bk * MB, MB)], ysem)

        def start(bk, carry):
            tail_copy(bk).start()
            return carry
        lax.fori_loop(nvb_ref[0], N_BLOCKS, start, 0)

        def wait(bk, carry):
            tail_copy(bk).wait()
            return carry
        lax.fori_loop(nvb_ref[0], N_BLOCKS, wait, 0)


def _moe(x_sorted, w1, b1, w2, b2, work_e, work_blk, work_n, n_valid_blk):
    b1r = b1.reshape(N_EXPERTS, 1, 2 * D_FF)
    b2r = b2.reshape(N_EXPERTS, 1, D_MODEL)
    grid_spec = pltpu.PrefetchScalarGridSpec(
        num_scalar_prefetch=4,
        grid=(W_MAX, NF),
        in_specs=[
            pl.BlockSpec(memory_space=pl.ANY),
            pl.BlockSpec(memory_space=pl.ANY),
            pl.BlockSpec(memory_space=pl.ANY),
            pl.BlockSpec((None, D_MODEL, TF), lambda w, f, we, *_: (we[w], 0, f)),
            pl.BlockSpec((None, TF2, D_MODEL), lambda w, f, we, *_: (we[w], 2 * f, 0)),
            pl.BlockSpec((None, 1, TF), lambda w, f, we, *_: (we[w], 0, f)),
            pl.BlockSpec((None, 1, TF), lambda w, f, we, *_: (we[w], 0, NF + f)),
            pl.BlockSpec((None, 1, D_MODEL), lambda w, f, we, *_: (we[w], 0, 0)),
        ],
        out_specs=pl.BlockSpec(memory_space=pl.ANY),
        scratch_shapes=[
            pltpu.VMEM((XROWS, D_MODEL), F32),
            pltpu.VMEM((XROWS, D_MODEL), BF16),
            pltpu.VMEM((XROWS, D_MODEL), F32),
            pltpu.VMEM((W_SLOTS, D_MODEL, TF), F32),
            pltpu.VMEM((W_SLOTS, TF2, D_MODEL), F32),
            pltpu.SemaphoreType.DMA(()),
            pltpu.SemaphoreType.DMA(()),
            pltpu.SemaphoreType.DMA((W_SLOTS,)),
        ],
    )
    return pl.pallas_call(
        _moe_kernel,
        grid_spec=grid_spec,
        out_shape=jax.ShapeDtypeStruct((P_ROWS, D_MODEL), F32),
        compiler_params=pltpu.CompilerParams(
            dimension_semantics=("arbitrary", "arbitrary"), vmem_limit_bytes=MOE_VMEM_LIMIT),
        name="moe_experts",
    )(work_e, work_blk, work_n, n_valid_blk, x_sorted, w1, w2, w1, w2, b1r, b1r, b2r)


TM7 = 256
N_TILES7 = T_LAT // TM7
TILES_PER_SEQ7 = SEQ // TM7


def _final_kernel(pos_ref, x1_ref, tw_ref, gt2_ref, gf_ref, y_hbm, o_ref, ybuf, sem):
    i = pl.program_id(0)
    slot = i % 2

    def start_gather(tile, slot_):
        base = tile * (TM7 * TOP_K)

        def body(tg, carry):
            t0 = pl.multiple_of(tg * ROW_GROUP, ROW_GROUP)
            for j in range(ROW_GROUP):
                for kk in range(TOP_K):
                    p = pos_ref[base + (t0 + j) * TOP_K + kk]
                    pltpu.make_async_copy(y_hbm.at[pl.ds(p, 1)],
                                          ybuf.at[slot_, kk, pl.ds(t0 + j, 1)],
                                          sem.at[slot_]).start(priority=kk % 2)
            return carry
        lax.fori_loop(0, TM7 // ROW_GROUP, body, 0)

    @pl.when(i == 0)
    def _():
        start_gather(0, 0)

    @pl.when(i + 1 < N_TILES7)
    def _():
        start_gather(i + 1, 1 - slot)

    for kk in range(TOP_K):
        pltpu.make_async_copy(y_hbm.at[pl.ds(0, TM7)], ybuf.at[slot, kk], sem.at[slot]).wait()

    row = i // TILES_PER_SEQ7
    tw = tw_ref[...]
    mix = jnp.zeros((TM7, D_MODEL), F32)
    for kk in range(TOP_K):
        mix += ybuf[slot, kk] * tw[:, kk:kk + 1]
    xo = x1_ref[...] + gt2_ref[pl.ds(row, 1), :] * mix
    o_ref[...] = _rms(xo) * gf_ref[...]


def _final(dest, x1, top_w, gt2, g_final, y_sorted):
    grid_spec = pltpu.PrefetchScalarGridSpec(
        num_scalar_prefetch=1,
        grid=(N_TILES7,),
        in_specs=[
            pl.BlockSpec((TM7, D_MODEL), lambda i, *_: (i, 0)),
            pl.BlockSpec((TM7, LANE), lambda i, *_: (i, 0)),
            pl.BlockSpec((MOD_ROWS, D_MODEL), lambda i, *_: (0, 0)),
            pl.BlockSpec((1, D_MODEL), lambda i, *_: (0, 0)),
            pl.BlockSpec(memory_space=pl.ANY),
        ],
        out_specs=pl.BlockSpec((TM7, D_MODEL), lambda i, *_: (i, 0)),
        scratch_shapes=[
            pltpu.VMEM((2, TOP_K, TM7, D_MODEL), F32),
            pltpu.SemaphoreType.DMA((2,)),
        ],
    )
    return pl.pallas_call(
        _final_kernel,
        grid_spec=grid_spec,
        out_shape=jax.ShapeDtypeStruct((T_LAT, D_MODEL), F32),
        compiler_params=_params(1),
        name="combine_final_norm",
    )(dest, x1, top_w, gt2, g_final, y_sorted)


def _rope_tables():
    rows = SEQ // GRID_W
    row = jnp.repeat(jnp.arange(rows, dtype=F32), GRID_W)
    col = jnp.tile(jnp.arange(GRID_W, dtype=F32), rows)
    inv = ROPE_THETA ** (-jnp.arange(0, AXIS_DIM, 2, dtype=F32) / AXIS_DIM)
    ang = jnp.concatenate([row[:, None] * inv, col[:, None] * inv], axis=-1)
    cos = jnp.repeat(jnp.cos(ang), 2, axis=-1)
    sin = jnp.repeat(jnp.sin(ang), 2, axis=-1)
    sign = jnp.tile(jnp.array([-1.0, 1.0], F32), HEAD_DIM // 2)
    cos_t = jnp.concatenate([cos, jnp.ones((ROWS1, HEAD_DIM), F32)], axis=0)
    sin_t = jnp.concatenate([sin * sign, jnp.zeros((ROWS1, HEAD_DIM), F32)], axis=0)
    return cos_t, sin_t


def kernel(x, c, ctx, c_ctx, w_mod, b_mod, g_norm1, w_in, b_in, g_q, g_k, g_mlstm, w_out,
           g_norm2, w_router, b_router, w1, b1, w2, b2, g_final):
    x2 = x.reshape(T_LAT, D_MODEL)
    ctx2 = ctx.reshape(T_CTX, D_MODEL)
    cos_t, sin_t = _rope_tables()
    assert w_mod.shape[0] == 1
    l = 0
    c_rows = jnp.concatenate(
        [c, c_ctx[None, :], jnp.zeros((MOD_ROWS - BATCH - 1, D_MODEL), F32)], axis=0)
    mod = _modulation(c_rows, w_mod[l], b_mod[l][None, :])
    sh1, sc1, gt1, sh2, sc2, gt2 = [mod[:, k * D_MODEL:(k + 1) * D_MODEL]
                                    for k in range(N_ADALN)]
    pad = IN_COLS_PAD - IN_COLS
    w_main = w_in[l].astype(BF16)
    w_gate = jnp.pad(w_in[l][:, C_G:], ((0, 0), (0, pad))).astype(BF16)
    b_main = b_in[l][None, :C_G]
    b_gate = jnp.pad(b_in[l][C_G:], (0, pad))[None, :]
    q, k, v, mq, mkt, mv, mo, gates = _in_projection(
        x2, ctx2, sh1, sc1, g_norm1[l][None, :], w_main, b_main, w_gate, b_gate,
        g_q[l][None, :], g_k[l][None, :], cos_t, sin_t)
    a_lat = _attention(q, k, v)
    hf, hb = _mlstm(mq, mkt, mv, gates)
    wr_p = jnp.pad(w_router[l], ((0, 0), (0, LANE - N_EXPERTS)))
    wr_hi = wr_p.astype(BF16)
    wr_lo = (wr_p - wr_hi.astype(F32)).astype(BF16)
    br_p = jnp.pad(b_router[l], (0, LANE - N_EXPERTS), constant_values=NEG_BIG)[None, :]
    x1, h2, top_i, top_r, top_w, tile_cnt = _out_projection(
        a_lat, hf, hb, mo, x2, w_out[l].astype(BF16), g_mlstm[l][None, :],
        gt1, sh2, sc2, g_norm2[l][None, :], jnp.concatenate([wr_hi, wr_lo], axis=1), br_p)
    dest, counts, blk_start, nblk, n_valid_blk, work_e, work_blk, work_n = _routing_tables(
        top_i[:, :TOP_K], top_r[:, :TOP_K], tile_cnt[:, 0, :N_EXPERTS])
    x_sorted = _dispatch(h2, dest, counts, blk_start, nblk, n_valid_blk)
    y_sorted = _moe(x_sorted, w1[l], b1[l], w2[l], b2[l], work_e, work_blk, work_n, n_valid_blk)
    out = _final(dest, x1, top_w, gt2, g_final[None, :], y_sorted)
    return out.reshape(BATCH, SEQ, D_MODEL)
```

```python
import jax
import jax.numpy as jnp
from jax import lax
from jax.experimental import pallas as pl
from jax.experimental.pallas import tpu as pltpu

F32 = jnp.float32
BF16 = jnp.bfloat16

D_MODEL = 2048
BATCH = 2
SEQ = 4096
CTX_LEN = 256
GRID_W = 64
N_ADALN = 6
EPS = 1e-6
N_HEADS = 8
KV_HEADS = 2
GROUP = N_HEADS // KV_HEADS
HEAD_DIM = 128
AXIS_DIM = HEAD_DIM // 2
ROPE_THETA = 10000.0
ATT_WIDTH = N_HEADS * HEAD_DIM
ATT_SCALE = HEAD_DIM ** -0.5
M_HEADS = 4
M_DK = 128
M_DV = 256
M_CHUNK = 128
M_WIDTH = M_HEADS * M_DV
N_EXPERTS = 32
TOP_K = 4
D_FF = D_MODEL
SWIGLU_LIMIT = 7.0
SWIGLU_ALPHA = 1.702

T_LAT = BATCH * SEQ
T_CTX = BATCH * CTX_LEN
T_ALL = T_LAT + T_CTX
KV_W = KV_HEADS * HEAD_DIM
MQK_W = M_HEADS * M_DK
LANE = 128

C_Q = 0
C_K = C_Q + ATT_WIDTH
C_V = C_K + KV_W
C_MQ = C_V + KV_W
C_MK = C_MQ + MQK_W
C_MV = C_MK + MQK_W
C_MO = C_MV + M_WIDTH
C_G = C_MO + M_WIDTH
IN_COLS = C_G + 4 * M_HEADS
IN_COLS_PAD = C_G + LANE

VMEM_LIMIT = 56 * 1024 * 1024


def _params(n_axes):
    return pltpu.CompilerParams(
        dimension_semantics=("arbitrary",) * n_axes, vmem_limit_bytes=VMEM_LIMIT)


def _rms(x):
    return x * lax.rsqrt(jnp.mean(x * x, axis=-1, keepdims=True) + EPS)


MOD_ROWS = 8
MOD_TN = 1024


def _mod_kernel(c_ref, w_ref, b_ref, o_ref):
    c = c_ref[...]
    s = c / (1.0 + jnp.exp(-c))
    o_ref[...] = jnp.dot(s, w_ref[...], preferred_element_type=F32) + b_ref[...]


def _modulation(c_rows, w_mod, b_mod):
    n = w_mod.shape[1]
    return pl.pallas_call(
        _mod_kernel,
        grid=(n // MOD_TN,),
        in_specs=[
            pl.BlockSpec((MOD_ROWS, D_MODEL), lambda j: (0, 0)),
            pl.BlockSpec((D_MODEL, MOD_TN), lambda j: (0, j)),
            pl.BlockSpec((1, MOD_TN), lambda j: (0, j)),
        ],
        out_specs=pl.BlockSpec((MOD_ROWS, MOD_TN), lambda j: (0, j)),
        out_shape=jax.ShapeDtypeStruct((MOD_ROWS, n), F32),
        compiler_params=_params(1),
        name="adaln_mod",
    )(c_rows, w_mod, b_mod)


TM1 = 256
N_LAT_TILES1 = T_LAT // TM1
N_TILES1 = T_ALL // TM1
TILES_PER_SEQ1 = SEQ // TM1
PCH = 512
LOG2E = 1.4426950408889634
Q_SCALE = ATT_SCALE * LOG2E
V_AUG = 2 * HEAD_DIM


STEP_TILES1 = 2
ROWS1 = STEP_TILES1 * TM1
N_STEPS1 = N_TILES1 // STEP_TILES1
LAT_STEPS1 = N_LAT_TILES1 // STEP_TILES1
STEPS_PER_SEQ1 = TILES_PER_SEQ1 // STEP_TILES1
assert T_CTX == ROWS1


def _inproj_kernel(x0_ref, xb_ref, xn_ref, ctx_ref, sh_ref, sc_ref, g1_ref, w_ref, b_ref,
                   wg_ref, bg_ref, gq_ref, gk_ref, cos_ref, sin_ref,
                   q_ref, k_ref, v_ref, mq_ref, mkt_ref, mv_ref, mo_ref, g_ref, ha_scr, hb_scr):
    i = pl.program_id(0)

    def norm_mod(xv, tile):
        row = jnp.where(tile < N_LAT_TILES1, tile // TILES_PER_SEQ1, BATCH)
        y = (_rms(xv) * g1_ref[...]) * (1.0 + sc_ref[pl.ds(row, 1), :]) + sh_ref[pl.ds(row, 1), :]
        return y.astype(BF16)

    lane = lax.broadcasted_iota(jnp.int32, (TM1, HEAD_DIM), 1)
    even = (lane % 2) == 0

    def project(h_scr, half):
        h = h_scr[...]
        rs = slice(half * TM1, (half + 1) * TM1)
        cos_f = cos_ref[rs, :]
        sin_s = sin_ref[rs, :]

        def proj(c0, c1):
            return jnp.dot(h, w_ref[:, c0:c1], preferred_element_type=F32) + b_ref[:, c0:c1]

        def head_norm_rope(a, g):
            y = _rms(a) * g
            nxt = pltpu.roll(y, HEAD_DIM - 1, 1)
            prv = pltpu.roll(y, 1, 1)
            return y * cos_f + jnp.where(even, nxt, prv) * sin_s

        for c0 in range(C_Q, C_K, PCH):
            acc = proj(c0, c0 + PCH)
            for hh in range(PCH // HEAD_DIM):
                a = acc[:, hh * HEAD_DIM:(hh + 1) * HEAD_DIM]
                r = head_norm_rope(a, gq_ref[...]) * Q_SCALE
                q_ref[rs, c0 + hh * HEAD_DIM:c0 + (hh + 1) * HEAD_DIM] = r.astype(BF16)
        acc = proj(C_K, C_MQ)
        for hh in range(KV_HEADS):
            a = acc[:, hh * HEAD_DIM:(hh + 1) * HEAD_DIM]
            k_ref[rs, hh * HEAD_DIM:(hh + 1) * HEAD_DIM] = (
                head_norm_rope(a, gk_ref[...]).astype(BF16))
        for hh in range(KV_HEADS):
            v_ref[rs, hh * V_AUG:hh * V_AUG + HEAD_DIM] = (
                acc[:, KV_W + hh * HEAD_DIM:KV_W + (hh + 1) * HEAD_DIM].astype(BF16))
            v_ref[rs, hh * V_AUG + HEAD_DIM:(hh + 1) * V_AUG] = jnp.ones((TM1, HEAD_DIM), BF16)
        mq_ref[rs, :] = proj(C_MQ, C_MK).astype(BF16)
        mk = proj(C_MK, C_MV) * (M_DK ** -0.5)
        mkt_ref[:, rs] = mk.T.astype(BF16)
        for c0 in range(C_MV, C_MO, PCH):
            mv_ref[rs, c0 - C_MV:c0 - C_MV + PCH] = proj(c0, c0 + PCH).astype(BF16)
        for c0 in range(C_MO, C_G, PCH):
            mo_ref[rs, c0 - C_MO:c0 - C_MO + PCH] = proj(c0, c0 + PCH)
        g_ref[rs, :] = jnp.dot(h, wg_ref[...], preferred_element_type=F32) + bg_ref[...]

    @pl.when(i == 0)
    def _():
        ha_scr[...] = norm_mod(x0_ref[...], 0)

    tile_b = STEP_TILES1 * i + 1
    x_b = jnp.where(tile_b < N_LAT_TILES1, xb_ref[...], ctx_ref[TM1:ROWS1, :])
    hb_scr[...] = norm_mod(x_b, tile_b)
    project(ha_scr, 0)
    tile_n = STEP_TILES1 * i + 2
    x_n = jnp.where(tile_n < N_LAT_TILES1, xn_ref[...], ctx_ref[0:TM1, :])
    project(hb_scr, 1)
    ha_scr[...] = norm_mod(x_n, tile_n)


def _in_projection(x2, ctx2, sh1, sc1, g1, w_main, b_main, w_gate, b_gate, g_q, g_k, cos_t, sin_t):
    last_lat = N_LAT_TILES1 - 1
    rope_idx = lambda i: (jnp.where(i < LAT_STEPS1, i % STEPS_PER_SEQ1, STEPS_PER_SEQ1), 0)
    full = lambda shape: pl.BlockSpec(shape, lambda i: (0, 0))
    row_blk = lambda w: pl.BlockSpec((ROWS1, w), lambda i: (i, 0))
    out_shapes = [
        jax.ShapeDtypeStruct((T_ALL, ATT_WIDTH), BF16),
        jax.ShapeDtypeStruct((T_ALL, KV_W), BF16),
        jax.ShapeDtypeStruct((T_ALL, KV_HEADS * V_AUG), BF16),
        jax.ShapeDtypeStruct((T_ALL, MQK_W), BF16),
        jax.ShapeDtypeStruct((MQK_W, T_ALL), BF16),
        jax.ShapeDtypeStruct((T_ALL, M_WIDTH), BF16),
        jax.ShapeDtypeStruct((T_ALL, M_WIDTH), F32),
        jax.ShapeDtypeStruct((T_ALL, LANE), F32),
    ]
    out_specs = [
        row_blk(ATT_WIDTH), row_blk(KV_W), row_blk(KV_HEADS * V_AUG), row_blk(MQK_W),
        pl.BlockSpec((MQK_W, ROWS1), lambda i: (0, i)),
        row_blk(M_WIDTH), row_blk(M_WIDTH), row_blk(LANE),
    ]
    x_tile = lambda idx, **kw: pl.BlockSpec((TM1, D_MODEL), idx, **kw)
    return pl.pallas_call(
        _inproj_kernel,
        grid=(N_STEPS1,),
        in_specs=[
            x_tile(lambda i: (0, 0), pipeline_mode=pl.Buffered(1)),
            x_tile(lambda i: (jnp.minimum(STEP_TILES1 * i + 1, last_lat), 0)),
            x_tile(lambda i: (jnp.minimum(STEP_TILES1 * i + 2, last_lat), 0)),
            full((T_CTX, D_MODEL)),
            full((MOD_ROWS, D_MODEL)), full((MOD_ROWS, D_MODEL)), full((1, D_MODEL)),
            pl.BlockSpec((D_MODEL, C_G), lambda i: (0, 0), pipeline_mode=pl.Buffered(1)),
            full((1, C_G)), full((D_MODEL, LANE)), full((1, LANE)),
            full((1, HEAD_DIM)), full((1, HEAD_DIM)),
            pl.BlockSpec((ROWS1, HEAD_DIM), rope_idx),
            pl.BlockSpec((ROWS1, HEAD_DIM), rope_idx),
        ],
        out_specs=out_specs,
        out_shape=out_shapes,
        scratch_shapes=[pltpu.VMEM((TM1, D_MODEL), BF16), pltpu.VMEM((TM1, D_MODEL), BF16)],
        compiler_params=_params(1),
        name="in_proj",
    )(x2, x2, x2, ctx2, sh1, sc1, g1, w_main, b_main, w_gate, b_gate, g_q, g_k, cos_t, sin_t)


TQ = 512
CK = 512
GQ_W = GROUP * HEAD_DIM


def _attn_kernel(q_ref, kl_ref, vl_ref, kc_ref, vc_ref, o_ref):
    chunks = [(kl_ref, vl_ref, c * CK, CK) for c in range(SEQ // CK)]
    chunks.append((kc_ref, vc_ref, 0, CTX_LEN))
    for g in range(GROUP):
        qh = q_ref[:, g * HEAD_DIM:(g + 1) * HEAD_DIM]
        m = jnp.full((TQ, 1), -jnp.inf, F32)
        acc = jnp.zeros((TQ, V_AUG), F32)
        for kr, vr, st, sz in chunks:
            s = lax.dot_general(qh, kr[st:st + sz, :], (((1,), (1,)), ((), ())),
                                preferred_element_type=F32)
            m_new = jnp.maximum(m, jnp.max(s, axis=-1, keepdims=True))
            p = jnp.exp2(s - m_new)
            alpha = jnp.exp2(m - m_new)
            acc = alpha * acc + jnp.dot(p.astype(BF16), vr[st:st + sz, :],
                                        preferred_element_type=F32)
            m = m_new
        o_ref[:, g * HEAD_DIM:(g + 1) * HEAD_DIM] = (
            acc[:, :HEAD_DIM] / acc[:, HEAD_DIM:]).astype(BF16)


def _attention(q, k, v):
    nq = SEQ // TQ
    ctx_blk0 = T_LAT // CTX_LEN
    lat_k = pl.BlockSpec((SEQ, HEAD_DIM), lambda b, h, i: (b, h))
    lat_v = pl.BlockSpec((SEQ, V_AUG), lambda b, h, i: (b, h))
    ctx_k = pl.BlockSpec((CTX_LEN, HEAD_DIM), lambda b, h, i: (ctx_blk0 + b, h))
    ctx_v = pl.BlockSpec((CTX_LEN, V_AUG), lambda b, h, i: (ctx_blk0 + b, h))
    q_blk = pl.BlockSpec((TQ, GQ_W), lambda b, h, i: (b * nq + i, h))
    return pl.pallas_call(
        _attn_kernel,
        grid=(BATCH, KV_HEADS, nq),
        in_specs=[q_blk, lat_k, lat_v, ctx_k, ctx_v],
        out_specs=q_blk,
        out_shape=jax.ShapeDtypeStruct((T_LAT, ATT_WIDTH), BF16),
        compiler_params=_params(3),
        name="gqa_attention",
    )(q, k, v, k, v)


N_CHUNKS = SEQ // M_CHUNK
CTX_CHUNKS = CTX_LEN // M_CHUNK
N_CHAINS = 2 * M_HEADS


def _log_sigmoid(x):
    return jnp.minimum(x, 0.0) - jnp.log(1.0 + jnp.exp(-jnp.abs(x)))


def _split3(x):
    hi = x.astype(BF16)
    r1 = x - hi.astype(F32)
    mid = r1.astype(BF16)
    lo = (r1 - mid.astype(F32)).astype(BF16)
    return hi, mid, lo


def _mlstm_kernel(qf_ref, qb_ref, qc_ref, kf_ref, kb_ref, kc_ref, vf_ref, vb_ref, vc_ref,
                  gf_ref, gb_ref, gc_ref, hf_ref, hb_ref, *state):
    c_scr = state[0:N_CHAINS]
    n_scr = state[N_CHAINS:2 * N_CHAINS]
    m_scr = state[2 * N_CHAINS:3 * N_CHAINS]
    j = pl.program_id(1)
    ri = lax.broadcasted_iota(jnp.int32, (M_CHUNK, M_CHUNK), 0)
    ci = lax.broadcasted_iota(jnp.int32, (M_CHUNK, M_CHUNK), 1)
    lower = ci <= ri
    upper = ci >= ri
    tri_l = jnp.where(lower, 1.0, 0.0).astype(BF16)
    tri_u = jnp.where(upper, 1.0, 0.0).astype(BF16)

    def gate_tables(g, rev):
        gt = g.T
        mcol, mrow = (tri_u, tri_l) if rev else (tri_l, tri_u)
        c_hi, c_mid, c_lo = _split3(_log_sigmoid(g))
        b3 = jnp.dot(mcol, jnp.concatenate([c_hi, c_mid, c_lo], axis=1),
                     preferred_element_type=F32)
        bcol = b3[:, :LANE] + b3[:, LANE:2 * LANE] + b3[:, 2 * LANE:]
        r_hi, r_mid, r_lo = _split3(_log_sigmoid(gt))
        r3 = jnp.dot(jnp.concatenate([r_hi, r_mid, r_lo], axis=0), mrow,
                     preferred_element_type=F32)
        brow = r3[:M_CHUNK] + r3[M_CHUNK:2 * M_CHUNK] + r3[2 * M_CHUNK:]
        return gt, bcol, brow

    def chain_step(chain, q, kt, v, tabs, rev, h_out):
        gt, bcol, brow = tabs
        icol = chain
        fcol = N_CHAINS + chain
        last = 0 if rev else M_CHUNK - 1
        b_col = bcol[:, fcol:fcol + 1]
        b_row = brow[fcol:fcol + 1, :]
        i_row = gt[icol:icol + 1, :]
        tot = brow[fcol:fcol + 1, last:last + 1]
        m_old = m_scr[chain][:1, :1]
        c_old = c_scr[chain][...]
        n_old = n_scr[chain][...]
        v_ones = jnp.concatenate([v, jnp.ones((M_CHUNK, LANE), BF16)], axis=1)
        c_row = i_row - b_row
        if h_out is not None:
            mask = upper if rev else lower
            mm = jnp.maximum(m_old, jnp.max(jnp.where(mask, c_row, -jnp.inf), axis=-1, keepdims=True))
            w_intra = jnp.exp(jnp.where(mask, c_row - mm, -jnp.inf))
            w_inter = jnp.exp(m_old - mm)
            s = jnp.dot(q, kt, preferred_element_type=F32) * w_intra
            qc = jnp.dot(q, c_old.astype(BF16), preferred_element_type=F32)
            qn = jnp.dot(q, n_old.astype(BF16), preferred_element_type=F32)[:, :1]
            sv = jnp.dot(s.astype(BF16), v_ones, preferred_element_type=F32)
            num = sv[:, :M_DV] + w_inter * qc
            den = sv[:, M_DV:M_DV + 1] + w_inter * qn
            h_out(num / jnp.maximum(jnp.abs(den), jnp.exp(-(b_col + mm))))
        g_row = tot + c_row
        m_new = jnp.maximum(tot + m_old, jnp.max(g_row, axis=-1, keepdims=True))
        w_state = jnp.exp(g_row - m_new)
        decay = jnp.exp(tot + m_old - m_new)
        kw = (kt.astype(F32) * w_state).astype(BF16)
        kv = jnp.dot(kw, v_ones, preferred_element_type=F32)
        c_scr[chain][...] = decay * c_old + kv[:, :M_DV]
        n_scr[chain][...] = decay * n_old + kv[:, M_DV:]
        m_scr[chain][...] = jnp.broadcast_to(m_new, (8, LANE))

    def run_chunk(q_ref, kt_ref, v_ref, g_ref, r0, rev, h_ref):
        tabs = gate_tables(g_ref[r0:r0 + M_CHUNK, :], rev)
        for hd in range(M_HEADS):
            chain = (M_HEADS if rev else 0) + hd
            q = q_ref[r0:r0 + M_CHUNK, hd * M_DK:(hd + 1) * M_DK]
            kt = kt_ref[hd * M_DK:(hd + 1) * M_DK, r0:r0 + M_CHUNK]
            v = v_ref[r0:r0 + M_CHUNK, hd * M_DV:(hd + 1) * M_DV]
            if h_ref is None:
                h_out = None
            else:
                def h_out(hv, hd=hd):
                    h_ref[:, hd * M_DV:(hd + 1) * M_DV] = hv
            chain_step(chain, q, kt, v, tabs, rev, h_out)

    @pl.when(j == 0)
    def _():
        for ref in state:
            ref[...] = jnp.zeros_like(ref)
        for cc in range(CTX_CHUNKS):
            run_chunk(qc_ref, kc_ref, vc_ref, gc_ref, cc * M_CHUNK, False, None)
            run_chunk(qc_ref, kc_ref, vc_ref, gc_ref, (CTX_CHUNKS - 1 - cc) * M_CHUNK, True, None)

    run_chunk(qf_ref, kf_ref, vf_ref, gf_ref, 0, False, hf_ref)
    run_chunk(qb_ref, kb_ref, vb_ref, gb_ref, 0, True, hb_ref)


def _mlstm(mq, mkt, mv, gates):
    ctx_blk0 = T_LAT // CTX_LEN
    fwd = lambda b, j: b * N_CHUNKS + j
    bwd = lambda b, j: b * N_CHUNKS + N_CHUNKS - 1 - j
    ctx = lambda b, j: ctx_blk0 + b

    def rows(width, tile, idx):
        return pl.BlockSpec((tile, width), lambda b, j: (idx(b, j), 0))

    def cols(tile, idx):
        return pl.BlockSpec((MQK_W, tile), lambda b, j: (0, idx(b, j)))

    h_shape = jax.ShapeDtypeStruct((T_LAT, M_WIDTH), F32)
    return pl.pallas_call(
        _mlstm_kernel,
        grid=(BATCH, N_CHUNKS),
        in_specs=[
            rows(MQK_W, M_CHUNK, fwd), rows(MQK_W, M_CHUNK, bwd), rows(MQK_W, CTX_LEN, ctx),
            cols(M_CHUNK, fwd), cols(M_CHUNK, bwd), cols(CTX_LEN, ctx),
            rows(M_WIDTH, M_CHUNK, fwd), rows(M_WIDTH, M_CHUNK, bwd), rows(M_WIDTH, CTX_LEN, ctx),
            rows(LANE, M_CHUNK, fwd), rows(LANE, M_CHUNK, bwd), rows(LANE, CTX_LEN, ctx),
        ],
        out_specs=[rows(M_WIDTH, M_CHUNK, fwd), rows(M_WIDTH, M_CHUNK, bwd)],
        out_shape=[h_shape, h_shape],
        scratch_shapes=([pltpu.VMEM((M_DK, M_DV), F32)] * N_CHAINS
                        + [pltpu.VMEM((M_DK, LANE), F32)] * N_CHAINS
                        + [pltpu.VMEM((8, LANE), F32)] * N_CHAINS),
        compiler_params=_params(2),
        name="mlstm_scan",
    )(mq, mq, mq, mkt, mkt, mkt, mv, mv, mv, gates, gates, gates)


TM4 = 256
TILES_PER_SEQ4 = SEQ // TM4
TILES_PER_STEP4 = 2
NEG_BIG = -1e30


def _outproj_kernel(a_ref, hf_ref, hb_ref, mo_ref, x_ref, w_ref, gm_ref, gt1_ref, sh2_ref,
                    sc2_ref, g2_ref, wr_ref, br_ref,
                    x1_ref, h2_ref, ti_ref, tr_ref, tw_ref, tc_ref):
    for hv in range(TILES_PER_STEP4):
        _outproj_tile(hv, a_ref, hf_ref, hb_ref, mo_ref, x_ref, w_ref, gm_ref, gt1_ref, sh2_ref,
                      sc2_ref, g2_ref, wr_ref, br_ref,
                      x1_ref, h2_ref, ti_ref, tr_ref, tw_ref, tc_ref)


def _outproj_tile(hv, a_ref, hf_ref, hb_ref, mo_ref, x_ref, w_ref, gm_ref, gt1_ref, sh2_ref,
                  sc2_ref, g2_ref, wr_ref, br_ref,
                  x1_ref, h2_ref, ti_ref, tr_ref, tw_ref, tc_ref):
    i = pl.program_id(0)
    row = (i * TILES_PER_STEP4 + hv) // TILES_PER_SEQ4
    rs = slice(hv * TM4, (hv + 1) * TM4)
    acc = jnp.dot(a_ref[rs, :], w_ref[0:ATT_WIDTH, :], preferred_element_type=F32)
    for hd in range(M_HEADS):
        sl = slice(hd * M_DV, (hd + 1) * M_DV)
        hn = _rms(hf_ref[rs, sl] + hb_ref[rs, sl]) * gm_ref[:, sl]
        mo = mo_ref[rs, sl]
        ym = hn / (1.0 + jnp.exp(-mo))
        acc += jnp.dot(ym.astype(BF16), w_ref[ATT_WIDTH + hd * M_DV:ATT_WIDTH + (hd + 1) * M_DV, :],
                       preferred_element_type=F32)
    x1 = x_ref[rs, :] + gt1_ref[pl.ds(row, 1), :] * acc
    x1_ref[rs, :] = x1
    h2 = (_rms(x1) * g2_ref[...]) * (1.0 + sc2_ref[pl.ds(row, 1), :]) + sh2_ref[pl.ds(row, 1), :]
    h2_ref[rs, :] = h2
    h_hi = h2.astype(BF16)
    h_lo = (h2 - h_hi.astype(F32)).astype(BF16)
    hi_prod = jnp.dot(h_hi, wr_ref[...], preferred_element_type=F32)
    lo_prod = jnp.dot(h_lo, wr_ref[:, :LANE], preferred_element_type=F32)
    logits = hi_prod[:, :LANE] + lo_prod + hi_prod[:, LANE:] + br_ref[...]
    lane = lax.broadcasted_iota(jnp.int32, (TM4, LANE), 1)
    vals, idxs = [], []
    for _ in range(TOP_K):
        mx = jnp.max(logits, axis=-1, keepdims=True)
        ik = jnp.min(jnp.where(logits == mx, lane, LANE), axis=-1, keepdims=True)
        vals.append(mx)
        idxs.append(ik)
        logits = jnp.where(lane == ik, -jnp.inf, logits)
    es = [jnp.exp(vk - vals[0]) for vk in vals]
    tot = es[0] + es[1] + es[2] + es[3]
    chosen = jnp.zeros((TM4, LANE), F32)
    for kk in range(TOP_K):
        chosen = jnp.where(lane == idxs[kk], 1.0, chosen)
    ri = lax.broadcasted_iota(jnp.int32, (TM4, TM4), 0)
    ci = lax.broadcasted_iota(jnp.int32, (TM4, TM4), 1)
    before = jnp.where(ci < ri, 1.0, 0.0).astype(BF16)
    earlier = jnp.dot(before, chosen.astype(BF16), preferred_element_type=F32)
    ti = jnp.zeros((TM4, LANE), jnp.int32)
    tr = jnp.zeros((TM4, LANE), jnp.int32)
    tw = jnp.zeros((TM4, LANE), F32)
    for kk in range(TOP_K):
        rk = jnp.sum(jnp.where(lane == idxs[kk], earlier, 0.0), axis=-1, keepdims=True)
        ti = jnp.where(lane == kk, idxs[kk], ti)
        tr = jnp.where(lane == kk, rk.astype(jnp.int32), tr)
        tw = jnp.where(lane == kk, es[kk] / tot, tw)
    ti_ref[rs, :] = ti
    tr_ref[rs, :] = tr
    tw_ref[rs, :] = tw
    tc_ref[hv] = jnp.sum(chosen, axis=0, keepdims=True).astype(jnp.int32)


def _out_projection(a_lat, hf, hb, mo, x2, w_out_b, g_mlstm, gt1, sh2, sc2, g2, wr_both, br_p):
    full = lambda shape: pl.BlockSpec(shape, lambda i: (0, 0))
    rows = TILES_PER_STEP4 * TM4
    row_blk = lambda w: pl.BlockSpec((rows, w), lambda i: (i, 0))
    return pl.pallas_call(
        _outproj_kernel,
        grid=(T_LAT // rows,),
        in_specs=[
            row_blk(ATT_WIDTH), row_blk(M_WIDTH), row_blk(M_WIDTH), row_blk(M_WIDTH),
            row_blk(D_MODEL), full((ATT_WIDTH + M_WIDTH, D_MODEL)), full((1, M_WIDTH)),
            full((MOD_ROWS, D_MODEL)), full((MOD_ROWS, D_MODEL)), full((MOD_ROWS, D_MODEL)),
            full((1, D_MODEL)), full((D_MODEL, 2 * LANE)), full((1, LANE)),
        ],
        out_specs=[row_blk(D_MODEL), row_blk(D_MODEL), row_blk(LANE), row_blk(LANE), row_blk(LANE),
                   pl.BlockSpec((TILES_PER_STEP4, 1, LANE), lambda i: (i, 0, 0))],
        out_shape=[
            jax.ShapeDtypeStruct((T_LAT, D_MODEL), F32),
            jax.ShapeDtypeStruct((T_LAT, D_MODEL), F32),
            jax.ShapeDtypeStruct((T_LAT, LANE), jnp.int32),
            jax.ShapeDtypeStruct((T_LAT, LANE), jnp.int32),
            jax.ShapeDtypeStruct((T_LAT, LANE), F32),
            jax.ShapeDtypeStruct((T_LAT // TM4, 1, LANE), jnp.int32),
        ],
        compiler_params=_params(1),
        name="out_proj_router",
    )(a_lat, hf, hb, mo, x2, w_out_b, g_mlstm, gt1, sh2, sc2, g2, wr_both, br_p)


MB = 128
N_ASSIGN = T_LAT * TOP_K
N_BLOCKS = N_ASSIGN // MB + N_EXPERTS
P_ROWS = N_BLOCKS * MB
N_PADS = P_ROWS - N_ASSIGN
SB = 9
W_MAX = N_EXPERTS + (N_BLOCKS - N_EXPERTS) // SB
XROWS = SB * MB


def _routing_tables(top_idx, tile_rank, tile_counts):
    n_tiles = tile_counts.shape[0]
    counts = jnp.sum(tile_counts, axis=0)
    tile_base = jnp.cumsum(tile_counts, axis=0) - tile_counts
    nblk = (counts + MB - 1) // MB
    blk_start = jnp.cumsum(nblk) - nblk
    n_valid_blk = jnp.sum(nblk)
    offs = blk_start[None, :] * MB + tile_base
    e4 = top_idx.reshape(n_tiles, -1, TOP_K, 1)
    hit = e4 == jnp.arange(N_EXPERTS, dtype=jnp.int32)
    dest = jnp.sum(jnp.where(hit, offs[:, None, None, :], 0), axis=-1)
    dest = (dest.reshape(T_LAT, TOP_K) + tile_rank).reshape(N_ASSIGN).astype(jnp.int32)
    nwork = (nblk + SB - 1) // SB
    wend = jnp.cumsum(nwork)
    wstart = wend - nwork
    n_items = wend[-1]
    wid = jnp.arange(W_MAX, dtype=jnp.int32)
    valid = wid < n_items
    we = jnp.minimum(jnp.sum((wend[None, :] <= wid[:, None]).astype(jnp.int32), axis=1),
                     N_EXPERTS - 1)
    local = wid - wstart[we]
    work_blk = jnp.where(valid, blk_start[we] + local * SB, 0)
    work_n = jnp.where(valid, jnp.clip(nblk[we] - local * SB, 0, SB), 0)
    last_e = we[jnp.maximum(n_items - 1, 0)]
    work_e = jnp.where(valid, we, last_e)
    i32 = lambda a: a.astype(jnp.int32)
    return (dest, i32(counts), i32(blk_start), i32(nblk), i32(n_valid_blk)[None],
            i32(work_e), i32(work_blk), i32(work_n))


TMD = 256
ROW_GROUP = 8


def _dispatch_kernel(dest_ref, cnt_ref, bs_ref, nb_ref, nvb_ref, h2_ref, x_hbm, zrow, sem, psem):
    i = pl.program_id(0)

    @pl.when(i == 0)
    def _():
        zrow[...] = jnp.zeros_like(zrow)

        def zero_row(r, carry):
            pltpu.make_async_copy(zrow.at[pl.ds(0, 1)], x_hbm.at[pl.ds(r, 1)], psem).start()
            return carry

        def expert(e, carry):
            first = bs_ref[e] * MB
            lax.fori_loop(first + cnt_ref[e], first + nb_ref[e] * MB, zero_row, 0)
            return carry
        lax.fori_loop(0, N_EXPERTS, expert, 0)
        lax.fori_loop(nvb_ref[0] * MB, P_ROWS, zero_row, 0)

    base = i * (TMD * TOP_K)

    def issue(tg, carry):
        t0 = pl.multiple_of(tg * ROW_GROUP, ROW_GROUP)
        for j in range(ROW_GROUP):
            for kk in range(TOP_K):
                d = dest_ref[base + (t0 + j) * TOP_K + kk]
                pltpu.make_async_copy(h2_ref.at[pl.ds(t0 + j, 1)], x_hbm.at[pl.ds(d, 1)],
                                      sem).start(priority=kk % 2)
        return carry
    lax.fori_loop(0, TMD // ROW_GROUP, issue, 0)
    for kk in range(TOP_K):
        pltpu.make_async_copy(h2_ref, x_hbm.at[pl.ds(0, TMD)], sem).wait()

    @pl.when(i == 0)
    def _():
        for _ in range(N_PADS // TMD):
            pltpu.make_async_copy(h2_ref, x_hbm.at[pl.ds(0, TMD)], psem).wait()


def _dispatch(h2, dest, counts, blk_start, nblk, n_valid_blk):
    grid_spec = pltpu.PrefetchScalarGridSpec(
        num_scalar_prefetch=5,
        grid=(T_LAT // TMD,),
        in_specs=[pl.BlockSpec((TMD, D_MODEL), lambda i, *_: (i, 0))],
        out_specs=pl.BlockSpec(memory_space=pl.ANY),
        scratch_shapes=[
            pltpu.VMEM((8, D_MODEL), F32),
            pltpu.SemaphoreType.DMA(()),
            pltpu.SemaphoreType.DMA(()),
        ],
    )
    return pl.pallas_call(
        _dispatch_kernel,
        grid_spec=grid_spec,
        out_shape=jax.ShapeDtypeStruct((P_ROWS, D_MODEL), F32),
        compiler_params=_params(1),
        name="moe_dispatch",
    )(dest, counts, blk_start, nblk, n_valid_blk, h2)


TF = 512
NF = D_FF // TF
BIG_BLKS = 4
TF2 = TF // 2
W_PRIORITY = 1
W_AHEAD = 2
W_SLOTS = W_AHEAD + 1
MOE_VMEM_LIMIT = 62 * 1024 * 1024


def _moe_kernel(we_ref, wb_ref, wn_ref, nvb_ref,
                x_hbm, w1_hbm, w2_hbm, b1g_ref, b1u_ref, b2_ref,
                y_hbm, xbuf, xb16, acc, w1g_buf, w1u_buf, w2_buf, xsem, ysem, wsem):
    w = pl.program_id(0)
    f = pl.program_id(1)
    nsub = wn_ref[w]
    r0 = wb_ref[w] * MB

    step = w * NF + f
    wslot = step % W_SLOTS

    def w_copies(item, ff, slot_):
        e = we_ref[item]
        sem_ = wsem.at[slot_]
        gate = pltpu.make_async_copy(w1_hbm.at[e, :, pl.ds(ff * TF, TF)], w1g_buf.at[slot_], sem_)
        up = pltpu.make_async_copy(w1_hbm.at[e, :, pl.ds(D_FF + ff * TF, TF)],
                                   w1u_buf.at[slot_], sem_)
        dn_a = pltpu.make_async_copy(w2_hbm.at[e, pl.ds(ff * TF, TF2), :],
                                     w2_buf.at[slot_, 0:TF2], sem_)
        dn_b = pltpu.make_async_copy(w2_hbm.at[e, pl.ds(ff * TF + TF2, TF2), :],
                                     w2_buf.at[slot_, TF2:TF], sem_)
        return gate, dn_a, up, dn_b

    def start_w(item, ff, slot_):
        gate, dn_a, up, dn_b = w_copies(item, ff, slot_)
        gate.start(priority=0)
        dn_a.start(priority=0)
        up.start(priority=1)
        dn_b.start(priority=1)

    @pl.when(jnp.logical_and(step == 0, nsub > 0))
    def _():
        for s in range(W_AHEAD):
            start_w(0, s, s)

    last_f = f == NF - 1
    nxt_item = w + (f + W_AHEAD) // NF
    nxt_f = (f + W_AHEAD) % NF

    @pl.when(jnp.logical_and(nxt_item < W_MAX, wn_ref[jnp.minimum(nxt_item, W_MAX - 1)] > 0))
    def _():
        start_w(nxt_item, nxt_f, (step + W_AHEAD) % W_SLOTS)

    @pl.when(nsub > 0)
    def _():
        for cp in w_copies(w, f, wslot):
            cp.wait()

    def x_copy(item, s):
        src = x_hbm.at[pl.ds(wb_ref[item] * MB + s * MB, MB)]
        return pltpu.make_async_copy(src, xbuf.at[pl.ds(s * MB, MB)], xsem)

    def start_x(item):
        def body(s, carry):
            x_copy(item, s).start()
            return carry
        lax.fori_loop(0, wn_ref[item], body, 0)

    @pl.when(jnp.logical_and(w == 0, f == 0))
    def _():
        start_x(0)

    @pl.when(f == 0)
    def _():
        def wait(s, carry):
            x_copy(w, s).wait()
            return carry
        lax.fori_loop(0, nsub, wait, 0)

        def body(s, carry):
            rs = pl.ds(pl.multiple_of(s * MB, MB), MB)
            xb16[rs, :] = xbuf[rs, :].astype(BF16)
            acc[rs, :] = jnp.broadcast_to(b2_ref[...], (MB, D_MODEL))
            return carry
        lax.fori_loop(0, nsub, body, 0)

    @pl.when(jnp.logical_and(f == 1, w + 1 < W_MAX))
    def _():
        start_x(w + 1)

    def y_copy(s):
        rs = pl.ds(pl.multiple_of(s * MB, MB), MB)
        return pltpu.make_async_copy(acc.at[rs], y_hbm.at[pl.ds(r0 + s * MB, MB)], ysem)

    def chunk(row, m):
        rs = pl.ds(row, m)
        xb = xb16[rs, :]
        gate = jnp.dot(xb, w1g_buf[wslot], preferred_element_type=F32) + b1g_ref[...]
        up = jnp.dot(xb, w1u_buf[wslot], preferred_element_type=F32) + b1u_ref[...]
        gate = jnp.minimum(gate, SWIGLU_LIMIT)
        up = jnp.clip(up, -SWIGLU_LIMIT, SWIGLU_LIMIT)
        glu = gate / (1.0 + jnp.exp(-SWIGLU_ALPHA * gate))
        act = (up + 1.0) * glu
        acc[rs, :] += jnp.dot(act, w2_buf[wslot], preferred_element_type=F32)

    def write_back(first_blk, n_blk):
        @pl.when(last_f)
        def _():
            for s in range(n_blk):
                y_copy(first_blk + s).start()

    nbig = nsub // BIG_BLKS

    def big(c, carry):
        chunk(pl.multiple_of(c * (BIG_BLKS * MB), BIG_BLKS * MB), BIG_BLKS * MB)
        write_back(c * BIG_BLKS, BIG_BLKS)
        return carry
    lax.fori_loop(0, nbig, big, 0)

    done = nbig * BIG_BLKS
    size = BIG_BLKS // 2
    while size >= 1:
        has = ((nsub - done) // size) % 2 == 1
        first = done + ((nsub - done) // (2 * size)) * (2 * size)

        @pl.when(has)
        def _(first=first, size=size):
            chunk(pl.multiple_of(first * MB, size * MB), size * MB)
            write_back(first, size)
        size //= 2

    @pl.when(last_f)
    def _():
        def wait(s, carry):
            y_copy(s).wait()
            return carry
        lax.fori_loop(0, nsub, wait, 0)

    @pl.when(jnp.logical_and(w == W_MAX - 1, f == NF - 1))
    def _():
        acc[0:MB, :] = jnp.zeros((MB, D_MODEL), F32)

        def tail_copy(bk):
            return pltpu.make_async_copy(acc.at[0:MB], y_hbm.at[pl.ds(bk * MB, MB)], ysem)

        def start(bk, carry):
            tail_copy(bk).start()
            return carry
        lax.fori_loop(nvb_ref[0], N_BLOCKS, start, 0)

        def wait(bk, carry):
            tail_copy(bk).wait()
            return carry
        lax.fori_loop(nvb_ref[0], N_BLOCKS, wait, 0)


def _moe(x_sorted, w1, b1, w2, b2, work_e, work_blk, work_n, n_valid_blk):
    b1r = b1.reshape(N_EXPERTS, 1, 2 * D_FF)
    b2r = b2.reshape(N_EXPERTS, 1, D_MODEL)
    grid_spec = pltpu.PrefetchScalarGridSpec(
        num_scalar_prefetch=4,
        grid=(W_MAX, NF),
        in_specs=[
            pl.BlockSpec(memory_space=pl.ANY),
            pl.BlockSpec(memory_space=pl.ANY),
            pl.BlockSpec(memory_space=pl.ANY),
            pl.BlockSpec((None, 1, TF), lambda w, f, we, *_: (we[w], 0, f)),
            pl.BlockSpec((None, 1, TF), lambda w, f, we, *_: (we[w], 0, NF + f)),
            pl.BlockSpec((None, 1, D_MODEL), lambda w, f, we, *_: (we[w], 0, 0)),
        ],
        out_specs=pl.BlockSpec(memory_space=pl.ANY),
        scratch_shapes=[
            pltpu.VMEM((XROWS, D_MODEL), F32),
            pltpu.VMEM((XROWS, D_MODEL), BF16),
            pltpu.VMEM((XROWS, D_MODEL), F32),
            pltpu.VMEM((W_SLOTS, D_MODEL, TF), F32),
            pltpu.VMEM((W_SLOTS, D_MODEL, TF), F32),
            pltpu.VMEM((W_SLOTS, TF, D_MODEL), F32),
            pltpu.SemaphoreType.DMA(()),
            pltpu.SemaphoreType.DMA(()),
            pltpu.SemaphoreType.DMA((W_SLOTS,)),
        ],
    )
    return pl.pallas_call(
        _moe_kernel,
        grid_spec=grid_spec,
        out_shape=jax.ShapeDtypeStruct((P_ROWS, D_MODEL), F32),
        compiler_params=pltpu.CompilerParams(
            dimension_semantics=("arbitrary", "arbitrary"), vmem_limit_bytes=MOE_VMEM_LIMIT),
        name="moe_experts",
    )(work_e, work_blk, work_n, n_valid_blk, x_sorted, w1, w2, b1r, b1r, b2r)


TM7 = 256
N_TILES7 = T_LAT // TM7
TILES_PER_SEQ7 = SEQ // TM7


def _final_kernel(pos_ref, x1_ref, tw_ref, gt2_ref, gf_ref, y_hbm, o_ref, ybuf, sem):
    i = pl.program_id(0)
    slot = i % 2

    def start_gather(tile, slot_):
        base = tile * (TM7 * TOP_K)

        def body(tg, carry):
            t0 = pl.multiple_of(tg * ROW_GROUP, ROW_GROUP)
            for j in range(ROW_GROUP):
                for kk in range(TOP_K):
                    p = pos_ref[base + (t0 + j) * TOP_K + kk]
                    pltpu.make_async_copy(y_hbm.at[pl.ds(p, 1)],
                                          ybuf.at[slot_, kk, pl.ds(t0 + j, 1)],
                                          sem.at[slot_]).start(priority=kk % 2)
            return carry
        lax.fori_loop(0, TM7 // ROW_GROUP, body, 0)

    @pl.when(i == 0)
    def _():
        start_gather(0, 0)

    @pl.when(i + 1 < N_TILES7)
    def _():
        start_gather(i + 1, 1 - slot)

    for kk in range(TOP_K):
        pltpu.make_async_copy(y_hbm.at[pl.ds(0, TM7)], ybuf.at[slot, kk], sem.at[slot]).wait()

    row = i // TILES_PER_SEQ7
    tw = tw_ref[...]
    mix = jnp.zeros((TM7, D_MODEL), F32)
    for kk in range(TOP_K):
        mix += ybuf[slot, kk] * tw[:, kk:kk + 1]
    xo = x1_ref[...] + gt2_ref[pl.ds(row, 1), :] * mix
    o_ref[...] = _rms(xo) * gf_ref[...]


def _final(dest, x1, top_w, gt2, g_final, y_sorted):
    grid_spec = pltpu.PrefetchScalarGridSpec(
        num_scalar_prefetch=1,
        grid=(N_TILES7,),
        in_specs=[
            pl.BlockSpec((TM7, D_MODEL), lambda i, *_: (i, 0)),
            pl.BlockSpec((TM7, LANE), lambda i, *_: (i, 0)),
            pl.BlockSpec((MOD_ROWS, D_MODEL), lambda i, *_: (0, 0)),
            pl.BlockSpec((1, D_MODEL), lambda i, *_: (0, 0)),
            pl.BlockSpec(memory_space=pl.ANY),
        ],
        out_specs=pl.BlockSpec((TM7, D_MODEL), lambda i, *_: (i, 0)),
        scratch_shapes=[
            pltpu.VMEM((2, TOP_K, TM7, D_MODEL), F32),
            pltpu.SemaphoreType.DMA((2,)),
        ],
    )
    return pl.pallas_call(
        _final_kernel,
        grid_spec=grid_spec,
        out_shape=jax.ShapeDtypeStruct((T_LAT, D_MODEL), F32),
        compiler_params=_params(1),
        name="combine_final_norm",
    )(dest, x1, top_w, gt2, g_final, y_sorted)


def _rope_tables():
    rows = SEQ // GRID_W
    row = jnp.repeat(jnp.arange(rows, dtype=F32), GRID_W)
    col = jnp.tile(jnp.arange(GRID_W, dtype=F32), rows)
    inv = ROPE_THETA ** (-jnp.arange(0, AXIS_DIM, 2, dtype=F32) / AXIS_DIM)
    ang = jnp.concatenate([row[:, None] * inv, col[:, None] * inv], axis=-1)
    cos = jnp.repeat(jnp.cos(ang), 2, axis=-1)
    sin = jnp.repeat(jnp.sin(ang), 2, axis=-1)
    sign = jnp.tile(jnp.array([-1.0, 1.0], F32), HEAD_DIM // 2)
    cos_t = jnp.concatenate([cos, jnp.ones((ROWS1, HEAD_DIM), F32)], axis=0)
    sin_t = jnp.concatenate([sin * sign, jnp.zeros((ROWS1, HEAD_DIM), F32)], axis=0)
    return cos_t, sin_t


def kernel(x, c, ctx, c_ctx, w_mod, b_mod, g_norm1, w_in, b_in, g_q, g_k, g_mlstm, w_out,
           g_norm2, w_router, b_router, w1, b1, w2, b2, g_final):
    x2 = x.reshape(T_LAT, D_MODEL)
    ctx2 = ctx.reshape(T_CTX, D_MODEL)
    cos_t, sin_t = _rope_tables()
    assert w_mod.shape[0] == 1
    l = 0
    c_rows = jnp.concatenate(
        [c, c_ctx[None, :], jnp.zeros((MOD_ROWS - BATCH - 1, D_MODEL), F32)], axis=0)
    mod = _modulation(c_rows, w_mod[l], b_mod[l][None, :])
    sh1, sc1, gt1, sh2, sc2, gt2 = [mod[:, k * D_MODEL:(k + 1) * D_MODEL]
                                    for k in range(N_ADALN)]
    pad = IN_COLS_PAD - IN_COLS
    w_main = w_in[l].astype(BF16)
    w_gate = jnp.pad(w_in[l][:, C_G:], ((0, 0), (0, pad))).astype(BF16)
    b_main = b_in[l][None, :C_G]
    b_gate = jnp.pad(b_in[l][C_G:], (0, pad))[None, :]
    q, k, v, mq, mkt, mv, mo, gates = _in_projection(
        x2, ctx2, sh1, sc1, g_norm1[l][None, :], w_main, b_main, w_gate, b_gate,
        g_q[l][None, :], g_k[l][None, :], cos_t, sin_t)
    a_lat = _attention(q, k, v)
    hf, hb = _mlstm(mq, mkt, mv, gates)
    wr_p = jnp.pad(w_router[l], ((0, 0), (0, LANE - N_EXPERTS)))
    wr_hi = wr_p.astype(BF16)
    wr_lo = (wr_p - wr_hi.astype(F32)).astype(BF16)
    br_p = jnp.pad(b_router[l], (0, LANE - N_EXPERTS), constant_values=NEG_BIG)[None, :]
    x1, h2, top_i, top_r, top_w, tile_cnt = _out_projection(
        a_lat, hf, hb, mo, x2, w_out[l].astype(BF16), g_mlstm[l][None, :],
        gt1, sh2, sc2, g_norm2[l][None, :], jnp.concatenate([wr_hi, wr_lo], axis=1), br_p)
    dest, counts, blk_start, nblk, n_valid_blk, work_e, work_blk, work_n = _routing_tables(
        top_i[:, :TOP_K], top_r[:, :TOP_K], tile_cnt[:, 0, :N_EXPERTS])
    x_sorted = _dispatch(h2, dest, counts, blk_start, nblk, n_valid_blk)
    y_sorted = _moe(x_sorted, w1[l], b1[l], w2[l], b2[l], work_e, work_blk, work_n, n_valid_blk)
    out = _final(dest, x1, top_w, gt2, g_final[None, :], y_sorted)
    return out.reshape(BATCH, SEQ, D_MODEL)
```

```python
import jax
import jax.numpy as jnp
from jax import lax
from jax.experimental import pallas as pl
from jax.experimental.pallas import tpu as pltpu

F32 = jnp.float32
BF16 = jnp.bfloat16

D_MODEL = 2048
BATCH = 2
SEQ = 4096
CTX_LEN = 256
GRID_W = 64
N_ADALN = 6
EPS = 1e-6
N_HEADS = 8
KV_HEADS = 2
GROUP = N_HEADS // KV_HEADS
HEAD_DIM = 128
AXIS_DIM = HEAD_DIM // 2
ROPE_THETA = 10000.0
ATT_WIDTH = N_HEADS * HEAD_DIM
ATT_SCALE = HEAD_DIM ** -0.5
M_HEADS = 4
M_DK = 128
M_DV = 256
M_CHUNK = 128
M_WIDTH = M_HEADS * M_DV
N_EXPERTS = 32
TOP_K = 4
D_FF = D_MODEL
SWIGLU_LIMIT = 7.0
SWIGLU_ALPHA = 1.702

T_LAT = BATCH * SEQ
T_CTX = BATCH * CTX_LEN
T_ALL = T_LAT + T_CTX
KV_W = KV_HEADS * HEAD_DIM
MQK_W = M_HEADS * M_DK
LANE = 128

C_Q = 0
C_K = C_Q + ATT_WIDTH
C_V = C_K + KV_W
C_MQ = C_V + KV_W
C_MK = C_MQ + MQK_W
C_MV = C_MK + MQK_W
C_MO = C_MV + M_WIDTH
C_G = C_MO + M_WIDTH
IN_COLS = C_G + 4 * M_HEADS
IN_COLS_PAD = C_G + LANE

VMEM_LIMIT = 56 * 1024 * 1024


def _params(n_axes):
    return pltpu.CompilerParams(
        dimension_semantics=("arbitrary",) * n_axes, vmem_limit_bytes=VMEM_LIMIT)


def _rms(x):
    return x * lax.rsqrt(jnp.mean(x * x, axis=-1, keepdims=True) + EPS)


MOD_ROWS = 8
MOD_TN = 1024


def _mod_kernel(c_ref, w_ref, b_ref, o_ref):
    c = c_ref[...]
    s = c / (1.0 + jnp.exp(-c))
    o_ref[...] = jnp.dot(s, w_ref[...], preferred_element_type=F32) + b_ref[...]


def _modulation(c_rows, w_mod, b_mod):
    n = w_mod.shape[1]
    return pl.pallas_call(
        _mod_kernel,
        grid=(n // MOD_TN,),
        in_specs=[
            pl.BlockSpec((MOD_ROWS, D_MODEL), lambda j: (0, 0)),
            pl.BlockSpec((D_MODEL, MOD_TN), lambda j: (0, j)),
            pl.BlockSpec((1, MOD_TN), lambda j: (0, j)),
        ],
        out_specs=pl.BlockSpec((MOD_ROWS, MOD_TN), lambda j: (0, j)),
        out_shape=jax.ShapeDtypeStruct((MOD_ROWS, n), F32),
        compiler_params=_params(1),
        name="adaln_mod",
    )(c_rows, w_mod, b_mod)


TM1 = 256
N_LAT_TILES1 = T_LAT // TM1
N_TILES1 = T_ALL // TM1
TILES_PER_SEQ1 = SEQ // TM1
PCH = 512
LOG2E = 1.4426950408889634
Q_SCALE = ATT_SCALE * LOG2E
V_AUG = 2 * HEAD_DIM


STEP_TILES1 = 2
ROWS1 = STEP_TILES1 * TM1
N_STEPS1 = N_TILES1 // STEP_TILES1
LAT_STEPS1 = N_LAT_TILES1 // STEP_TILES1
STEPS_PER_SEQ1 = TILES_PER_SEQ1 // STEP_TILES1
assert T_CTX == ROWS1


def _inproj_kernel(x0_ref, xb_ref, xn_ref, ctx_ref, sh_ref, sc_ref, g1_ref, w_ref, b_ref,
                   wg_ref, bg_ref, gq_ref, gk_ref, cos_ref, sin_ref,
                   q_ref, k_ref, v_ref, mq_ref, mkt_ref, mv_ref, mo_ref, g_ref, ha_scr, hb_scr):
    i = pl.program_id(0)

    def norm_mod(xv, tile):
        row = jnp.where(tile < N_LAT_TILES1, tile // TILES_PER_SEQ1, BATCH)
        y = (_rms(xv) * g1_ref[...]) * (1.0 + sc_ref[pl.ds(row, 1), :]) + sh_ref[pl.ds(row, 1), :]
        return y.astype(BF16)

    lane = lax.broadcasted_iota(jnp.int32, (TM1, HEAD_DIM), 1)
    even = (lane % 2) == 0

    def project(h_scr, half):
        h = h_scr[...]
        rs = slice(half * TM1, (half + 1) * TM1)
        cos_f = cos_ref[rs, :]
        sin_s = sin_ref[rs, :]

        def proj(c0, c1):
            return jnp.dot(h, w_ref[:, c0:c1], preferred_element_type=F32) + b_ref[:, c0:c1]

        def head_norm_rope(a, g):
            y = _rms(a) * g
            nxt = pltpu.roll(y, HEAD_DIM - 1, 1)
            prv = pltpu.roll(y, 1, 1)
            return y * cos_f + jnp.where(even, nxt, prv) * sin_s

        for c0 in range(C_Q, C_K, PCH):
            acc = proj(c0, c0 + PCH)
            for hh in range(PCH // HEAD_DIM):
                a = acc[:, hh * HEAD_DIM:(hh + 1) * HEAD_DIM]
                r = head_norm_rope(a, gq_ref[...]) * Q_SCALE
                q_ref[rs, c0 + hh * HEAD_DIM:c0 + (hh + 1) * HEAD_DIM] = r.astype(BF16)
        acc = proj(C_K, C_MQ)
        for hh in range(KV_HEADS):
            a = acc[:, hh * HEAD_DIM:(hh + 1) * HEAD_DIM]
            k_ref[rs, hh * HEAD_DIM:(hh + 1) * HEAD_DIM] = (
                head_norm_rope(a, gk_ref[...]).astype(BF16))
        for hh in range(KV_HEADS):
            v_ref[rs, hh * V_AUG:hh * V_AUG + HEAD_DIM] = (
                acc[:, KV_W + hh * HEAD_DIM:KV_W + (hh + 1) * HEAD_DIM].astype(BF16))
            v_ref[rs, hh * V_AUG + HEAD_DIM:(hh + 1) * V_AUG] = jnp.ones((TM1, HEAD_DIM), BF16)
        mq_ref[rs, :] = proj(C_MQ, C_MK).astype(BF16)
        mk = proj(C_MK, C_MV) * (M_DK ** -0.5)
        mkt_ref[:, rs] = mk.T.astype(BF16)
        for c0 in range(C_MV, C_MO, PCH):
            mv_ref[rs, c0 - C_MV:c0 - C_MV + PCH] = proj(c0, c0 + PCH).astype(BF16)
        for c0 in range(C_MO, C_G, PCH):
            mo_ref[rs, c0 - C_MO:c0 - C_MO + PCH] = proj(c0, c0 + PCH)
        g_ref[rs, :] = jnp.dot(h, wg_ref[...], preferred_element_type=F32) + bg_ref[...]

    @pl.when(i == 0)
    def _():
        ha_scr[...] = norm_mod(x0_ref[...], 0)

    tile_b = STEP_TILES1 * i + 1
    x_b = jnp.where(tile_b < N_LAT_TILES1, xb_ref[...], ctx_ref[TM1:ROWS1, :])
    hb_scr[...] = norm_mod(x_b, tile_b)
    project(ha_scr, 0)
    tile_n = STEP_TILES1 * i + 2
    x_n = jnp.where(tile_n < N_LAT_TILES1, xn_ref[...], ctx_ref[0:TM1, :])
    project(hb_scr, 1)
    ha_scr[...] = norm_mod(x_n, tile_n)


def _in_projection(x2, ctx2, sh1, sc1, g1, w_main, b_main, w_gate, b_gate, g_q, g_k, cos_t, sin_t):
    last_lat = N_LAT_TILES1 - 1
    rope_idx = lambda i: (jnp.where(i < LAT_STEPS1, i % STEPS_PER_SEQ1, STEPS_PER_SEQ1), 0)
    full = lambda shape: pl.BlockSpec(shape, lambda i: (0, 0))
    row_blk = lambda w: pl.BlockSpec((ROWS1, w), lambda i: (i, 0))
    out_shapes = [
        jax.ShapeDtypeStruct((T_ALL, ATT_WIDTH), BF16),
        jax.ShapeDtypeStruct((T_ALL, KV_W), BF16),
        jax.ShapeDtypeStruct((T_ALL, KV_HEADS * V_AUG), BF16),
        jax.ShapeDtypeStruct((T_ALL, MQK_W), BF16),
        jax.ShapeDtypeStruct((MQK_W, T_ALL), BF16),
        jax.ShapeDtypeStruct((T_ALL, M_WIDTH), BF16),
        jax.ShapeDtypeStruct((T_ALL, M_WIDTH), F32),
        jax.ShapeDtypeStruct((T_ALL, LANE), F32),
    ]
    out_specs = [
        row_blk(ATT_WIDTH), row_blk(KV_W), row_blk(KV_HEADS * V_AUG), row_blk(MQK_W),
        pl.BlockSpec((MQK_W, ROWS1), lambda i: (0, i)),
        row_blk(M_WIDTH), row_blk(M_WIDTH), row_blk(LANE),
    ]
    x_tile = lambda idx, **kw: pl.BlockSpec((TM1, D_MODEL), idx, **kw)
    return pl.pallas_call(
        _inproj_kernel,
        grid=(N_STEPS1,),
        in_specs=[
            x_tile(lambda i: (0, 0), pipeline_mode=pl.Buffered(1)),
            x_tile(lambda i: (jnp.minimum(STEP_TILES1 * i + 1, last_lat), 0)),
            x_tile(lambda i: (jnp.minimum(STEP_TILES1 * i + 2, last_lat), 0)),
            full((T_CTX, D_MODEL)),
            full((MOD_ROWS, D_MODEL)), full((MOD_ROWS, D_MODEL)), full((1, D_MODEL)),
            pl.BlockSpec((D_MODEL, C_G), lambda i: (0, 0), pipeline_mode=pl.Buffered(1)),
            full((1, C_G)), full((D_MODEL, LANE)), full((1, LANE)),
            full((1, HEAD_DIM)), full((1, HEAD_DIM)),
            pl.BlockSpec((ROWS1, HEAD_DIM), rope_idx),
            pl.BlockSpec((ROWS1, HEAD_DIM), rope_idx),
        ],
        out_specs=out_specs,
        out_shape=out_shapes,
        scratch_shapes=[pltpu.VMEM((TM1, D_MODEL), BF16), pltpu.VMEM((TM1, D_MODEL), BF16)],
        compiler_params=_params(1),
        name="in_proj",
    )(x2, x2, x2, ctx2, sh1, sc1, g1, w_main, b_main, w_gate, b_gate, g_q, g_k, cos_t, sin_t)


TQ = 512
CK = 512
GQ_W = GROUP * HEAD_DIM


def _attn_kernel(q_ref, kl_ref, vl_ref, kc_ref, vc_ref, o_ref):
    chunks = [(kl_ref, vl_ref, c * CK, CK) for c in range(SEQ // CK)]
    chunks.append((kc_ref, vc_ref, 0, CTX_LEN))
    for g in range(GROUP):
        qh = q_ref[:, g * HEAD_DIM:(g + 1) * HEAD_DIM]
        m = jnp.full((TQ, 1), -jnp.inf, F32)
        acc = jnp.zeros((TQ, V_AUG), F32)
        for kr, vr, st, sz in chunks:
            s = lax.dot_general(qh, kr[st:st + sz, :], (((1,), (1,)), ((), ())),
                                preferred_element_type=F32)
            m_new = jnp.maximum(m, jnp.max(s, axis=-1, keepdims=True))
            p = jnp.exp2(s - m_new)
            alpha = jnp.exp2(m - m_new)
            acc = alpha * acc + jnp.dot(p.astype(BF16), vr[st:st + sz, :],
                                        preferred_element_type=F32)
            m = m_new
        o_ref[:, g * HEAD_DIM:(g + 1) * HEAD_DIM] = (
            acc[:, :HEAD_DIM] / acc[:, HEAD_DIM:]).astype(BF16)


def _attention(q, k, v):
    nq = SEQ // TQ
    ctx_blk0 = T_LAT // CTX_LEN
    lat_k = pl.BlockSpec((SEQ, HEAD_DIM), lambda b, h, i: (b, h))
    lat_v = pl.BlockSpec((SEQ, V_AUG), lambda b, h, i: (b, h))
    ctx_k = pl.BlockSpec((CTX_LEN, HEAD_DIM), lambda b, h, i: (ctx_blk0 + b, h))
    ctx_v = pl.BlockSpec((CTX_LEN, V_AUG), lambda b, h, i: (ctx_blk0 + b, h))
    q_blk = pl.BlockSpec((TQ, GQ_W), lambda b, h, i: (b * nq + i, h))
    return pl.pallas_call(
        _attn_kernel,
        grid=(BATCH, KV_HEADS, nq),
        in_specs=[q_blk, lat_k, lat_v, ctx_k, ctx_v],
        out_specs=q_blk,
        out_shape=jax.ShapeDtypeStruct((T_LAT, ATT_WIDTH), BF16),
        compiler_params=_params(3),
        name="gqa_attention",
    )(q, k, v, k, v)


N_CHUNKS = SEQ // M_CHUNK
CTX_CHUNKS = CTX_LEN // M_CHUNK
N_CHAINS = 2 * M_HEADS


def _log_sigmoid(x):
    return jnp.minimum(x, 0.0) - jnp.log(1.0 + jnp.exp(-jnp.abs(x)))


def _split3(x):
    hi = x.astype(BF16)
    r1 = x - hi.astype(F32)
    mid = r1.astype(BF16)
    lo = (r1 - mid.astype(F32)).astype(BF16)
    return hi, mid, lo


def _mlstm_kernel(qf_ref, qb_ref, qc_ref, kf_ref, kb_ref, kc_ref, vf_ref, vb_ref, vc_ref,
                  gf_ref, gb_ref, gc_ref, hf_ref, hb_ref, *state):
    c_scr = state[0:N_CHAINS]
    n_scr = state[N_CHAINS:2 * N_CHAINS]
    m_scr = state[2 * N_CHAINS:3 * N_CHAINS]
    j = pl.program_id(1)
    ri = lax.broadcasted_iota(jnp.int32, (M_CHUNK, M_CHUNK), 0)
    ci = lax.broadcasted_iota(jnp.int32, (M_CHUNK, M_CHUNK), 1)
    lower = ci <= ri
    upper = ci >= ri
    tri_l = jnp.where(lower, 1.0, 0.0).astype(BF16)
    tri_u = jnp.where(upper, 1.0, 0.0).astype(BF16)

    def gate_tables(g, rev):
        gt = g.T
        mcol, mrow = (tri_u, tri_l) if rev else (tri_l, tri_u)
        c_hi, c_mid, c_lo = _split3(_log_sigmoid(g))
        b3 = jnp.dot(mcol, jnp.concatenate([c_hi, c_mid, c_lo], axis=1),
                     preferred_element_type=F32)
        bcol = b3[:, :LANE] + b3[:, LANE:2 * LANE] + b3[:, 2 * LANE:]
        r_hi, r_mid, r_lo = _split3(_log_sigmoid(gt))
        r3 = jnp.dot(jnp.concatenate([r_hi, r_mid, r_lo], axis=0), mrow,
                     preferred_element_type=F32)
        brow = r3[:M_CHUNK] + r3[M_CHUNK:2 * M_CHUNK] + r3[2 * M_CHUNK:]
        return gt, bcol, brow

    def chain_step(chain, q, kt, v, tabs, rev, h_out):
        gt, bcol, brow = tabs
        icol = chain
        fcol = N_CHAINS + chain
        last = 0 if rev else M_CHUNK - 1
        b_col = bcol[:, fcol:fcol + 1]
        b_row = brow[fcol:fcol + 1, :]
        i_row = gt[icol:icol + 1, :]
        tot = brow[fcol:fcol + 1, last:last + 1]
        m_old = m_scr[chain][:1, :1]
        c_old = c_scr[chain][...]
        n_old = n_scr[chain][...]
        v_ones = jnp.concatenate([v, jnp.ones((M_CHUNK, LANE), BF16)], axis=1)
        c_row = i_row - b_row
        if h_out is not None:
            mask = upper if rev else lower
            mm = jnp.maximum(m_old, jnp.max(jnp.where(mask, c_row, -jnp.inf), axis=-1, keepdims=True))
            w_intra = jnp.exp(jnp.where(mask, c_row - mm, -jnp.inf))
            w_inter = jnp.exp(m_old - mm)
            s = jnp.dot(q, kt, preferred_element_type=F32) * w_intra
            qc = jnp.dot(q, c_old.astype(BF16), preferred_element_type=F32)
            qn = jnp.dot(q, n_old.astype(BF16), preferred_element_type=F32)[:, :1]
            sv = jnp.dot(s.astype(BF16), v_ones, preferred_element_type=F32)
            num = sv[:, :M_DV] + w_inter * qc
            den = sv[:, M_DV:M_DV + 1] + w_inter * qn
            h_out(num / jnp.maximum(jnp.abs(den), jnp.exp(-(b_col + mm))))
        g_row = tot + c_row
        m_new = jnp.maximum(tot + m_old, jnp.max(g_row, axis=-1, keepdims=True))
        w_state = jnp.exp(g_row - m_new)
        decay = jnp.exp(tot + m_old - m_new)
        kw = (kt.astype(F32) * w_state).astype(BF16)
        kv = jnp.dot(kw, v_ones, preferred_element_type=F32)
        c_scr[chain][...] = decay * c_old + kv[:, :M_DV]
        n_scr[chain][...] = decay * n_old + kv[:, M_DV:]
        m_scr[chain][...] = jnp.broadcast_to(m_new, (8, LANE))

    def run_chunk(q_ref, kt_ref, v_ref, g_ref, r0, rev, h_ref):
        tabs = gate_tables(g_ref[r0:r0 + M_CHUNK, :], rev)
        for hd in range(M_HEADS):
            chain = (M_HEADS if rev else 0) + hd
            q = q_ref[r0:r0 + M_CHUNK, hd * M_DK:(hd + 1) * M_DK]
            kt = kt_ref[hd * M_DK:(hd + 1) * M_DK, r0:r0 + M_CHUNK]
            v = v_ref[r0:r0 + M_CHUNK, hd * M_DV:(hd + 1) * M_DV]
            if h_ref is None:
                h_out = None
            else:
                def h_out(hv, hd=hd):
                    h_ref[:, hd * M_DV:(hd + 1) * M_DV] = hv
            chain_step(chain, q, kt, v, tabs, rev, h_out)

    @pl.when(j == 0)
    def _():
        for ref in state:
            ref[...] = jnp.zeros_like(ref)
        for cc in range(CTX_CHUNKS):
            run_chunk(qc_ref, kc_ref, vc_ref, gc_ref, cc * M_CHUNK, False, None)
            run_chunk(qc_ref, kc_ref, vc_ref, gc_ref, (CTX_CHUNKS - 1 - cc) * M_CHUNK, True, None)

    run_chunk(qf_ref, kf_ref, vf_ref, gf_ref, 0, False, hf_ref)
    run_chunk(qb_ref, kb_ref, vb_ref, gb_ref, 0, True, hb_ref)


def _mlstm(mq, mkt, mv, gates):
    ctx_blk0 = T_LAT // CTX_LEN
    fwd = lambda b, j: b * N_CHUNKS + j
    bwd = lambda b, j: b * N_CHUNKS + N_CHUNKS - 1 - j
    ctx = lambda b, j: ctx_blk0 + b

    def rows(width, tile, idx):
        return pl.BlockSpec((tile, width), lambda b, j: (idx(b, j), 0))

    def cols(tile, idx):
        return pl.BlockSpec((MQK_W, tile), lambda b, j: (0, idx(b, j)))

    h_shape = jax.ShapeDtypeStruct((T_LAT, M_WIDTH), F32)
    return pl.pallas_call(
        _mlstm_kernel,
        grid=(BATCH, N_CHUNKS),
        in_specs=[
            rows(MQK_W, M_CHUNK, fwd), rows(MQK_W, M_CHUNK, bwd), rows(MQK_W, CTX_LEN, ctx),
            cols(M_CHUNK, fwd), cols(M_CHUNK, bwd), cols(CTX_LEN, ctx),
            rows(M_WIDTH, M_CHUNK, fwd), rows(M_WIDTH, M_CHUNK, bwd), rows(M_WIDTH, CTX_LEN, ctx),
            rows(LANE, M_CHUNK, fwd), rows(LANE, M_CHUNK, bwd), rows(LANE, CTX_LEN, ctx),
        ],
        out_specs=[rows(M_WIDTH, M_CHUNK, fwd), rows(M_WIDTH, M_CHUNK, bwd)],
        out_shape=[h_shape, h_shape],
        scratch_shapes=([pltpu.VMEM((M_DK, M_DV), F32)] * N_CHAINS
                        + [pltpu.VMEM((M_DK, LANE), F32)] * N_CHAINS
                        + [pltpu.VMEM((8, LANE), F32)] * N_CHAINS),
        compiler_params=_params(2),
        name="mlstm_scan",
    )(mq, mq, mq, mkt, mkt, mkt, mv, mv, mv, gates, gates, gates)


TM4 = 256
TILES_PER_SEQ4 = SEQ // TM4
TILES_PER_STEP4 = 2
NEG_BIG = -1e30


def _outproj_kernel(a_ref, hf_ref, hb_ref, mo_ref, x_ref, w_ref, gm_ref, gt1_ref, sh2_ref,
                    sc2_ref, g2_ref, wr_ref, br_ref,
                    x1_ref, h2_ref, ti_ref, tr_ref, tw_ref, tc_ref):
    for hv in range(TILES_PER_STEP4):
        _outproj_tile(hv, a_ref, hf_ref, hb_ref, mo_ref, x_ref, w_ref, gm_ref, gt1_ref, sh2_ref,
                      sc2_ref, g2_ref, wr_ref, br_ref,
                      x1_ref, h2_ref, ti_ref, tr_ref, tw_ref, tc_ref)


def _outproj_tile(hv, a_ref, hf_ref, hb_ref, mo_ref, x_ref, w_ref, gm_ref, gt1_ref, sh2_ref,
                  sc2_ref, g2_ref, wr_ref, br_ref,
                  x1_ref, h2_ref, ti_ref, tr_ref, tw_ref, tc_ref):
    i = pl.program_id(0)
    row = (i * TILES_PER_STEP4 + hv) // TILES_PER_SEQ4
    rs = slice(hv * TM4, (hv + 1) * TM4)
    acc = jnp.dot(a_ref[rs, :], w_ref[0:ATT_WIDTH, :], preferred_element_type=F32)
    for hd in range(M_HEADS):
        sl = slice(hd * M_DV, (hd + 1) * M_DV)
        hn = _rms(hf_ref[rs, sl] + hb_ref[rs, sl]) * gm_ref[:, sl]
        mo = mo_ref[rs, sl]
        ym = hn / (1.0 + jnp.exp(-mo))
        acc += jnp.dot(ym.astype(BF16), w_ref[ATT_WIDTH + hd * M_DV:ATT_WIDTH + (hd + 1) * M_DV, :],
                       preferred_element_type=F32)
    x1 = x_ref[rs, :] + gt1_ref[pl.ds(row, 1), :] * acc
    x1_ref[rs, :] = x1
    h2 = (_rms(x1) * g2_ref[...]) * (1.0 + sc2_ref[pl.ds(row, 1), :]) + sh2_ref[pl.ds(row, 1), :]
    h2_ref[rs, :] = h2
    h_hi = h2.astype(BF16)
    h_lo = (h2 - h_hi.astype(F32)).astype(BF16)
    hi_prod = jnp.dot(h_hi, wr_ref[...], preferred_element_type=F32)
    lo_prod = jnp.dot(h_lo, wr_ref[:, :LANE], preferred_element_type=F32)
    logits = hi_prod[:, :LANE] + lo_prod + hi_prod[:, LANE:] + br_ref[...]
    lane = lax.broadcasted_iota(jnp.int32, (TM4, LANE), 1)
    vals, idxs = [], []
    for _ in range(TOP_K):
        mx = jnp.max(logits, axis=-1, keepdims=True)
        ik = jnp.min(jnp.where(logits == mx, lane, LANE), axis=-1, keepdims=True)
        vals.append(mx)
        idxs.append(ik)
        logits = jnp.where(lane == ik, -jnp.inf, logits)
    es = [jnp.exp(vk - vals[0]) for vk in vals]
    tot = es[0] + es[1] + es[2] + es[3]
    chosen = jnp.zeros((TM4, LANE), F32)
    for kk in range(TOP_K):
        chosen = jnp.where(lane == idxs[kk], 1.0, chosen)
    ri = lax.broadcasted_iota(jnp.int32, (TM4, TM4), 0)
    ci = lax.broadcasted_iota(jnp.int32, (TM4, TM4), 1)
    before = jnp.where(ci < ri, 1.0, 0.0).astype(BF16)
    earlier = jnp.dot(before, chosen.astype(BF16), preferred_element_type=F32)
    ti = jnp.zeros((TM4, LANE), jnp.int32)
    tr = jnp.zeros((TM4, LANE), jnp.int32)
    tw = jnp.zeros((TM4, LANE), F32)
    for kk in range(TOP_K):
        rk = jnp.sum(jnp.where(lane == idxs[kk], earlier, 0.0), axis=-1, keepdims=True)
        ti = jnp.where(lane == kk, idxs[kk], ti)
        tr = jnp.where(lane == kk, rk.astype(jnp.int32), tr)
        tw = jnp.where(lane == kk, es[kk] / tot, tw)
    ti_ref[rs, :] = ti
    tr_ref[rs, :] = tr
    tw_ref[rs, :] = tw
    tc_ref[hv] = jnp.sum(chosen, axis=0, keepdims=True).astype(jnp.int32)


def _out_projection(a_lat, hf, hb, mo, x2, w_out_b, g_mlstm, gt1, sh2, sc2, g2, wr_both, br_p):
    full = lambda shape: pl.BlockSpec(shape, lambda i: (0, 0))
    rows = TILES_PER_STEP4 * TM4
    row_blk = lambda w: pl.BlockSpec((rows, w), lambda i: (i, 0))
    return pl.pallas_call(
        _outproj_kernel,
        grid=(T_LAT // rows,),
        in_specs=[
            row_blk(ATT_WIDTH), row_blk(M_WIDTH), row_blk(M_WIDTH), row_blk(M_WIDTH),
            row_blk(D_MODEL), full((ATT_WIDTH + M_WIDTH, D_MODEL)), full((1, M_WIDTH)),
            full((MOD_ROWS, D_MODEL)), full((MOD_ROWS, D_MODEL)), full((MOD_ROWS, D_MODEL)),
            full((1, D_MODEL)), full((D_MODEL, 2 * LANE)), full((1, LANE)),
        ],
        out_specs=[row_blk(D_MODEL), row_blk(D_MODEL), row_blk(LANE), row_blk(LANE), row_blk(LANE),
                   pl.BlockSpec((TILES_PER_STEP4, 1, LANE), lambda i: (i, 0, 0))],
        out_shape=[
            jax.ShapeDtypeStruct((T_LAT, D_MODEL), F32),
            jax.ShapeDtypeStruct((T_LAT, D_MODEL), F32),
            jax.ShapeDtypeStruct((T_LAT, LANE), jnp.int32),
            jax.ShapeDtypeStruct((T_LAT, LANE), jnp.int32),
            jax.ShapeDtypeStruct((T_LAT, LANE), F32),
            jax.ShapeDtypeStruct((T_LAT // TM4, 1, LANE), jnp.int32),
        ],
        compiler_params=_params(1),
        name="out_proj_router",
    )(a_lat, hf, hb, mo, x2, w_out_b, g_mlstm, gt1, sh2, sc2, g2, wr_both, br_p)


MB = 128
N_ASSIGN = T_LAT * TOP_K
N_BLOCKS = N_ASSIGN // MB + N_EXPERTS
P_ROWS = N_BLOCKS * MB
N_PADS = P_ROWS - N_ASSIGN
SB = 9
W_MAX = N_EXPERTS + (N_BLOCKS - N_EXPERTS) // SB
XROWS = SB * MB


def _routing_tables(top_idx, tile_rank, tile_counts):
    n_tiles = tile_counts.shape[0]
    counts = jnp.sum(tile_counts, axis=0)
    tile_base = jnp.cumsum(tile_counts, axis=0) - tile_counts
    nblk = (counts + MB - 1) // MB
    blk_start = jnp.cumsum(nblk) - nblk
    n_valid_blk = jnp.sum(nblk)
    offs = blk_start[None, :] * MB + tile_base
    e4 = top_idx.reshape(n_tiles, -1, TOP_K, 1)
    hit = e4 == jnp.arange(N_EXPERTS, dtype=jnp.int32)
    dest = jnp.sum(jnp.where(hit, offs[:, None, None, :], 0), axis=-1)
    dest = (dest.reshape(T_LAT, TOP_K) + tile_rank).reshape(N_ASSIGN).astype(jnp.int32)
    nwork = (nblk + SB - 1) // SB
    wend = jnp.cumsum(nwork)
    wstart = wend - nwork
    n_items = wend[-1]
    wid = jnp.arange(W_MAX, dtype=jnp.int32)
    valid = wid < n_items
    we = jnp.minimum(jnp.sum((wend[None, :] <= wid[:, None]).astype(jnp.int32), axis=1),
                     N_EXPERTS - 1)
    local = wid - wstart[we]
    work_blk = jnp.where(valid, blk_start[we] + local * SB, 0)
    work_n = jnp.where(valid, jnp.clip(nblk[we] - local * SB, 0, SB), 0)
    last_e = we[jnp.maximum(n_items - 1, 0)]
    work_e = jnp.where(valid, we, last_e)
    i32 = lambda a: a.astype(jnp.int32)
    return (dest, i32(counts), i32(blk_start), i32(nblk), i32(n_valid_blk)[None],
            i32(work_e), i32(work_blk), i32(work_n), i32(n_items))


TMD = 256
ROW_GROUP = 8


def _dispatch_kernel(dest_ref, cnt_ref, bs_ref, nb_ref, nvb_ref, h2_ref, x_hbm, zrow, sem, psem):
    i = pl.program_id(0)

    @pl.when(i == 0)
    def _():
        zrow[...] = jnp.zeros_like(zrow)

        def zero_row(r, carry):
            pltpu.make_async_copy(zrow.at[pl.ds(0, 1)], x_hbm.at[pl.ds(r, 1)], psem).start()
            return carry

        def expert(e, carry):
            first = bs_ref[e] * MB
            lax.fori_loop(first + cnt_ref[e], first + nb_ref[e] * MB, zero_row, 0)
            return carry
        lax.fori_loop(0, N_EXPERTS, expert, 0)
        lax.fori_loop(nvb_ref[0] * MB, P_ROWS, zero_row, 0)

    base = i * (TMD * TOP_K)

    def issue(tg, carry):
        t0 = pl.multiple_of(tg * ROW_GROUP, ROW_GROUP)
        for j in range(ROW_GROUP):
            for kk in range(TOP_K):
                d = dest_ref[base + (t0 + j) * TOP_K + kk]
                pltpu.make_async_copy(h2_ref.at[pl.ds(t0 + j, 1)], x_hbm.at[pl.ds(d, 1)],
                                      sem).start(priority=kk % 2)
        return carry
    lax.fori_loop(0, TMD // ROW_GROUP, issue, 0)
    for kk in range(TOP_K):
        pltpu.make_async_copy(h2_ref, x_hbm.at[pl.ds(0, TMD)], sem).wait()

    @pl.when(i == 0)
    def _():
        for _ in range(N_PADS // TMD):
            pltpu.make_async_copy(h2_ref, x_hbm.at[pl.ds(0, TMD)], psem).wait()


def _dispatch(h2, dest, counts, blk_start, nblk, n_valid_blk):
    grid_spec = pltpu.PrefetchScalarGridSpec(
        num_scalar_prefetch=5,
        grid=(T_LAT // TMD,),
        in_specs=[pl.BlockSpec((TMD, D_MODEL), lambda i, *_: (i, 0))],
        out_specs=pl.BlockSpec(memory_space=pl.ANY),
        scratch_shapes=[
            pltpu.VMEM((8, D_MODEL), F32),
            pltpu.SemaphoreType.DMA(()),
            pltpu.SemaphoreType.DMA(()),
        ],
    )
    return pl.pallas_call(
        _dispatch_kernel,
        grid_spec=grid_spec,
        out_shape=jax.ShapeDtypeStruct((P_ROWS, D_MODEL), F32),
        compiler_params=_params(1),
        name="moe_dispatch",
    )(dest, counts, blk_start, nblk, n_valid_blk, h2)


TF = 512
NF = D_FF // TF
BIG_BLKS = 4
TF2 = TF // 2
W_PRIORITY = 1
W_AHEAD = 2
W_SLOTS = W_AHEAD + 1
MOE_VMEM_LIMIT = 62 * 1024 * 1024


def _moe_kernel(we_ref, wb_ref, wn_ref, nvb_ref,
                x_hbm, w1_hbm, w2_hbm, b1g_ref, b1u_ref, b2_ref,
                y_hbm, xbuf, xb16, acc, w1g_buf, w1u_buf, w2_buf, xsem, ysem, wsem):
    w = pl.program_id(0)
    f = pl.program_id(1)
    nsub = wn_ref[w]
    r0 = wb_ref[w] * MB

    step = w * NF + f
    wslot = step % W_SLOTS

    def w_copies(item, ff, slot_):
        e = we_ref[item]
        sem_ = wsem.at[slot_]
        gate = pltpu.make_async_copy(w1_hbm.at[e, :, pl.ds(ff * TF, TF)], w1g_buf.at[slot_], sem_)
        up = pltpu.make_async_copy(w1_hbm.at[e, :, pl.ds(D_FF + ff * TF, TF)],
                                   w1u_buf.at[slot_], sem_)
        dn_a = pltpu.make_async_copy(w2_hbm.at[e, pl.ds(ff * TF, TF2), :],
                                     w2_buf.at[slot_, 0:TF2], sem_)
        dn_b = pltpu.make_async_copy(w2_hbm.at[e, pl.ds(ff * TF + TF2, TF2), :],
                                     w2_buf.at[slot_, TF2:TF], sem_)
        return gate, dn_a, up, dn_b

    def start_w(item, ff, slot_):
        gate, dn_a, up, dn_b = w_copies(item, ff, slot_)
        gate.start(priority=0)
        dn_a.start(priority=0)
        up.start(priority=1)
        dn_b.start(priority=1)

    @pl.when(jnp.logical_and(step == 0, nsub > 0))
    def _():
        for s in range(W_AHEAD):
            start_w(0, s, s)

    last_f = f == NF - 1
    nxt_item = w + (f + W_AHEAD) // NF
    nxt_f = (f + W_AHEAD) % NF

    @pl.when(jnp.logical_and(nxt_item < W_MAX, wn_ref[jnp.minimum(nxt_item, W_MAX - 1)] > 0))
    def _():
        start_w(nxt_item, nxt_f, (step + W_AHEAD) % W_SLOTS)

    @pl.when(nsub > 0)
    def _():
        for cp in w_copies(w, f, wslot):
            cp.wait()

    def x_copy(item, s):
        src = x_hbm.at[pl.ds(wb_ref[item] * MB + s * MB, MB)]
        return pltpu.make_async_copy(src, xbuf.at[pl.ds(s * MB, MB)], xsem)

    def start_x(item):
        def body(s, carry):
            x_copy(item, s).start()
            return carry
        lax.fori_loop(0, wn_ref[item], body, 0)

    @pl.when(jnp.logical_and(w == 0, f == 0))
    def _():
        start_x(0)

    @pl.when(f == 0)
    def _():
        def wait(s, carry):
            x_copy(w, s).wait()
            return carry
        lax.fori_loop(0, nsub, wait, 0)

        def body(s, carry):
            rs = pl.ds(pl.multiple_of(s * MB, MB), MB)
            xb16[rs, :] = xbuf[rs, :].astype(BF16)
            acc[rs, :] = jnp.broadcast_to(b2_ref[...], (MB, D_MODEL))
            return carry
        lax.fori_loop(0, nsub, body, 0)

    @pl.when(jnp.logical_and(f == 1, w + 1 < W_MAX))
    def _():
        start_x(w + 1)

    def y_copy(s):
        rs = pl.ds(pl.multiple_of(s * MB, MB), MB)
        return pltpu.make_async_copy(acc.at[rs], y_hbm.at[pl.ds(r0 + s * MB, MB)], ysem)

    def chunk(row, m):
        rs = pl.ds(row, m)
        xb = xb16[rs, :]
        gate = jnp.dot(xb, w1g_buf[wslot], preferred_element_type=F32) + b1g_ref[...]
        up = jnp.dot(xb, w1u_buf[wslot], preferred_element_type=F32) + b1u_ref[...]
        gate = jnp.minimum(gate, SWIGLU_LIMIT)
        up = jnp.clip(up, -SWIGLU_LIMIT, SWIGLU_LIMIT)
        glu = gate / (1.0 + jnp.exp(-SWIGLU_ALPHA * gate))
        act = (up + 1.0) * glu
        acc[rs, :] += jnp.dot(act, w2_buf[wslot], preferred_element_type=F32)

    def write_back(first_blk, n_blk):
        @pl.when(last_f)
        def _():
            for s in range(n_blk):
                y_copy(first_blk + s).start()

    nbig = nsub // BIG_BLKS

    def big(c, carry):
        chunk(pl.multiple_of(c * (BIG_BLKS * MB), BIG_BLKS * MB), BIG_BLKS * MB)
        write_back(c * BIG_BLKS, BIG_BLKS)
        return carry
    lax.fori_loop(0, nbig, big, 0)

    done = nbig * BIG_BLKS
    size = BIG_BLKS // 2
    while size >= 1:
        has = ((nsub - done) // size) % 2 == 1
        first = done + ((nsub - done) // (2 * size)) * (2 * size)

        @pl.when(has)
        def _(first=first, size=size):
            chunk(pl.multiple_of(first * MB, size * MB), size * MB)
            write_back(first, size)
        size //= 2

    @pl.when(last_f)
    def _():
        def wait(s, carry):
            y_copy(s).wait()
            return carry
        lax.fori_loop(0, nsub, wait, 0)

    @pl.when(jnp.logical_and(w == pl.num_programs(0) - 1, f == NF - 1))
    def _():
        acc[0:MB, :] = jnp.zeros((MB, D_MODEL), F32)

        def tail_copy(bk):
            return pltpu.make_async_copy(acc.at[0:MB], y_hbm.at[pl.ds(bk * MB, MB)], ysem)

        def start(bk, carry):
            tail_copy(bk).start()
            return carry
        lax.fori_loop(nvb_ref[0], N_BLOCKS, start, 0)

        def wait(bk, carry):
            tail_copy(bk).wait()
            return carry
        lax.fori_loop(nvb_ref[0], N_BLOCKS, wait, 0)


def _moe(x_sorted, w1, b1, w2, b2, work_e, work_blk, work_n, n_valid_blk, n_items):
    b1r = b1.reshape(N_EXPERTS, 1, 2 * D_FF)
    b2r = b2.reshape(N_EXPERTS, 1, D_MODEL)
    grid_spec = pltpu.PrefetchScalarGridSpec(
        num_scalar_prefetch=4,
        grid=(n_items, NF),
        in_specs=[
            pl.BlockSpec(memory_space=pl.ANY),
            pl.BlockSpec(memory_space=pl.ANY),
            pl.BlockSpec(memory_space=pl.ANY),
            pl.BlockSpec((None, 1, TF), lambda w, f, we, *_: (we[w], 0, f)),
            pl.BlockSpec((None, 1, TF), lambda w, f, we, *_: (we[w], 0, NF + f)),
            pl.BlockSpec((None, 1, D_MODEL), lambda w, f, we, *_: (we[w], 0, 0)),
        ],
        out_specs=pl.BlockSpec(memory_space=pl.ANY),
        scratch_shapes=[
            pltpu.VMEM((XROWS, D_MODEL), F32),
            pltpu.VMEM((XROWS, D_MODEL), BF16),
            pltpu.VMEM((XROWS, D_MODEL), F32),
            pltpu.VMEM((W_SLOTS, D_MODEL, TF), F32),
            pltpu.VMEM((W_SLOTS, D_MODEL, TF), F32),
            pltpu.VMEM((W_SLOTS, TF, D_MODEL), F32),
            pltpu.SemaphoreType.DMA(()),
            pltpu.SemaphoreType.DMA(()),
            pltpu.SemaphoreType.DMA((W_SLOTS,)),
        ],
    )
    return pl.pallas_call(
        _moe_kernel,
        grid_spec=grid_spec,
        out_shape=jax.ShapeDtypeStruct((P_ROWS, D_MODEL), F32),
        compiler_params=pltpu.CompilerParams(
            dimension_semantics=("arbitrary", "arbitrary"), vmem_limit_bytes=MOE_VMEM_LIMIT),
        name="moe_experts",
    )(work_e, work_blk, work_n, n_valid_blk, x_sorted, w1, w2, b1r, b1r, b2r)


TM7 = 256
N_TILES7 = T_LAT // TM7
TILES_PER_SEQ7 = SEQ // TM7


def _final_kernel(pos_ref, x1_ref, tw_ref, gt2_ref, gf_ref, y_hbm, o_ref, ybuf, sem):
    i = pl.program_id(0)
    slot = i % 2

    def start_gather(tile, slot_):
        base = tile * (TM7 * TOP_K)

        def body(tg, carry):
            t0 = pl.multiple_of(tg * ROW_GROUP, ROW_GROUP)
            for j in range(ROW_GROUP):
                for kk in range(TOP_K):
                    p = pos_ref[base + (t0 + j) * TOP_K + kk]
                    pltpu.make_async_copy(y_hbm.at[pl.ds(p, 1)],
                                          ybuf.at[slot_, kk, pl.ds(t0 + j, 1)],
                                          sem.at[slot_]).start(priority=kk % 2)
            return carry
        lax.fori_loop(0, TM7 // ROW_GROUP, body, 0)

    @pl.when(i == 0)
    def _():
        start_gather(0, 0)

    @pl.when(i + 1 < N_TILES7)
    def _():
        start_gather(i + 1, 1 - slot)

    for kk in range(TOP_K):
        pltpu.make_async_copy(y_hbm.at[pl.ds(0, TM7)], ybuf.at[slot, kk], sem.at[slot]).wait()

    row = i // TILES_PER_SEQ7
    tw = tw_ref[...]
    mix = jnp.zeros((TM7, D_MODEL), F32)
    for kk in range(TOP_K):
        mix += ybuf[slot, kk] * tw[:, kk:kk + 1]
    xo = x1_ref[...] + gt2_ref[pl.ds(row, 1), :] * mix
    o_ref[...] = _rms(xo) * gf_ref[...]


def _final(dest, x1, top_w, gt2, g_final, y_sorted):
    grid_spec = pltpu.PrefetchScalarGridSpec(
        num_scalar_prefetch=1,
        grid=(N_TILES7,),
        in_specs=[
            pl.BlockSpec((TM7, D_MODEL), lambda i, *_: (i, 0)),
            pl.BlockSpec((TM7, LANE), lambda i, *_: (i, 0)),
            pl.BlockSpec((MOD_ROWS, D_MODEL), lambda i, *_: (0, 0)),
            pl.BlockSpec((1, D_MODEL), lambda i, *_: (0, 0)),
            pl.BlockSpec(memory_space=pl.ANY),
        ],
        out_specs=pl.BlockSpec((TM7, D_MODEL), lambda i, *_: (i, 0)),
        scratch_shapes=[
            pltpu.VMEM((2, TOP_K, TM7, D_MODEL), F32),
            pltpu.SemaphoreType.DMA((2,)),
        ],
    )
    return pl.pallas_call(
        _final_kernel,
        grid_spec=grid_spec,
        out_shape=jax.ShapeDtypeStruct((T_LAT, D_MODEL), F32),
        compiler_params=_params(1),
        name="combine_final_norm",
    )(dest, x1, top_w, gt2, g_final, y_sorted)


def _rope_tables():
    rows = SEQ // GRID_W
    row = jnp.repeat(jnp.arange(rows, dtype=F32), GRID_W)
    col = jnp.tile(jnp.arange(GRID_W, dtype=F32), rows)
    inv = ROPE_THETA ** (-jnp.arange(0, AXIS_DIM, 2, dtype=F32) / AXIS_DIM)
    ang = jnp.concatenate([row[:, None] * inv, col[:, None] * inv], axis=-1)
    cos = jnp.repeat(jnp.cos(ang), 2, axis=-1)
    sin = jnp.repeat(jnp.sin(ang), 2, axis=-1)
    sign = jnp.tile(jnp.array([-1.0, 1.0], F32), HEAD_DIM // 2)
    cos_t = jnp.concatenate([cos, jnp.ones((ROWS1, HEAD_DIM), F32)], axis=0)
    sin_t = jnp.concatenate([sin * sign, jnp.zeros((ROWS1, HEAD_DIM), F32)], axis=0)
    return cos_t, sin_t


def kernel(x, c, ctx, c_ctx, w_mod, b_mod, g_norm1, w_in, b_in, g_q, g_k, g_mlstm, w_out,
           g_norm2, w_router, b_router, w1, b1, w2, b2, g_final):
    x2 = x.reshape(T_LAT, D_MODEL)
    ctx2 = ctx.reshape(T_CTX, D_MODEL)
    cos_t, sin_t = _rope_tables()
    assert w_mod.shape[0] == 1
    l = 0
    c_rows = jnp.concatenate(
        [c, c_ctx[None, :], jnp.zeros((MOD_ROWS - BATCH - 1, D_MODEL), F32)], axis=0)
    mod = _modulation(c_rows, w_mod[l], b_mod[l][None, :])
    sh1, sc1, gt1, sh2, sc2, gt2 = [mod[:, k * D_MODEL:(k + 1) * D_MODEL]
                                    for k in range(N_ADALN)]
    pad = IN_COLS_PAD - IN_COLS
    w_main = w_in[l].astype(BF16)
    w_gate = jnp.pad(w_in[l][:, C_G:], ((0, 0), (0, pad))).astype(BF16)
    b_main = b_in[l][None, :C_G]
    b_gate = jnp.pad(b_in[l][C_G:], (0, pad))[None, :]
    q, k, v, mq, mkt, mv, mo, gates = _in_projection(
        x2, ctx2, sh1, sc1, g_norm1[l][None, :], w_main, b_main, w_gate, b_gate,
        g_q[l][None, :], g_k[l][None, :], cos_t, sin_t)
    a_lat = _attention(q, k, v)
    hf, hb = _mlstm(mq, mkt, mv, gates)
    wr_p = jnp.pad(w_router[l], ((0, 0), (0, LANE - N_EXPERTS)))
    wr_hi = wr_p.astype(BF16)
    wr_lo = (wr_p - wr_hi.astype(F32)).astype(BF16)
    br_p = jnp.pad(b_router[l], (0, LANE - N_EXPERTS), constant_values=NEG_BIG)[None, :]
    x1, h2, top_i, top_r, top_w, tile_cnt = _out_projection(
        a_lat, hf, hb, mo, x2, w_out[l].astype(BF16), g_mlstm[l][None, :],
        gt1, sh2, sc2, g_norm2[l][None, :], jnp.concatenate([wr_hi, wr_lo], axis=1), br_p)
    dest, counts, blk_start, nblk, n_valid_blk, work_e, work_blk, work_n, n_items = _routing_tables(
        top_i[:, :TOP_K], top_r[:, :TOP_K], tile_cnt[:, 0, :N_EXPERTS])
    x_sorted = _dispatch(h2, dest, counts, blk_start, nblk, n_valid_blk)
    y_sorted = _moe(x_sorted, w1[l], b1[l], w2[l], b2[l], work_e, work_blk, work_n, n_valid_blk,
                    n_items)
    out = _final(dest, x1, top_w, gt2, g_final[None, :], y_sorted)
    return out.reshape(BATCH, SEQ, D_MODEL)
```

```python
import jax
import jax.numpy as jnp
from jax import lax
from jax.experimental import pallas as pl
from jax.experimental.pallas import tpu as pltpu

F32 = jnp.float32
BF16 = jnp.bfloat16

D_MODEL = 2048
BATCH = 2
SEQ = 4096
CTX_LEN = 256
GRID_W = 64
N_ADALN = 6
EPS = 1e-6
N_HEADS = 8
KV_HEADS = 2
GROUP = N_HEADS // KV_HEADS
HEAD_DIM = 128
AXIS_DIM = HEAD_DIM // 2
ROPE_THETA = 10000.0
ATT_WIDTH = N_HEADS * HEAD_DIM
ATT_SCALE = HEAD_DIM ** -0.5
M_HEADS = 4
M_DK = 128
M_DV = 256
M_CHUNK = 128
M_WIDTH = M_HEADS * M_DV
N_EXPERTS = 32
TOP_K = 4
D_FF = D_MODEL
SWIGLU_LIMIT = 7.0
SWIGLU_ALPHA = 1.702

T_LAT = BATCH * SEQ
T_CTX = BATCH * CTX_LEN
T_ALL = T_LAT + T_CTX
KV_W = KV_HEADS * HEAD_DIM
MQK_W = M_HEADS * M_DK
LANE = 128

C_Q = 0
C_K = C_Q + ATT_WIDTH
C_V = C_K + KV_W
C_MQ = C_V + KV_W
C_MK = C_MQ + MQK_W
C_MV = C_MK + MQK_W
C_MO = C_MV + M_WIDTH
C_G = C_MO + M_WIDTH
IN_COLS = C_G + 4 * M_HEADS
IN_COLS_PAD = C_G + LANE

VMEM_LIMIT = 56 * 1024 * 1024


def _params(n_axes):
    return pltpu.CompilerParams(
        dimension_semantics=("arbitrary",) * n_axes, vmem_limit_bytes=VMEM_LIMIT)


def _rms(x):
    return x * lax.rsqrt(jnp.mean(x * x, axis=-1, keepdims=True) + EPS)


MOD_ROWS = 8
MOD_TN = 1024


def _mod_kernel(c_ref, w_ref, b_ref, o_ref):
    c = c_ref[...]
    s = c / (1.0 + jnp.exp(-c))
    o_ref[...] = jnp.dot(s, w_ref[...], preferred_element_type=F32) + b_ref[...]


def _modulation(c_rows, w_mod, b_mod):
    n = w_mod.shape[1]
    return pl.pallas_call(
        _mod_kernel,
        grid=(n // MOD_TN,),
        in_specs=[
            pl.BlockSpec((MOD_ROWS, D_MODEL), lambda j: (0, 0)),
            pl.BlockSpec((D_MODEL, MOD_TN), lambda j: (0, j)),
            pl.BlockSpec((1, MOD_TN), lambda j: (0, j)),
        ],
        out_specs=pl.BlockSpec((MOD_ROWS, MOD_TN), lambda j: (0, j)),
        out_shape=jax.ShapeDtypeStruct((MOD_ROWS, n), F32),
        compiler_params=_params(1),
        name="adaln_mod",
    )(c_rows, w_mod, b_mod)


TM1 = 256
N_LAT_TILES1 = T_LAT // TM1
N_TILES1 = T_ALL // TM1
TILES_PER_SEQ1 = SEQ // TM1
PCH = 512
LOG2E = 1.4426950408889634
Q_SCALE = ATT_SCALE * LOG2E
V_AUG = 2 * HEAD_DIM


STEP_TILES1 = 2
ROWS1 = STEP_TILES1 * TM1
N_STEPS1 = N_TILES1 // STEP_TILES1
LAT_STEPS1 = N_LAT_TILES1 // STEP_TILES1
STEPS_PER_SEQ1 = TILES_PER_SEQ1 // STEP_TILES1
assert T_CTX == ROWS1


def _inproj_kernel(x0_ref, xb_ref, xn_ref, ctx_ref, sh_ref, sc_ref, g1_ref, w_ref, b_ref,
                   wg_ref, bg_ref, gq_ref, gk_ref, cos_ref, sin_ref,
                   q_ref, k_ref, v_ref, mq_ref, mkt_ref, mv_ref, mo_ref, g_ref, ha_scr, hb_scr):
    i = pl.program_id(0)

    def norm_mod(xv, tile):
        row = jnp.where(tile < N_LAT_TILES1, tile // TILES_PER_SEQ1, BATCH)
        y = (_rms(xv) * g1_ref[...]) * (1.0 + sc_ref[pl.ds(row, 1), :]) + sh_ref[pl.ds(row, 1), :]
        return y.astype(BF16)

    lane = lax.broadcasted_iota(jnp.int32, (TM1, HEAD_DIM), 1)
    even = (lane % 2) == 0

    def project(h_scr, half):
        h = h_scr[...]
        rs = slice(half * TM1, (half + 1) * TM1)
        cos_f = cos_ref[rs, :]
        sin_s = sin_ref[rs, :]

        def proj(c0, c1):
            return jnp.dot(h, w_ref[:, c0:c1], preferred_element_type=F32) + b_ref[:, c0:c1]

        def head_norm_rope(a, g):
            y = _rms(a) * g
            nxt = pltpu.roll(y, HEAD_DIM - 1, 1)
            prv = pltpu.roll(y, 1, 1)
            return y * cos_f + jnp.where(even, nxt, prv) * sin_s

        for c0 in range(C_Q, C_K, PCH):
            acc = proj(c0, c0 + PCH)
            for hh in range(PCH // HEAD_DIM):
                a = acc[:, hh * HEAD_DIM:(hh + 1) * HEAD_DIM]
                r = head_norm_rope(a, gq_ref[...]) * Q_SCALE
                q_ref[rs, c0 + hh * HEAD_DIM:c0 + (hh + 1) * HEAD_DIM] = r.astype(BF16)
        acc = proj(C_K, C_MQ)
        for hh in range(KV_HEADS):
            a = acc[:, hh * HEAD_DIM:(hh + 1) * HEAD_DIM]
            k_ref[rs, hh * HEAD_DIM:(hh + 1) * HEAD_DIM] = (
                head_norm_rope(a, gk_ref[...]).astype(BF16))
        for hh in range(KV_HEADS):
            v_ref[rs, hh * V_AUG:hh * V_AUG + HEAD_DIM] = (
                acc[:, KV_W + hh * HEAD_DIM:KV_W + (hh + 1) * HEAD_DIM].astype(BF16))
            v_ref[rs, hh * V_AUG + HEAD_DIM:(hh + 1) * V_AUG] = jnp.ones((TM1, HEAD_DIM), BF16)
        mq_ref[rs, :] = proj(C_MQ, C_MK).astype(BF16)
        mk = proj(C_MK, C_MV) * (M_DK ** -0.5)
        mkt_ref[:, rs] = mk.T.astype(BF16)
        for c0 in range(C_MV, C_MO, PCH):
            mv_ref[rs, c0 - C_MV:c0 - C_MV + PCH] = proj(c0, c0 + PCH).astype(BF16)
        for c0 in range(C_MO, C_G, PCH):
            mo_ref[rs, c0 - C_MO:c0 - C_MO + PCH] = proj(c0, c0 + PCH)
        g_ref[rs, :] = jnp.dot(h, wg_ref[...], preferred_element_type=F32) + bg_ref[...]

    @pl.when(i == 0)
    def _():
        ha_scr[...] = norm_mod(x0_ref[...], 0)

    tile_b = STEP_TILES1 * i + 1
    x_b = jnp.where(tile_b < N_LAT_TILES1, xb_ref[...], ctx_ref[TM1:ROWS1, :])
    hb_scr[...] = norm_mod(x_b, tile_b)
    project(ha_scr, 0)
    tile_n = STEP_TILES1 * i + 2
    x_n = jnp.where(tile_n < N_LAT_TILES1, xn_ref[...], ctx_ref[0:TM1, :])
    project(hb_scr, 1)
    ha_scr[...] = norm_mod(x_n, tile_n)


def _in_projection(x2, ctx2, sh1, sc1, g1, w_main, b_main, w_gate, b_gate, g_q, g_k, cos_t, sin_t):
    last_lat = N_LAT_TILES1 - 1
    rope_idx = lambda i: (jnp.where(i < LAT_STEPS1, i % STEPS_PER_SEQ1, STEPS_PER_SEQ1), 0)
    full = lambda shape: pl.BlockSpec(shape, lambda i: (0, 0))
    row_blk = lambda w: pl.BlockSpec((ROWS1, w), lambda i: (i, 0))
    out_shapes = [
        jax.ShapeDtypeStruct((T_ALL, ATT_WIDTH), BF16),
        jax.ShapeDtypeStruct((T_ALL, KV_W), BF16),
        jax.ShapeDtypeStruct((T_ALL, KV_HEADS * V_AUG), BF16),
        jax.ShapeDtypeStruct((T_ALL, MQK_W), BF16),
        jax.ShapeDtypeStruct((MQK_W, T_ALL), BF16),
        jax.ShapeDtypeStruct((T_ALL, M_WIDTH), BF16),
        jax.ShapeDtypeStruct((T_ALL, M_WIDTH), F32),
        jax.ShapeDtypeStruct((T_ALL, LANE), F32),
    ]
    out_specs = [
        row_blk(ATT_WIDTH), row_blk(KV_W), row_blk(KV_HEADS * V_AUG), row_blk(MQK_W),
        pl.BlockSpec((MQK_W, ROWS1), lambda i: (0, i)),
        row_blk(M_WIDTH), row_blk(M_WIDTH), row_blk(LANE),
    ]
    x_tile = lambda idx, **kw: pl.BlockSpec((TM1, D_MODEL), idx, **kw)
    return pl.pallas_call(
        _inproj_kernel,
        grid=(N_STEPS1,),
        in_specs=[
            x_tile(lambda i: (0, 0), pipeline_mode=pl.Buffered(1)),
            x_tile(lambda i: (jnp.minimum(STEP_TILES1 * i + 1, last_lat), 0)),
            x_tile(lambda i: (jnp.minimum(STEP_TILES1 * i + 2, last_lat), 0)),
            full((T_CTX, D_MODEL)),
            full((MOD_ROWS, D_MODEL)), full((MOD_ROWS, D_MODEL)), full((1, D_MODEL)),
            pl.BlockSpec((D_MODEL, C_G), lambda i: (0, 0), pipeline_mode=pl.Buffered(1)),
            full((1, C_G)), full((D_MODEL, LANE)), full((1, LANE)),
            full((1, HEAD_DIM)), full((1, HEAD_DIM)),
            pl.BlockSpec((ROWS1, HEAD_DIM), rope_idx),
            pl.BlockSpec((ROWS1, HEAD_DIM), rope_idx),
        ],
        out_specs=out_specs,
        out_shape=out_shapes,
        scratch_shapes=[pltpu.VMEM((TM1, D_MODEL), BF16), pltpu.VMEM((TM1, D_MODEL), BF16)],
        compiler_params=_params(1),
        name="in_proj",
    )(x2, x2, x2, ctx2, sh1, sc1, g1, w_main, b_main, w_gate, b_gate, g_q, g_k, cos_t, sin_t)


TQ = 512
CK = 512
GQ_W = GROUP * HEAD_DIM


def _attn_kernel(q_ref, kl_ref, vl_ref, kc_ref, vc_ref, o_ref):
    chunks = [(kl_ref, vl_ref, c * CK, CK) for c in range(SEQ // CK)]
    chunks.append((kc_ref, vc_ref, 0, CTX_LEN))
    for g in range(GROUP):
        qh = q_ref[:, g * HEAD_DIM:(g + 1) * HEAD_DIM]
        m = jnp.full((TQ, 1), -jnp.inf, F32)
        acc = jnp.zeros((TQ, V_AUG), F32)
        for kr, vr, st, sz in chunks:
            s = lax.dot_general(qh, kr[st:st + sz, :], (((1,), (1,)), ((), ())),
                                preferred_element_type=F32)
            m_new = jnp.maximum(m, jnp.max(s, axis=-1, keepdims=True))
            p = jnp.exp2(s - m_new)
            alpha = jnp.exp2(m - m_new)
            acc = alpha * acc + jnp.dot(p.astype(BF16), vr[st:st + sz, :],
                                        preferred_element_type=F32)
            m = m_new
        o_ref[:, g * HEAD_DIM:(g + 1) * HEAD_DIM] = (
            acc[:, :HEAD_DIM] / acc[:, HEAD_DIM:]).astype(BF16)


def _attention(q, k, v):
    nq = SEQ // TQ
    ctx_blk0 = T_LAT // CTX_LEN
    lat_k = pl.BlockSpec((SEQ, HEAD_DIM), lambda b, h, i: (b, h))
    lat_v = pl.BlockSpec((SEQ, V_AUG), lambda b, h, i: (b, h))
    ctx_k = pl.BlockSpec((CTX_LEN, HEAD_DIM), lambda b, h, i: (ctx_blk0 + b, h))
    ctx_v = pl.BlockSpec((CTX_LEN, V_AUG), lambda b, h, i: (ctx_blk0 + b, h))
    q_blk = pl.BlockSpec((TQ, GQ_W), lambda b, h, i: (b * nq + i, h))
    return pl.pallas_call(
        _attn_kernel,
        grid=(BATCH, KV_HEADS, nq),
        in_specs=[q_blk, lat_k, lat_v, ctx_k, ctx_v],
        out_specs=q_blk,
        out_shape=jax.ShapeDtypeStruct((T_LAT, ATT_WIDTH), BF16),
        compiler_params=_params(3),
        name="gqa_attention",
    )(q, k, v, k, v)


N_CHUNKS = SEQ // M_CHUNK
CTX_CHUNKS = CTX_LEN // M_CHUNK
N_CHAINS = 2 * M_HEADS


def _log_sigmoid(x):
    return jnp.minimum(x, 0.0) - jnp.log(1.0 + jnp.exp(-jnp.abs(x)))


def _split3(x):
    hi = x.astype(BF16)
    r1 = x - hi.astype(F32)
    mid = r1.astype(BF16)
    lo = (r1 - mid.astype(F32)).astype(BF16)
    return hi, mid, lo


def _mlstm_kernel(qf_ref, qb_ref, qc_ref, kf_ref, kb_ref, kc_ref, vf_ref, vb_ref, vc_ref,
                  gf_ref, gb_ref, gc_ref, hf_ref, hb_ref, *state):
    c_scr = state[0:N_CHAINS]
    n_scr = state[N_CHAINS:2 * N_CHAINS]
    m_scr = state[2 * N_CHAINS:3 * N_CHAINS]
    j = pl.program_id(1)
    ri = lax.broadcasted_iota(jnp.int32, (M_CHUNK, M_CHUNK), 0)
    ci = lax.broadcasted_iota(jnp.int32, (M_CHUNK, M_CHUNK), 1)
    lower = ci <= ri
    upper = ci >= ri
    tri_l = jnp.where(lower, 1.0, 0.0).astype(BF16)
    tri_u = jnp.where(upper, 1.0, 0.0).astype(BF16)

    def gate_tables(g, rev):
        gt = g.T
        mcol, mrow = (tri_u, tri_l) if rev else (tri_l, tri_u)
        c_hi, c_mid, c_lo = _split3(_log_sigmoid(g))
        b3 = jnp.dot(mcol, jnp.concatenate([c_hi, c_mid, c_lo], axis=1),
                     preferred_element_type=F32)
        bcol = b3[:, :LANE] + b3[:, LANE:2 * LANE] + b3[:, 2 * LANE:]
        r_hi, r_mid, r_lo = _split3(_log_sigmoid(gt))
        r3 = jnp.dot(jnp.concatenate([r_hi, r_mid, r_lo], axis=0), mrow,
                     preferred_element_type=F32)
        brow = r3[:M_CHUNK] + r3[M_CHUNK:2 * M_CHUNK] + r3[2 * M_CHUNK:]
        return gt, bcol, brow

    def chain_step(chain, q, kt, v, tabs, rev, h_out):
        gt, bcol, brow = tabs
        icol = chain
        fcol = N_CHAINS + chain
        last = 0 if rev else M_CHUNK - 1
        b_col = bcol[:, fcol:fcol + 1]
        b_row = brow[fcol:fcol + 1, :]
        i_row = gt[icol:icol + 1, :]
        tot = brow[fcol:fcol + 1, last:last + 1]
        m_old = m_scr[chain][:1, :1]
        c_old = c_scr[chain][...]
        n_old = n_scr[chain][...]
        v_ones = jnp.concatenate([v, jnp.ones((M_CHUNK, LANE), BF16)], axis=1)
        c_row = i_row - b_row
        if h_out is not None:
            mask = upper if rev else lower
            mm = jnp.maximum(m_old, jnp.max(jnp.where(mask, c_row, -jnp.inf), axis=-1, keepdims=True))
            w_intra = jnp.exp(jnp.where(mask, c_row - mm, -jnp.inf))
            w_inter = jnp.exp(m_old - mm)
            s = jnp.dot(q, kt, preferred_element_type=F32) * w_intra
            qc = jnp.dot(q, c_old.astype(BF16), preferred_element_type=F32)
            qn = jnp.dot(q, n_old.astype(BF16), preferred_element_type=F32)[:, :1]
            sv = jnp.dot(s.astype(BF16), v_ones, preferred_element_type=F32)
            num = sv[:, :M_DV] + w_inter * qc
            den = sv[:, M_DV:M_DV + 1] + w_inter * qn
            h_out(num / jnp.maximum(jnp.abs(den), jnp.exp(-(b_col + mm))))
        g_row = tot + c_row
        m_new = jnp.maximum(tot + m_old, jnp.max(g_row, axis=-1, keepdims=True))
        w_state = jnp.exp(g_row - m_new)
        decay = jnp.exp(tot + m_old - m_new)
        kw = (kt.astype(F32) * w_state).astype(BF16)
        kv = jnp.dot(kw, v_ones, preferred_element_type=F32)
        c_scr[chain][...] = decay * c_old + kv[:, :M_DV]
        n_scr[chain][...] = decay * n_old + kv[:, M_DV:]
        m_scr[chain][...] = jnp.broadcast_to(m_new, (8, LANE))

    def run_chunk(q_ref, kt_ref, v_ref, g_ref, r0, rev, h_ref):
        tabs = gate_tables(g_ref[r0:r0 + M_CHUNK, :], rev)
        for hd in range(M_HEADS):
            chain = (M_HEADS if rev else 0) + hd
            q = q_ref[r0:r0 + M_CHUNK, hd * M_DK:(hd + 1) * M_DK]
            kt = kt_ref[hd * M_DK:(hd + 1) * M_DK, r0:r0 + M_CHUNK]
            v = v_ref[r0:r0 + M_CHUNK, hd * M_DV:(hd + 1) * M_DV]
            if h_ref is None:
                h_out = None
            else:
                def h_out(hv, hd=hd):
                    h_ref[:, hd * M_DV:(hd + 1) * M_DV] = hv
            chain_step(chain, q, kt, v, tabs, rev, h_out)

    @pl.when(j == 0)
    def _():
        for ref in state:
            ref[...] = jnp.zeros_like(ref)
        for cc in range(CTX_CHUNKS):
            run_chunk(qc_ref, kc_ref, vc_ref, gc_ref, cc * M_CHUNK, False, None)
            run_chunk(qc_ref, kc_ref, vc_ref, gc_ref, (CTX_CHUNKS - 1 - cc) * M_CHUNK, True, None)

    run_chunk(qf_ref, kf_ref, vf_ref, gf_ref, 0, False, hf_ref)
    run_chunk(qb_ref, kb_ref, vb_ref, gb_ref, 0, True, hb_ref)


def _mlstm(mq, mkt, mv, gates):
    ctx_blk0 = T_LAT // CTX_LEN
    fwd = lambda b, j: b * N_CHUNKS + j
    bwd = lambda b, j: b * N_CHUNKS + N_CHUNKS - 1 - j
    ctx = lambda b, j: ctx_blk0 + b

    def rows(width, tile, idx):
        return pl.BlockSpec((tile, width), lambda b, j: (idx(b, j), 0))

    def cols(tile, idx):
        return pl.BlockSpec((MQK_W, tile), lambda b, j: (0, idx(b, j)))

    h_shape = jax.ShapeDtypeStruct((T_LAT, M_WIDTH), F32)
    return pl.pallas_call(
        _mlstm_kernel,
        grid=(BATCH, N_CHUNKS),
        in_specs=[
            rows(MQK_W, M_CHUNK, fwd), rows(MQK_W, M_CHUNK, bwd), rows(MQK_W, CTX_LEN, ctx),
            cols(M_CHUNK, fwd), cols(M_CHUNK, bwd), cols(CTX_LEN, ctx),
            rows(M_WIDTH, M_CHUNK, fwd), rows(M_WIDTH, M_CHUNK, bwd), rows(M_WIDTH, CTX_LEN, ctx),
            rows(LANE, M_CHUNK, fwd), rows(LANE, M_CHUNK, bwd), rows(LANE, CTX_LEN, ctx),
        ],
        out_specs=[rows(M_WIDTH, M_CHUNK, fwd), rows(M_WIDTH, M_CHUNK, bwd)],
        out_shape=[h_shape, h_shape],
        scratch_shapes=([pltpu.VMEM((M_DK, M_DV), F32)] * N_CHAINS
                        + [pltpu.VMEM((M_DK, LANE), F32)] * N_CHAINS
                        + [pltpu.VMEM((8, LANE), F32)] * N_CHAINS),
        compiler_params=_params(2),
        name="mlstm_scan",
    )(mq, mq, mq, mkt, mkt, mkt, mv, mv, mv, gates, gates, gates)


TM4 = 256
TILES_PER_SEQ4 = SEQ // TM4
TILES_PER_STEP4 = 2
NEG_BIG = -1e30


def _outproj_kernel(a_ref, hf_ref, hb_ref, mo_ref, x_ref, w_ref, gm_ref, gt1_ref, sh2_ref,
                    sc2_ref, g2_ref, wr_ref, br_ref,
                    x1_ref, h2_ref, ti_ref, tr_ref, tw_ref, tc_ref):
    for hv in range(TILES_PER_STEP4):
        _outproj_tile(hv, a_ref, hf_ref, hb_ref, mo_ref, x_ref, w_ref, gm_ref, gt1_ref, sh2_ref,
                      sc2_ref, g2_ref, wr_ref, br_ref,
                      x1_ref, h2_ref, ti_ref, tr_ref, tw_ref, tc_ref)


def _outproj_tile(hv, a_ref, hf_ref, hb_ref, mo_ref, x_ref, w_ref, gm_ref, gt1_ref, sh2_ref,
                  sc2_ref, g2_ref, wr_ref, br_ref,
                  x1_ref, h2_ref, ti_ref, tr_ref, tw_ref, tc_ref):
    i = pl.program_id(0)
    row = (i * TILES_PER_STEP4 + hv) // TILES_PER_SEQ4
    rs = slice(hv * TM4, (hv + 1) * TM4)
    acc = jnp.dot(a_ref[rs, :], w_ref[0:ATT_WIDTH, :], preferred_element_type=F32)
    for hd in range(M_HEADS):
        sl = slice(hd * M_DV, (hd + 1) * M_DV)
        hn = _rms(hf_ref[rs, sl] + hb_ref[rs, sl]) * gm_ref[:, sl]
        mo = mo_ref[rs, sl]
        ym = hn / (1.0 + jnp.exp(-mo))
        acc += jnp.dot(ym.astype(BF16), w_ref[ATT_WIDTH + hd * M_DV:ATT_WIDTH + (hd + 1) * M_DV, :],
                       preferred_element_type=F32)
    x1 = x_ref[rs, :] + gt1_ref[pl.ds(row, 1), :] * acc
    x1_ref[rs, :] = x1
    h2 = (_rms(x1) * g2_ref[...]) * (1.0 + sc2_ref[pl.ds(row, 1), :]) + sh2_ref[pl.ds(row, 1), :]
    h2_ref[rs, :] = h2
    h_hi = h2.astype(BF16)
    h_lo = (h2 - h_hi.astype(F32)).astype(BF16)
    hi_prod = jnp.dot(h_hi, wr_ref[...], preferred_element_type=F32)
    lo_prod = jnp.dot(h_lo, wr_ref[:, :LANE], preferred_element_type=F32)
    logits = hi_prod[:, :LANE] + lo_prod + hi_prod[:, LANE:] + br_ref[...]
    lane = lax.broadcasted_iota(jnp.int32, (TM4, LANE), 1)
    vals, idxs = [], []
    for _ in range(TOP_K):
        mx = jnp.max(logits, axis=-1, keepdims=True)
        ik = jnp.min(jnp.where(logits == mx, lane, LANE), axis=-1, keepdims=True)
        vals.append(mx)
        idxs.append(ik)
        logits = jnp.where(lane == ik, -jnp.inf, logits)
    es = [jnp.exp(vk - vals[0]) for vk in vals]
    tot = es[0] + es[1] + es[2] + es[3]
    chosen = jnp.zeros((TM4, LANE), F32)
    for kk in range(TOP_K):
        chosen = jnp.where(lane == idxs[kk], 1.0, chosen)
    ri = lax.broadcasted_iota(jnp.int32, (TM4, TM4), 0)
    ci = lax.broadcasted_iota(jnp.int32, (TM4, TM4), 1)
    before = jnp.where(ci < ri, 1.0, 0.0).astype(BF16)
    earlier = jnp.dot(before, chosen.astype(BF16), preferred_element_type=F32)
    ti = jnp.zeros((TM4, LANE), jnp.int32)
    tr = jnp.zeros((TM4, LANE), jnp.int32)
    tw = jnp.zeros((TM4, LANE), F32)
    for kk in range(TOP_K):
        rk = jnp.sum(jnp.where(lane == idxs[kk], earlier, 0.0), axis=-1, keepdims=True)
        ti = jnp.where(lane == kk, idxs[kk], ti)
        tr = jnp.where(lane == kk, rk.astype(jnp.int32), tr)
        tw = jnp.where(lane == kk, es[kk] / tot, tw)
    ti_ref[rs, :] = ti
    tr_ref[rs, :] = tr
    tw_ref[rs, :] = tw
    tc_ref[hv] = jnp.sum(chosen, axis=0, keepdims=True).astype(jnp.int32)


def _out_projection(a_lat, hf, hb, mo, x2, w_out_b, g_mlstm, gt1, sh2, sc2, g2, wr_both, br_p):
    full = lambda shape: pl.BlockSpec(shape, lambda i: (0, 0))
    rows = TILES_PER_STEP4 * TM4
    row_blk = lambda w: pl.BlockSpec((rows, w), lambda i: (i, 0))
    return pl.pallas_call(
        _outproj_kernel,
        grid=(T_LAT // rows,),
        in_specs=[
            row_blk(ATT_WIDTH), row_blk(M_WIDTH), row_blk(M_WIDTH), row_blk(M_WIDTH),
            row_blk(D_MODEL), full((ATT_WIDTH + M_WIDTH, D_MODEL)), full((1, M_WIDTH)),
            full((MOD_ROWS, D_MODEL)), full((MOD_ROWS, D_MODEL)), full((MOD_ROWS, D_MODEL)),
            full((1, D_MODEL)), full((D_MODEL, 2 * LANE)), full((1, LANE)),
        ],
        out_specs=[row_blk(D_MODEL), row_blk(D_MODEL), row_blk(LANE), row_blk(LANE), row_blk(LANE),
                   pl.BlockSpec((TILES_PER_STEP4, 1, LANE), lambda i: (i, 0, 0))],
        out_shape=[
            jax.ShapeDtypeStruct((T_LAT, D_MODEL), F32),
            jax.ShapeDtypeStruct((T_LAT, D_MODEL), F32),
            jax.ShapeDtypeStruct((T_LAT, LANE), jnp.int32),
            jax.ShapeDtypeStruct((T_LAT, LANE), jnp.int32),
            jax.ShapeDtypeStruct((T_LAT, LANE), F32),
            jax.ShapeDtypeStruct((T_LAT // TM4, 1, LANE), jnp.int32),
        ],
        compiler_params=_params(1),
        name="out_proj_router",
    )(a_lat, hf, hb, mo, x2, w_out_b, g_mlstm, gt1, sh2, sc2, g2, wr_both, br_p)


MB = 128
N_ASSIGN = T_LAT * TOP_K
N_BLOCKS = N_ASSIGN // MB + N_EXPERTS
P_ROWS = N_BLOCKS * MB
N_PADS = P_ROWS - N_ASSIGN
SB = 9
W_MAX = N_EXPERTS + (N_BLOCKS - N_EXPERTS) // SB
XROWS = SB * MB


def _routing_tables(top_idx, tile_rank, tile_counts):
    n_tiles = tile_counts.shape[0]
    counts = jnp.sum(tile_counts, axis=0)
    tile_base = jnp.cumsum(tile_counts, axis=0) - tile_counts
    nblk = (counts + MB - 1) // MB
    blk_start = jnp.cumsum(nblk) - nblk
    n_valid_blk = jnp.sum(nblk)
    offs = blk_start[None, :] * MB + tile_base
    e4 = top_idx.reshape(n_tiles, -1, TOP_K, 1)
    hit = e4 == jnp.arange(N_EXPERTS, dtype=jnp.int32)
    dest = jnp.sum(jnp.where(hit, offs[:, None, None, :], 0), axis=-1)
    dest = (dest.reshape(T_LAT, TOP_K) + tile_rank).reshape(N_ASSIGN).astype(jnp.int32)
    nwork = (nblk + SB - 1) // SB
    wend = jnp.cumsum(nwork)
    wstart = wend - nwork
    n_items = wend[-1]
    wid = jnp.arange(W_MAX, dtype=jnp.int32)
    valid = wid < n_items
    we = jnp.minimum(jnp.sum((wend[None, :] <= wid[:, None]).astype(jnp.int32), axis=1),
                     N_EXPERTS - 1)
    local = wid - wstart[we]
    work_blk = jnp.where(valid, blk_start[we] + local * SB, 0)
    work_n = jnp.where(valid, jnp.clip(nblk[we] - local * SB, 0, SB), 0)
    last_e = we[jnp.maximum(n_items - 1, 0)]
    work_e = jnp.where(valid, we, last_e)
    i32 = lambda a: a.astype(jnp.int32)
    return (dest, i32(counts), i32(blk_start), i32(nblk), i32(n_valid_blk)[None],
            i32(work_e), i32(work_blk), i32(work_n), i32(n_items))


TMD = 256
ROW_GROUP = 8


def _dispatch_kernel(dest_ref, cnt_ref, bs_ref, nb_ref, nvb_ref, h2_ref, x_hbm, zrow, sem, psem):
    i = pl.program_id(0)

    @pl.when(i == 0)
    def _():
        zrow[...] = jnp.zeros_like(zrow)

        def zero_row(r, carry):
            pltpu.make_async_copy(zrow.at[pl.ds(0, 1)], x_hbm.at[pl.ds(r, 1)], psem).start()
            return carry

        def expert(e, carry):
            first = bs_ref[e] * MB
            lax.fori_loop(first + cnt_ref[e], first + nb_ref[e] * MB, zero_row, 0)
            return carry
        lax.fori_loop(0, N_EXPERTS, expert, 0)
        lax.fori_loop(nvb_ref[0] * MB, P_ROWS, zero_row, 0)

    base = i * (TMD * TOP_K)

    def issue(tg, carry):
        t0 = pl.multiple_of(tg * ROW_GROUP, ROW_GROUP)
        for j in range(ROW_GROUP):
            for kk in range(TOP_K):
                d = dest_ref[base + (t0 + j) * TOP_K + kk]
                pltpu.make_async_copy(h2_ref.at[pl.ds(t0 + j, 1)], x_hbm.at[pl.ds(d, 1)],
                                      sem).start(priority=kk % 2)
        return carry
    lax.fori_loop(0, TMD // ROW_GROUP, issue, 0)
    for kk in range(TOP_K):
        pltpu.make_async_copy(h2_ref, x_hbm.at[pl.ds(0, TMD)], sem).wait()

    @pl.when(i == 0)
    def _():
        for _ in range(N_PADS // TMD):
            pltpu.make_async_copy(h2_ref, x_hbm.at[pl.ds(0, TMD)], psem).wait()


def _dispatch(h2, dest, counts, blk_start, nblk, n_valid_blk):
    grid_spec = pltpu.PrefetchScalarGridSpec(
        num_scalar_prefetch=5,
        grid=(T_LAT // TMD,),
        in_specs=[pl.BlockSpec((TMD, D_MODEL), lambda i, *_: (i, 0))],
        out_specs=pl.BlockSpec(memory_space=pl.ANY),
        scratch_shapes=[
            pltpu.VMEM((8, D_MODEL), F32),
            pltpu.SemaphoreType.DMA(()),
            pltpu.SemaphoreType.DMA(()),
        ],
    )
    return pl.pallas_call(
        _dispatch_kernel,
        grid_spec=grid_spec,
        out_shape=jax.ShapeDtypeStruct((P_ROWS, D_MODEL), F32),
        compiler_params=_params(1),
        name="moe_dispatch",
    )(dest, counts, blk_start, nblk, n_valid_blk, h2)


TF = 512
NF = D_FF // TF
BIG_BLKS = 4
TF2 = TF // 2
W_PRIORITY = 1
W_AHEAD = 2
W_SLOTS = W_AHEAD + 1
MOE_VMEM_LIMIT = 62 * 1024 * 1024


def _moe_kernel(we_ref, wb_ref, wn_ref, nvb_ref,
                x_hbm, w1_hbm, w2_hbm, b1g_ref, b1u_ref, b2_ref,
                y_hbm, xbuf, xb16, acc, w1g_buf, w1u_buf, w2_buf, xsem, ysem, wsem):
    w = pl.program_id(0)
    f = pl.program_id(1)
    nsub = wn_ref[w]
    r0 = wb_ref[w] * MB

    step = w * NF + f
    wslot = step % W_SLOTS

    def w_copies(item, ff, slot_):
        e = we_ref[item]
        sem_ = wsem.at[slot_]
        gate = pltpu.make_async_copy(w1_hbm.at[e, :, pl.ds(ff * TF, TF)], w1g_buf.at[slot_], sem_)
        up = pltpu.make_async_copy(w1_hbm.at[e, :, pl.ds(D_FF + ff * TF, TF)],
                                   w1u_buf.at[slot_], sem_)
        dn_a = pltpu.make_async_copy(w2_hbm.at[e, pl.ds(ff * TF, TF2), :],
                                     w2_buf.at[slot_, 0:TF2], sem_)
        dn_b = pltpu.make_async_copy(w2_hbm.at[e, pl.ds(ff * TF + TF2, TF2), :],
                                     w2_buf.at[slot_, TF2:TF], sem_)
        return gate, dn_a, up, dn_b

    def start_w(item, ff, slot_):
        gate, dn_a, up, dn_b = w_copies(item, ff, slot_)
        gate.start(priority=0)
        dn_a.start(priority=0)
        up.start(priority=1)
        dn_b.start(priority=1)

    @pl.when(jnp.logical_and(step == 0, nsub > 0))
    def _():
        for s in range(W_AHEAD):
            start_w(0, s, s)

    last_f = f == NF - 1
    nxt_item = w + (f + W_AHEAD) // NF
    nxt_f = (f + W_AHEAD) % NF

    @pl.when(jnp.logical_and(nxt_item < W_MAX, wn_ref[jnp.minimum(nxt_item, W_MAX - 1)] > 0))
    def _():
        start_w(nxt_item, nxt_f, (step + W_AHEAD) % W_SLOTS)

    @pl.when(nsub > 0)
    def _():
        for cp in w_copies(w, f, wslot):
            cp.wait()

    def x_copy(item, s):
        src = x_hbm.at[pl.ds(wb_ref[item] * MB + s * MB, MB)]
        return pltpu.make_async_copy(src, xbuf.at[pl.ds(s * MB, MB)], xsem)

    def start_x(item):
        def body(s, carry):
            x_copy(item, s).start()
            return carry
        lax.fori_loop(0, wn_ref[item], body, 0)

    @pl.when(jnp.logical_and(w == 0, f == 0))
    def _():
        start_x(0)

    @pl.when(f == 0)
    def _():
        def wait(s, carry):
            x_copy(w, s).wait()
            return carry
        lax.fori_loop(0, nsub, wait, 0)

        def body(s, carry):
            rs = pl.ds(pl.multiple_of(s * MB, MB), MB)
            xb16[rs, :] = xbuf[rs, :].astype(BF16)
            acc[rs, :] = jnp.broadcast_to(b2_ref[...], (MB, D_MODEL))
            return carry
        lax.fori_loop(0, nsub, body, 0)

    @pl.when(jnp.logical_and(f == 1, w + 1 < W_MAX))
    def _():
        start_x(w + 1)

    def y_copy(s):
        rs = pl.ds(pl.multiple_of(s * MB, MB), MB)
        return pltpu.make_async_copy(acc.at[rs], y_hbm.at[pl.ds(r0 + s * MB, MB)], ysem)

    def chunk(row, m):
        rs = pl.ds(row, m)
        xb = xb16[rs, :]
        gate = jnp.dot(xb, w1g_buf[wslot], preferred_element_type=F32) + b1g_ref[...]
        up = jnp.dot(xb, w1u_buf[wslot], preferred_element_type=F32) + b1u_ref[...]
        gate = jnp.minimum(gate, SWIGLU_LIMIT)
        up = jnp.clip(up, -SWIGLU_LIMIT, SWIGLU_LIMIT)
        glu = gate / (1.0 + jnp.exp(-SWIGLU_ALPHA * gate))
        act = (up + 1.0) * glu
        acc[rs, :] += jnp.dot(act, w2_buf[wslot], preferred_element_type=F32)

    def write_back(first_blk, n_blk):
        @pl.when(last_f)
        def _():
            for s in range(n_blk):
                y_copy(first_blk + s).start()

    nbig = nsub // BIG_BLKS

    def big(c, carry):
        chunk(pl.multiple_of(c * (BIG_BLKS * MB), BIG_BLKS * MB), BIG_BLKS * MB)
        write_back(c * BIG_BLKS, BIG_BLKS)
        return carry
    lax.fori_loop(0, nbig, big, 0)

    done = nbig * BIG_BLKS
    size = BIG_BLKS // 2
    while size >= 1:
        has = ((nsub - done) // size) % 2 == 1
        first = done + ((nsub - done) // (2 * size)) * (2 * size)

        @pl.when(has)
        def _(first=first, size=size):
            chunk(pl.multiple_of(first * MB, size * MB), size * MB)
            write_back(first, size)
        size //= 2

    @pl.when(last_f)
    def _():
        def wait(s, carry):
            y_copy(s).wait()
            return carry
        lax.fori_loop(0, nsub, wait, 0)

    @pl.when(jnp.logical_and(w == pl.num_programs(0) - 1, f == NF - 1))
    def _():
        acc[0:MB, :] = jnp.zeros((MB, D_MODEL), F32)

        def tail_copy(bk):
            return pltpu.make_async_copy(acc.at[0:MB], y_hbm.at[pl.ds(bk * MB, MB)], ysem)

        def start(bk, carry):
            tail_copy(bk).start()
            return carry
        lax.fori_loop(nvb_ref[0], N_BLOCKS, start, 0)

        def wait(bk, carry):
            tail_copy(bk).wait()
            return carry
        lax.fori_loop(nvb_ref[0], N_BLOCKS, wait, 0)


def _moe(x_sorted, w1, b1, w2, b2, work_e, work_blk, work_n, n_valid_blk, n_items):
    b1r = b1.reshape(N_EXPERTS, 1, 2 * D_FF)
    b2r = b2.reshape(N_EXPERTS, 1, D_MODEL)
    grid_spec = pltpu.PrefetchScalarGridSpec(
        num_scalar_prefetch=4,
        grid=(n_items, NF),
        in_specs=[
            pl.BlockSpec(memory_space=pl.ANY),
            pl.BlockSpec(memory_space=pl.ANY),
            pl.BlockSpec(memory_space=pl.ANY),
            pl.BlockSpec((None, 1, TF), lambda w, f, we, *_: (we[w], 0, f)),
            pl.BlockSpec((None, 1, TF), lambda w, f, we, *_: (we[w], 0, NF + f)),
            pl.BlockSpec((None, 1, D_MODEL), lambda w, f, we, *_: (we[w], 0, 0)),
        ],
        out_specs=pl.BlockSpec(memory_space=pl.ANY),
        scratch_shapes=[
            pltpu.VMEM((XROWS, D_MODEL), F32),
            pltpu.VMEM((XROWS, D_MODEL), BF16),
            pltpu.VMEM((XROWS, D_MODEL), F32),
            pltpu.VMEM((W_SLOTS, D_MODEL, TF), F32),
            pltpu.VMEM((W_SLOTS, D_MODEL, TF), F32),
            pltpu.VMEM((W_SLOTS, TF, D_MODEL), F32),
            pltpu.SemaphoreType.DMA(()),
            pltpu.SemaphoreType.DMA(()),
            pltpu.SemaphoreType.DMA((W_SLOTS,)),
        ],
    )
    return pl.pallas_call(
        _moe_kernel,
        grid_spec=grid_spec,
        out_shape=jax.ShapeDtypeStruct((P_ROWS, D_MODEL), F32),
        compiler_params=pltpu.CompilerParams(
            dimension_semantics=("arbitrary", "arbitrary"), vmem_limit_bytes=MOE_VMEM_LIMIT),
        name="moe_experts",
    )(work_e, work_blk, work_n, n_valid_blk, x_sorted, w1, w2, b1r, b1r, b2r)


TM7 = 256
N_TILES7 = T_LAT // TM7
TILES_PER_SEQ7 = SEQ // TM7
N_INLINE7 = TM7


def _final_kernel(pos_ref, x1_ref, tw_ref, gt2_ref, gf_ref, y_hbm, o_ref, ybuf, sem):
    i = pl.program_id(0)
    slot = i % 2

    def gather_token(base, slot_, t):
        for kk in range(TOP_K):
            p = pos_ref[base + t * TOP_K + kk]
            pltpu.make_async_copy(y_hbm.at[pl.ds(p, 1)], ybuf.at[slot_, kk, pl.ds(t, 1)],
                                  sem.at[slot_]).start(priority=kk % 2)

    def start_gather(tile, slot_, first_token):
        base = tile * (TM7 * TOP_K)

        def body(tg, carry):
            t0 = pl.multiple_of(tg * ROW_GROUP, ROW_GROUP)
            for j in range(ROW_GROUP):
                gather_token(base, slot_, t0 + j)
            return carry
        lax.fori_loop(first_token // ROW_GROUP, TM7 // ROW_GROUP, body, 0)

    @pl.when(i == 0)
    def _():
        start_gather(0, 0, 0)

    @pl.when(i + 1 < N_TILES7)
    def _():
        start_gather(i + 1, 1 - slot, N_INLINE7)

    for kk in range(TOP_K):
        pltpu.make_async_copy(y_hbm.at[pl.ds(0, TM7)], ybuf.at[slot, kk], sem.at[slot]).wait()

    nxt_base = jnp.minimum(i + 1, N_TILES7 - 1) * (TM7 * TOP_K)
    for t in range(N_INLINE7):
        gather_token(nxt_base, 1 - slot, t)

    row = i // TILES_PER_SEQ7
    tw = tw_ref[...]
    mix = jnp.zeros((TM7, D_MODEL), F32)
    for kk in range(TOP_K):
        mix += ybuf[slot, kk] * tw[:, kk:kk + 1]
    xo = x1_ref[...] + gt2_ref[pl.ds(row, 1), :] * mix
    o_ref[...] = _rms(xo) * gf_ref[...]

    @pl.when(i == N_TILES7 - 1)
    def _():
        for kk in range(TOP_K):
            pltpu.make_async_copy(y_hbm.at[pl.ds(0, N_INLINE7)],
                                  ybuf.at[1 - slot, kk, pl.ds(0, N_INLINE7)],
                                  sem.at[1 - slot]).wait()


def _final(dest, x1, top_w, gt2, g_final, y_sorted):
    grid_spec = pltpu.PrefetchScalarGridSpec(
        num_scalar_prefetch=1,
        grid=(N_TILES7,),
        in_specs=[
            pl.BlockSpec((TM7, D_MODEL), lambda i, *_: (i, 0)),
            pl.BlockSpec((TM7, LANE), lambda i, *_: (i, 0)),
            pl.BlockSpec((MOD_ROWS, D_MODEL), lambda i, *_: (0, 0)),
            pl.BlockSpec((1, D_MODEL), lambda i, *_: (0, 0)),
            pl.BlockSpec(memory_space=pl.ANY),
        ],
        out_specs=pl.BlockSpec((TM7, D_MODEL), lambda i, *_: (i, 0)),
        scratch_shapes=[
            pltpu.VMEM((2, TOP_K, TM7, D_MODEL), F32),
            pltpu.SemaphoreType.DMA((2,)),
        ],
    )
    return pl.pallas_call(
        _final_kernel,
        grid_spec=grid_spec,
        out_shape=jax.ShapeDtypeStruct((T_LAT, D_MODEL), F32),
        compiler_params=_params(1),
        name="combine_final_norm",
    )(dest, x1, top_w, gt2, g_final, y_sorted)


def _rope_tables():
    rows = SEQ // GRID_W
    row = jnp.repeat(jnp.arange(rows, dtype=F32), GRID_W)
    col = jnp.tile(jnp.arange(GRID_W, dtype=F32), rows)
    inv = ROPE_THETA ** (-jnp.arange(0, AXIS_DIM, 2, dtype=F32) / AXIS_DIM)
    ang = jnp.concatenate([row[:, None] * inv, col[:, None] * inv], axis=-1)
    cos = jnp.repeat(jnp.cos(ang), 2, axis=-1)
    sin = jnp.repeat(jnp.sin(ang), 2, axis=-1)
    sign = jnp.tile(jnp.array([-1.0, 1.0], F32), HEAD_DIM // 2)
    cos_t = jnp.concatenate([cos, jnp.ones((ROWS1, HEAD_DIM), F32)], axis=0)
    sin_t = jnp.concatenate([sin * sign, jnp.zeros((ROWS1, HEAD_DIM), F32)], axis=0)
    return cos_t, sin_t


def kernel(x, c, ctx, c_ctx, w_mod, b_mod, g_norm1, w_in, b_in, g_q, g_k, g_mlstm, w_out,
           g_norm2, w_router, b_router, w1, b1, w2, b2, g_final):
    x2 = x.reshape(T_LAT, D_MODEL)
    ctx2 = ctx.reshape(T_CTX, D_MODEL)
    cos_t, sin_t = _rope_tables()
    assert w_mod.shape[0] == 1
    l = 0
    c_rows = jnp.concatenate(
        [c, c_ctx[None, :], jnp.zeros((MOD_ROWS - BATCH - 1, D_MODEL), F32)], axis=0)
    mod = _modulation(c_rows, w_mod[l], b_mod[l][None, :])
    sh1, sc1, gt1, sh2, sc2, gt2 = [mod[:, k * D_MODEL:(k + 1) * D_MODEL]
                                    for k in range(N_ADALN)]
    pad = IN_COLS_PAD - IN_COLS
    w_main = w_in[l].astype(BF16)
    w_gate = jnp.pad(w_in[l][:, C_G:], ((0, 0), (0, pad))).astype(BF16)
    b_main = b_in[l][None, :C_G]
    b_gate = jnp.pad(b_in[l][C_G:], (0, pad))[None, :]
    q, k, v, mq, mkt, mv, mo, gates = _in_projection(
        x2, ctx2, sh1, sc1, g_norm1[l][None, :], w_main, b_main, w_gate, b_gate,
        g_q[l][None, :], g_k[l][None, :], cos_t, sin_t)
    a_lat = _attention(q, k, v)
    hf, hb = _mlstm(mq, mkt, mv, gates)
    wr_p = jnp.pad(w_router[l], ((0, 0), (0, LANE - N_EXPERTS)))
    wr_hi = wr_p.astype(BF16)
    wr_lo = (wr_p - wr_hi.astype(F32)).astype(BF16)
    br_p = jnp.pad(b_router[l], (0, LANE - N_EXPERTS), constant_values=NEG_BIG)[None, :]
    x1, h2, top_i, top_r, top_w, tile_cnt = _out_projection(
        a_lat, hf, hb, mo, x2, w_out[l].astype(BF16), g_mlstm[l][None, :],
        gt1, sh2, sc2, g_norm2[l][None, :], jnp.concatenate([wr_hi, wr_lo], axis=1), br_p)
    dest, counts, blk_start, nblk, n_valid_blk, work_e, work_blk, work_n, n_items = _routing_tables(
        top_i[:, :TOP_K], top_r[:, :TOP_K], tile_cnt[:, 0, :N_EXPERTS])
    x_sorted = _dispatch(h2, dest, counts, blk_start, nblk, n_valid_blk)
    y_sorted = _moe(x_sorted, w1[l], b1[l], w2[l], b2[l], work_e, work_blk, work_n, n_valid_blk,
                    n_items)
    out = _final(dest, x1, top_w, gt2, g_final[None, :], y_sorted)
    return out.reshape(BATCH, SEQ, D_MODEL)
```

```python
import jax
import jax.numpy as jnp
from jax import lax
from jax.experimental import pallas as pl
from jax.experimental.pallas import tpu as pltpu

F32 = jnp.float32
BF16 = jnp.bfloat16

D_MODEL = 2048
BATCH = 2
SEQ = 4096
CTX_LEN = 256
GRID_W = 64
N_ADALN = 6
EPS = 1e-6
N_HEADS = 8
KV_HEADS = 2
GROUP = N_HEADS // KV_HEADS
HEAD_DIM = 128
AXIS_DIM = HEAD_DIM // 2
ROPE_THETA = 10000.0
ATT_WIDTH = N_HEADS * HEAD_DIM
ATT_SCALE = HEAD_DIM ** -0.5
M_HEADS = 4
M_DK = 128
M_DV = 256
M_CHUNK = 128
M_WIDTH = M_HEADS * M_DV
N_EXPERTS = 32
TOP_K = 4
D_FF = D_MODEL
SWIGLU_LIMIT = 7.0
SWIGLU_ALPHA = 1.702

T_LAT = BATCH * SEQ
T_CTX = BATCH * CTX_LEN
T_ALL = T_LAT + T_CTX
KV_W = KV_HEADS * HEAD_DIM
MQK_W = M_HEADS * M_DK
LANE = 128

C_Q = 0
C_K = C_Q + ATT_WIDTH
C_V = C_K + KV_W
C_MQ = C_V + KV_W
C_MK = C_MQ + MQK_W
C_MV = C_MK + MQK_W
C_MO = C_MV + M_WIDTH
C_G = C_MO + M_WIDTH
IN_COLS = C_G + 4 * M_HEADS
IN_COLS_PAD = C_G + LANE

VMEM_LIMIT = 56 * 1024 * 1024


def _params(n_axes):
    return pltpu.CompilerParams(
        dimension_semantics=("arbitrary",) * n_axes, vmem_limit_bytes=VMEM_LIMIT)


def _rms(x):
    return x * lax.rsqrt(jnp.mean(x * x, axis=-1, keepdims=True) + EPS)


MOD_ROWS = 8
MOD_TN = 1024


def _mod_kernel(c_ref, w_ref, b_ref, o_ref):
    c = c_ref[...]
    s = c / (1.0 + jnp.exp(-c))
    o_ref[...] = jnp.dot(s, w_ref[...], preferred_element_type=F32) + b_ref[...]


def _modulation(c_rows, w_mod, b_mod):
    n = w_mod.shape[1]
    return pl.pallas_call(
        _mod_kernel,
        grid=(n // MOD_TN,),
        in_specs=[
            pl.BlockSpec((MOD_ROWS, D_MODEL), lambda j: (0, 0)),
            pl.BlockSpec((D_MODEL, MOD_TN), lambda j: (0, j)),
            pl.BlockSpec((1, MOD_TN), lambda j: (0, j)),
        ],
        out_specs=pl.BlockSpec((MOD_ROWS, MOD_TN), lambda j: (0, j)),
        out_shape=jax.ShapeDtypeStruct((MOD_ROWS, n), F32),
        compiler_params=_params(1),
        name="adaln_mod",
    )(c_rows, w_mod, b_mod)


TM1 = 256
N_LAT_TILES1 = T_LAT // TM1
N_TILES1 = T_ALL // TM1
TILES_PER_SEQ1 = SEQ // TM1
PCH = 512
LOG2E = 1.4426950408889634
Q_SCALE = ATT_SCALE * LOG2E
V_AUG = 2 * HEAD_DIM


STEP_TILES1 = 2
ROWS1 = STEP_TILES1 * TM1
N_STEPS1 = N_TILES1 // STEP_TILES1
LAT_STEPS1 = N_LAT_TILES1 // STEP_TILES1
STEPS_PER_SEQ1 = TILES_PER_SEQ1 // STEP_TILES1
assert T_CTX == ROWS1


def _inproj_kernel(x0_ref, xb_ref, xn_ref, ctx_ref, sh_ref, sc_ref, g1_ref, w_ref, b_ref,
                   wg_ref, bg_ref, gq_ref, gk_ref, cos_ref, sin_ref,
                   q_ref, k_ref, v_ref, mq_ref, mkt_ref, mv_ref, mo_ref, g_ref, ha_scr, hb_scr):
    i = pl.program_id(0)

    def norm_mod(xv, tile):
        row = jnp.where(tile < N_LAT_TILES1, tile // TILES_PER_SEQ1, BATCH)
        y = (_rms(xv) * g1_ref[...]) * (1.0 + sc_ref[pl.ds(row, 1), :]) + sh_ref[pl.ds(row, 1), :]
        return y.astype(BF16)

    lane = lax.broadcasted_iota(jnp.int32, (TM1, HEAD_DIM), 1)
    even = (lane % 2) == 0

    def project(h_scr, half):
        h = h_scr[...]
        rs = slice(half * TM1, (half + 1) * TM1)
        cos_f = cos_ref[rs, :]
        sin_s = sin_ref[rs, :]

        def proj(c0, c1):
            return jnp.dot(h, w_ref[:, c0:c1], preferred_element_type=F32) + b_ref[:, c0:c1]

        def head_norm_rope(a, g):
            y = _rms(a) * g
            nxt = pltpu.roll(y, HEAD_DIM - 1, 1)
            prv = pltpu.roll(y, 1, 1)
            return y * cos_f + jnp.where(even, nxt, prv) * sin_s

        for c0 in range(C_Q, C_K, PCH):
            acc = proj(c0, c0 + PCH)
            for hh in range(PCH // HEAD_DIM):
                a = acc[:, hh * HEAD_DIM:(hh + 1) * HEAD_DIM]
                r = head_norm_rope(a, gq_ref[...]) * Q_SCALE
                q_ref[rs, c0 + hh * HEAD_DIM:c0 + (hh + 1) * HEAD_DIM] = r.astype(BF16)
        acc = proj(C_K, C_MQ)
        for hh in range(KV_HEADS):
            a = acc[:, hh * HEAD_DIM:(hh + 1) * HEAD_DIM]
            k_ref[rs, hh * HEAD_DIM:(hh + 1) * HEAD_DIM] = (
                head_norm_rope(a, gk_ref[...]).astype(BF16))
        for hh in range(KV_HEADS):
            v_ref[rs, hh * V_AUG:hh * V_AUG + HEAD_DIM] = (
                acc[:, KV_W + hh * HEAD_DIM:KV_W + (hh + 1) * HEAD_DIM].astype(BF16))
            v_ref[rs, hh * V_AUG + HEAD_DIM:(hh + 1) * V_AUG] = jnp.ones((TM1, HEAD_DIM), BF16)
        mq_ref[rs, :] = proj(C_MQ, C_MK).astype(BF16)
        mk = proj(C_MK, C_MV) * (M_DK ** -0.5)
        mkt_ref[:, rs] = mk.T.astype(BF16)
        for c0 in range(C_MV, C_MO, PCH):
            mv_ref[rs, c0 - C_MV:c0 - C_MV + PCH] = proj(c0, c0 + PCH).astype(BF16)
        for c0 in range(C_MO, C_G, PCH):
            mo_ref[rs, c0 - C_MO:c0 - C_MO + PCH] = proj(c0, c0 + PCH)
        g_ref[rs, :] = jnp.dot(h, wg_ref[...], preferred_element_type=F32) + bg_ref[...]

    @pl.when(i == 0)
    def _():
        ha_scr[...] = norm_mod(x0_ref[...], 0)

    tile_b = STEP_TILES1 * i + 1
    x_b = jnp.where(tile_b < N_LAT_TILES1, xb_ref[...], ctx_ref[TM1:ROWS1, :])
    hb_scr[...] = norm_mod(x_b, tile_b)
    project(ha_scr, 0)
    tile_n = STEP_TILES1 * i + 2
    x_n = jnp.where(tile_n < N_LAT_TILES1, xn_ref[...], ctx_ref[0:TM1, :])
    project(hb_scr, 1)
    ha_scr[...] = norm_mod(x_n, tile_n)


def _in_projection(x2, ctx2, sh1, sc1, g1, w_main, b_main, w_gate, b_gate, g_q, g_k, cos_t, sin_t):
    last_lat = N_LAT_TILES1 - 1
    rope_idx = lambda i: (jnp.where(i < LAT_STEPS1, i % STEPS_PER_SEQ1, STEPS_PER_SEQ1), 0)
    full = lambda shape: pl.BlockSpec(shape, lambda i: (0, 0))
    row_blk = lambda w: pl.BlockSpec((ROWS1, w), lambda i: (i, 0))
    out_shapes = [
        jax.ShapeDtypeStruct((T_ALL, ATT_WIDTH), BF16),
        jax.ShapeDtypeStruct((T_ALL, KV_W), BF16),
        jax.ShapeDtypeStruct((T_ALL, KV_HEADS * V_AUG), BF16),
        jax.ShapeDtypeStruct((T_ALL, MQK_W), BF16),
        jax.ShapeDtypeStruct((MQK_W, T_ALL), BF16),
        jax.ShapeDtypeStruct((T_ALL, M_WIDTH), BF16),
        jax.ShapeDtypeStruct((T_ALL, M_WIDTH), F32),
        jax.ShapeDtypeStruct((T_ALL, LANE), F32),
    ]
    out_specs = [
        row_blk(ATT_WIDTH), row_blk(KV_W), row_blk(KV_HEADS * V_AUG), row_blk(MQK_W),
        pl.BlockSpec((MQK_W, ROWS1), lambda i: (0, i)),
        row_blk(M_WIDTH), row_blk(M_WIDTH), row_blk(LANE),
    ]
    x_tile = lambda idx, **kw: pl.BlockSpec((TM1, D_MODEL), idx, **kw)
    return pl.pallas_call(
        _inproj_kernel,
        grid=(N_STEPS1,),
        in_specs=[
            x_tile(lambda i: (0, 0), pipeline_mode=pl.Buffered(1)),
            x_tile(lambda i: (jnp.minimum(STEP_TILES1 * i + 1, last_lat), 0)),
            x_tile(lambda i: (jnp.minimum(STEP_TILES1 * i + 2, last_lat), 0)),
            full((T_CTX, D_MODEL)),
            full((MOD_ROWS, D_MODEL)), full((MOD_ROWS, D_MODEL)), full((1, D_MODEL)),
            pl.BlockSpec((D_MODEL, C_G), lambda i: (0, 0), pipeline_mode=pl.Buffered(1)),
            full((1, C_G)), full((D_MODEL, LANE)), full((1, LANE)),
            full((1, HEAD_DIM)), full((1, HEAD_DIM)),
            pl.BlockSpec((ROWS1, HEAD_DIM), rope_idx),
            pl.BlockSpec((ROWS1, HEAD_DIM), rope_idx),
        ],
        out_specs=out_specs,
        out_shape=out_shapes,
        scratch_shapes=[pltpu.VMEM((TM1, D_MODEL), BF16), pltpu.VMEM((TM1, D_MODEL), BF16)],
        compiler_params=_params(1),
        name="in_proj",
    )(x2, x2, x2, ctx2, sh1, sc1, g1, w_main, b_main, w_gate, b_gate, g_q, g_k, cos_t, sin_t)


TQ = 512
CK = 512
GQ_W = GROUP * HEAD_DIM


def _attn_kernel(q_ref, kl_ref, vl_ref, kc_ref, vc_ref, o_ref):
    chunks = [(kl_ref, vl_ref, c * CK, CK) for c in range(SEQ // CK)]
    chunks.append((kc_ref, vc_ref, 0, CTX_LEN))
    for g in range(GROUP):
        qh = q_ref[:, g * HEAD_DIM:(g + 1) * HEAD_DIM]
        m = jnp.full((TQ, 1), -jnp.inf, F32)
        acc = jnp.zeros((TQ, V_AUG), F32)
        for kr, vr, st, sz in chunks:
            s = lax.dot_general(qh, kr[st:st + sz, :], (((1,), (1,)), ((), ())),
                                preferred_element_type=F32)
            m_new = jnp.maximum(m, jnp.max(s, axis=-1, keepdims=True))
            p = jnp.exp2(s - m_new)
            alpha = jnp.exp2(m - m_new)
            acc = alpha * acc + jnp.dot(p.astype(BF16), vr[st:st + sz, :],
                                        preferred_element_type=F32)
            m = m_new
        o_ref[:, g * HEAD_DIM:(g + 1) * HEAD_DIM] = (
            acc[:, :HEAD_DIM] / acc[:, HEAD_DIM:]).astype(BF16)


def _attention(q, k, v):
    nq = SEQ // TQ
    ctx_blk0 = T_LAT // CTX_LEN
    lat_k = pl.BlockSpec((SEQ, HEAD_DIM), lambda b, h, i: (b, h))
    lat_v = pl.BlockSpec((SEQ, V_AUG), lambda b, h, i: (b, h))
    ctx_k = pl.BlockSpec((CTX_LEN, HEAD_DIM), lambda b, h, i: (ctx_blk0 + b, h))
    ctx_v = pl.BlockSpec((CTX_LEN, V_AUG), lambda b, h, i: (ctx_blk0 + b, h))
    q_blk = pl.BlockSpec((TQ, GQ_W), lambda b, h, i: (b * nq + i, h))
    return pl.pallas_call(
        _attn_kernel,
        grid=(BATCH, KV_HEADS, nq),
        in_specs=[q_blk, lat_k, lat_v, ctx_k, ctx_v],
        out_specs=q_blk,
        out_shape=jax.ShapeDtypeStruct((T_LAT, ATT_WIDTH), BF16),
        compiler_params=_params(3),
        name="gqa_attention",
    )(q, k, v, k, v)


N_CHUNKS = SEQ // M_CHUNK
CTX_CHUNKS = CTX_LEN // M_CHUNK
N_CHAINS = 2 * M_HEADS


def _log_sigmoid(x):
    return jnp.minimum(x, 0.0) - jnp.log(1.0 + jnp.exp(-jnp.abs(x)))


def _split3(x):
    hi = x.astype(BF16)
    r1 = x - hi.astype(F32)
    mid = r1.astype(BF16)
    lo = (r1 - mid.astype(F32)).astype(BF16)
    return hi, mid, lo


def _mlstm_kernel(qf_ref, qb_ref, qc_ref, kf_ref, kb_ref, kc_ref, vf_ref, vb_ref, vc_ref,
                  gf_ref, gb_ref, gc_ref, hf_ref, hb_ref, *state):
    c_scr = state[0:N_CHAINS]
    n_scr = state[N_CHAINS:2 * N_CHAINS]
    m_scr = state[2 * N_CHAINS:3 * N_CHAINS]
    j = pl.program_id(1)
    ri = lax.broadcasted_iota(jnp.int32, (M_CHUNK, M_CHUNK), 0)
    ci = lax.broadcasted_iota(jnp.int32, (M_CHUNK, M_CHUNK), 1)
    lower = ci <= ri
    upper = ci >= ri
    tri_l = jnp.where(lower, 1.0, 0.0).astype(BF16)
    tri_u = jnp.where(upper, 1.0, 0.0).astype(BF16)

    def gate_tables(g, rev):
        gt = g.T
        mcol, mrow = (tri_u, tri_l) if rev else (tri_l, tri_u)
        c_hi, c_mid, c_lo = _split3(_log_sigmoid(g))
        b3 = jnp.dot(mcol, jnp.concatenate([c_hi, c_mid, c_lo], axis=1),
                     preferred_element_type=F32)
        bcol = b3[:, :LANE] + b3[:, LANE:2 * LANE] + b3[:, 2 * LANE:]
        r_hi, r_mid, r_lo = _split3(_log_sigmoid(gt))
        r3 = jnp.dot(jnp.concatenate([r_hi, r_mid, r_lo], axis=0), mrow,
                     preferred_element_type=F32)
        brow = r3[:M_CHUNK] + r3[M_CHUNK:2 * M_CHUNK] + r3[2 * M_CHUNK:]
        return gt, bcol, brow

    def chain_step(chain, q, kt, v, tabs, rev, h_out):
        gt, bcol, brow = tabs
        icol = chain
        fcol = N_CHAINS + chain
        last = 0 if rev else M_CHUNK - 1
        b_col = bcol[:, fcol:fcol + 1]
        b_row = brow[fcol:fcol + 1, :]
        i_row = gt[icol:icol + 1, :]
        tot = brow[fcol:fcol + 1, last:last + 1]
        m_old = m_scr[chain][:1, :1]
        c_old = c_scr[chain][...]
        n_old = n_scr[chain][...]
        v_ones = jnp.concatenate([v, jnp.ones((M_CHUNK, LANE), BF16)], axis=1)
        c_row = i_row - b_row
        if h_out is not None:
            mask = upper if rev else lower
            mm = jnp.maximum(m_old, jnp.max(jnp.where(mask, c_row, -jnp.inf), axis=-1, keepdims=True))
            w_intra = jnp.exp(jnp.where(mask, c_row - mm, -jnp.inf))
            w_inter = jnp.exp(m_old - mm)
            s = jnp.dot(q, kt, preferred_element_type=F32) * w_intra
            qc = jnp.dot(q, c_old.astype(BF16), preferred_element_type=F32)
            qn = jnp.dot(q, n_old.astype(BF16), preferred_element_type=F32)[:, :1]
            sv = jnp.dot(s.astype(BF16), v_ones, preferred_element_type=F32)
            num = sv[:, :M_DV] + w_inter * qc
            den = sv[:, M_DV:M_DV + 1] + w_inter * qn
            h_out(num / jnp.maximum(jnp.abs(den), jnp.exp(-(b_col + mm))))
        g_row = tot + c_row
        m_new = jnp.maximum(tot + m_old, jnp.max(g_row, axis=-1, keepdims=True))
        w_state = jnp.exp(g_row - m_new)
        decay = jnp.exp(tot + m_old - m_new)
        kw = (kt.astype(F32) * w_state).astype(BF16)
        kv = jnp.dot(kw, v_ones, preferred_element_type=F32)
        c_scr[chain][...] = decay * c_old + kv[:, :M_DV]
        n_scr[chain][...] = decay * n_old + kv[:, M_DV:]
        m_scr[chain][...] = jnp.broadcast_to(m_new, (8, LANE))

    def run_chunk(q_ref, kt_ref, v_ref, g_ref, r0, rev, h_ref):
        tabs = gate_tables(g_ref[r0:r0 + M_CHUNK, :], rev)
        for hd in range(M_HEADS):
            chain = (M_HEADS if rev else 0) + hd
            q = q_ref[r0:r0 + M_CHUNK, hd * M_DK:(hd + 1) * M_DK]
            kt = kt_ref[hd * M_DK:(hd + 1) * M_DK, r0:r0 + M_CHUNK]
            v = v_ref[r0:r0 + M_CHUNK, hd * M_DV:(hd + 1) * M_DV]
            if h_ref is None:
                h_out = None
            else:
                def h_out(hv, hd=hd):
                    h_ref[:, hd * M_DV:(hd + 1) * M_DV] = hv
            chain_step(chain, q, kt, v, tabs, rev, h_out)

    @pl.when(j == 0)
    def _():
        for ref in state:
            ref[...] = jnp.zeros_like(ref)
        for cc in range(CTX_CHUNKS):
            run_chunk(qc_ref, kc_ref, vc_ref, gc_ref, cc * M_CHUNK, False, None)
            run_chunk(qc_ref, kc_ref, vc_ref, gc_ref, (CTX_CHUNKS - 1 - cc) * M_CHUNK, True, None)

    run_chunk(qf_ref, kf_ref, vf_ref, gf_ref, 0, False, hf_ref)
    run_chunk(qb_ref, kb_ref, vb_ref, gb_ref, 0, True, hb_ref)


def _mlstm(mq, mkt, mv, gates):
    ctx_blk0 = T_LAT // CTX_LEN
    fwd = lambda b, j: b * N_CHUNKS + j
    bwd = lambda b, j: b * N_CHUNKS + N_CHUNKS - 1 - j
    ctx = lambda b, j: ctx_blk0 + b

    def rows(width, tile, idx):
        return pl.BlockSpec((tile, width), lambda b, j: (idx(b, j), 0))

    def cols(tile, idx):
        return pl.BlockSpec((MQK_W, tile), lambda b, j: (0, idx(b, j)))

    h_shape = jax.ShapeDtypeStruct((T_LAT, M_WIDTH), F32)
    return pl.pallas_call(
        _mlstm_kernel,
        grid=(BATCH, N_CHUNKS),
        in_specs=[
            rows(MQK_W, M_CHUNK, fwd), rows(MQK_W, M_CHUNK, bwd), rows(MQK_W, CTX_LEN, ctx),
            cols(M_CHUNK, fwd), cols(M_CHUNK, bwd), cols(CTX_LEN, ctx),
            rows(M_WIDTH, M_CHUNK, fwd), rows(M_WIDTH, M_CHUNK, bwd), rows(M_WIDTH, CTX_LEN, ctx),
            rows(LANE, M_CHUNK, fwd), rows(LANE, M_CHUNK, bwd), rows(LANE, CTX_LEN, ctx),
        ],
        out_specs=[rows(M_WIDTH, M_CHUNK, fwd), rows(M_WIDTH, M_CHUNK, bwd)],
        out_shape=[h_shape, h_shape],
        scratch_shapes=([pltpu.VMEM((M_DK, M_DV), F32)] * N_CHAINS
                        + [pltpu.VMEM((M_DK, LANE), F32)] * N_CHAINS
                        + [pltpu.VMEM((8, LANE), F32)] * N_CHAINS),
        compiler_params=_params(2),
        name="mlstm_scan",
    )(mq, mq, mq, mkt, mkt, mkt, mv, mv, mv, gates, gates, gates)


TM4 = 256
TILES_PER_SEQ4 = SEQ // TM4
TILES_PER_STEP4 = 2
NEG_BIG = -1e30


def _outproj_kernel(a_ref, hf_ref, hb_ref, mo_ref, x_ref, w_ref, gm_ref, gt1_ref, sh2_ref,
                    sc2_ref, g2_ref, wr_ref, br_ref,
                    x1_ref, h2_ref, ti_ref, tr_ref, tw_ref, tc_ref):
    for hv in range(TILES_PER_STEP4):
        _outproj_tile(hv, a_ref, hf_ref, hb_ref, mo_ref, x_ref, w_ref, gm_ref, gt1_ref, sh2_ref,
                      sc2_ref, g2_ref, wr_ref, br_ref,
                      x1_ref, h2_ref, ti_ref, tr_ref, tw_ref, tc_ref)


def _outproj_tile(hv, a_ref, hf_ref, hb_ref, mo_ref, x_ref, w_ref, gm_ref, gt1_ref, sh2_ref,
                  sc2_ref, g2_ref, wr_ref, br_ref,
                  x1_ref, h2_ref, ti_ref, tr_ref, tw_ref, tc_ref):
    i = pl.program_id(0)
    row = (i * TILES_PER_STEP4 + hv) // TILES_PER_SEQ4
    rs = slice(hv * TM4, (hv + 1) * TM4)
    acc = jnp.dot(a_ref[rs, :], w_ref[0:ATT_WIDTH, :], preferred_element_type=F32)
    for hd in range(M_HEADS):
        sl = slice(hd * M_DV, (hd + 1) * M_DV)
        hn = _rms(hf_ref[rs, sl] + hb_ref[rs, sl]) * gm_ref[:, sl]
        mo = mo_ref[rs, sl]
        ym = hn / (1.0 + jnp.exp(-mo))
        acc += jnp.dot(ym.astype(BF16), w_ref[ATT_WIDTH + hd * M_DV:ATT_WIDTH + (hd + 1) * M_DV, :],
                       preferred_element_type=F32)
    x1 = x_ref[rs, :] + gt1_ref[pl.ds(row, 1), :] * acc
    x1_ref[rs, :] = x1
    h2 = (_rms(x1) * g2_ref[...]) * (1.0 + sc2_ref[pl.ds(row, 1), :]) + sh2_ref[pl.ds(row, 1), :]
    h2_ref[rs, :] = h2
    h_hi = h2.astype(BF16)
    h_lo = (h2 - h_hi.astype(F32)).astype(BF16)
    hi_prod = jnp.dot(h_hi, wr_ref[...], preferred_element_type=F32)
    lo_prod = jnp.dot(h_lo, wr_ref[:, :LANE], preferred_element_type=F32)
    logits = hi_prod[:, :LANE] + lo_prod + hi_prod[:, LANE:] + br_ref[...]
    lane = lax.broadcasted_iota(jnp.int32, (TM4, LANE), 1)
    vals, idxs = [], []
    for _ in range(TOP_K):
        mx = jnp.max(logits, axis=-1, keepdims=True)
        ik = jnp.min(jnp.where(logits == mx, lane, LANE), axis=-1, keepdims=True)
        vals.append(mx)
        idxs.append(ik)
        logits = jnp.where(lane == ik, -jnp.inf, logits)
    es = [jnp.exp(vk - vals[0]) for vk in vals]
    tot = es[0] + es[1] + es[2] + es[3]
    chosen = jnp.zeros((TM4, LANE), F32)
    for kk in range(TOP_K):
        chosen = jnp.where(lane == idxs[kk], 1.0, chosen)
    ri = lax.broadcasted_iota(jnp.int32, (TM4, TM4), 0)
    ci = lax.broadcasted_iota(jnp.int32, (TM4, TM4), 1)
    before = jnp.where(ci < ri, 1.0, 0.0).astype(BF16)
    earlier = jnp.dot(before, chosen.astype(BF16), preferred_element_type=F32)
    ti = jnp.zeros((TM4, LANE), jnp.int32)
    tr = jnp.zeros((TM4, LANE), jnp.int32)
    tw = jnp.zeros((TM4, LANE), F32)
    for kk in range(TOP_K):
        rk = jnp.sum(jnp.where(lane == idxs[kk], earlier, 0.0), axis=-1, keepdims=True)
        ti = jnp.where(lane == kk, idxs[kk], ti)
        tr = jnp.where(lane == kk, rk.astype(jnp.int32), tr)
        tw = jnp.where(lane == kk, es[kk] / tot, tw)
    ti_ref[rs, :] = ti
    tr_ref[rs, :] = tr
    tw_ref[rs, :] = tw
    tc_ref[hv] = jnp.sum(chosen, axis=0, keepdims=True).astype(jnp.int32)


def _out_projection(a_lat, hf, hb, mo, x2, w_out_b, g_mlstm, gt1, sh2, sc2, g2, wr_both, br_p):
    full = lambda shape: pl.BlockSpec(shape, lambda i: (0, 0))
    rows = TILES_PER_STEP4 * TM4
    row_blk = lambda w: pl.BlockSpec((rows, w), lambda i: (i, 0))
    return pl.pallas_call(
        _outproj_kernel,
        grid=(T_LAT // rows,),
        in_specs=[
            row_blk(ATT_WIDTH), row_blk(M_WIDTH), row_blk(M_WIDTH), row_blk(M_WIDTH),
            row_blk(D_MODEL), full((ATT_WIDTH + M_WIDTH, D_MODEL)), full((1, M_WIDTH)),
            full((MOD_ROWS, D_MODEL)), full((MOD_ROWS, D_MODEL)), full((MOD_ROWS, D_MODEL)),
            full((1, D_MODEL)), full((D_MODEL, 2 * LANE)), full((1, LANE)),
        ],
        out_specs=[row_blk(D_MODEL), row_blk(D_MODEL), row_blk(LANE), row_blk(LANE), row_blk(LANE),
                   pl.BlockSpec((TILES_PER_STEP4, 1, LANE), lambda i: (i, 0, 0))],
        out_shape=[
            jax.ShapeDtypeStruct((T_LAT, D_MODEL), F32),
            jax.ShapeDtypeStruct((T_LAT, D_MODEL), F32),
            jax.ShapeDtypeStruct((T_LAT, LANE), jnp.int32),
            jax.ShapeDtypeStruct((T_LAT, LANE), jnp.int32),
            jax.ShapeDtypeStruct((T_LAT, LANE), F32),
            jax.ShapeDtypeStruct((T_LAT // TM4, 1, LANE), jnp.int32),
        ],
        compiler_params=_params(1),
        name="out_proj_router",
    )(a_lat, hf, hb, mo, x2, w_out_b, g_mlstm, gt1, sh2, sc2, g2, wr_both, br_p)


MB = 128
N_ASSIGN = T_LAT * TOP_K
N_BLOCKS = N_ASSIGN // MB + N_EXPERTS
P_ROWS = N_BLOCKS * MB
N_PADS = P_ROWS - N_ASSIGN
SB = 9
W_MAX = N_EXPERTS + (N_BLOCKS - N_EXPERTS) // SB
XROWS = SB * MB


def _routing_tables(top_idx, tile_rank, tile_counts):
    n_tiles = tile_counts.shape[0]
    counts = jnp.sum(tile_counts, axis=0)
    tile_base = jnp.cumsum(tile_counts, axis=0) - tile_counts
    nblk = (counts + MB - 1) // MB
    blk_start = jnp.cumsum(nblk) - nblk
    n_valid_blk = jnp.sum(nblk)
    offs = blk_start[None, :] * MB + tile_base
    e4 = top_idx.reshape(n_tiles, -1, TOP_K, 1)
    hit = e4 == jnp.arange(N_EXPERTS, dtype=jnp.int32)
    dest = jnp.sum(jnp.where(hit, offs[:, None, None, :], 0), axis=-1)
    dest = (dest.reshape(T_LAT, TOP_K) + tile_rank).reshape(N_ASSIGN).astype(jnp.int32)
    nwork = (nblk + SB - 1) // SB
    wend = jnp.cumsum(nwork)
    wstart = wend - nwork
    n_items = wend[-1]
    wid = jnp.arange(W_MAX, dtype=jnp.int32)
    valid = wid < n_items
    we = jnp.minimum(jnp.sum((wend[None, :] <= wid[:, None]).astype(jnp.int32), axis=1),
                     N_EXPERTS - 1)
    local = wid - wstart[we]
    work_blk = jnp.where(valid, blk_start[we] + local * SB, 0)
    work_n = jnp.where(valid, jnp.clip(nblk[we] - local * SB, 0, SB), 0)
    last_e = we[jnp.maximum(n_items - 1, 0)]
    work_e = jnp.where(valid, we, last_e)
    i32 = lambda a: a.astype(jnp.int32)
    return (dest, i32(counts), i32(blk_start), i32(nblk), i32(n_valid_blk)[None],
            i32(work_e), i32(work_blk), i32(work_n), i32(n_items))


TMD = 256
ROW_GROUP = 8


def _dispatch_kernel(dest_ref, cnt_ref, bs_ref, nb_ref, nvb_ref, h2_ref, x_hbm, zrow, sem, psem):
    i = pl.program_id(0)

    @pl.when(i == 0)
    def _():
        zrow[...] = jnp.zeros_like(zrow)

        def zero_row(r, carry):
            pltpu.make_async_copy(zrow.at[pl.ds(0, 1)], x_hbm.at[pl.ds(r, 1)], psem).start()
            return carry

        def expert(e, carry):
            first = bs_ref[e] * MB
            lax.fori_loop(first + cnt_ref[e], first + nb_ref[e] * MB, zero_row, 0)
            return carry
        lax.fori_loop(0, N_EXPERTS, expert, 0)
        lax.fori_loop(nvb_ref[0] * MB, P_ROWS, zero_row, 0)

    base = i * (TMD * TOP_K)

    for t in range(TMD):
        for kk in range(TOP_K):
            d = dest_ref[base + t * TOP_K + kk]
            pltpu.make_async_copy(h2_ref.at[pl.ds(t, 1)], x_hbm.at[pl.ds(d, 1)],
                                  sem).start(priority=kk % 2)
    for kk in range(TOP_K):
        pltpu.make_async_copy(h2_ref, x_hbm.at[pl.ds(0, TMD)], sem).wait()

    @pl.when(i == 0)
    def _():
        for _ in range(N_PADS // TMD):
            pltpu.make_async_copy(h2_ref, x_hbm.at[pl.ds(0, TMD)], psem).wait()


def _dispatch(h2, dest, counts, blk_start, nblk, n_valid_blk):
    grid_spec = pltpu.PrefetchScalarGridSpec(
        num_scalar_prefetch=5,
        grid=(T_LAT // TMD,),
        in_specs=[pl.BlockSpec((TMD, D_MODEL), lambda i, *_: (i, 0))],
        out_specs=pl.BlockSpec(memory_space=pl.ANY),
        scratch_shapes=[
            pltpu.VMEM((8, D_MODEL), F32),
            pltpu.SemaphoreType.DMA(()),
            pltpu.SemaphoreType.DMA(()),
        ],
    )
    return pl.pallas_call(
        _dispatch_kernel,
        grid_spec=grid_spec,
        out_shape=jax.ShapeDtypeStruct((P_ROWS, D_MODEL), F32),
        compiler_params=_params(1),
        name="moe_dispatch",
    )(dest, counts, blk_start, nblk, n_valid_blk, h2)


TF = 512
NF = D_FF // TF
BIG_BLKS = 4
TF2 = TF // 2
W_PRIORITY = 1
W_AHEAD = 2
W_SLOTS = W_AHEAD + 1
MOE_VMEM_LIMIT = 62 * 1024 * 1024


def _moe_kernel(we_ref, wb_ref, wn_ref, nvb_ref,
                x_hbm, w1_hbm, w2_hbm, b1g_ref, b1u_ref, b2_ref,
                y_hbm, xbuf, xb16, acc, w1g_buf, w1u_buf, w2_buf, xsem, ysem, wsem):
    w = pl.program_id(0)
    f = pl.program_id(1)
    nsub = wn_ref[w]
    r0 = wb_ref[w] * MB

    step = w * NF + f
    wslot = step % W_SLOTS

    def w_copies(item, ff, slot_):
        e = we_ref[item]
        sem_ = wsem.at[slot_]
        gate = pltpu.make_async_copy(w1_hbm.at[e, :, pl.ds(ff * TF, TF)], w1g_buf.at[slot_], sem_)
        up = pltpu.make_async_copy(w1_hbm.at[e, :, pl.ds(D_FF + ff * TF, TF)],
                                   w1u_buf.at[slot_], sem_)
        dn_a = pltpu.make_async_copy(w2_hbm.at[e, pl.ds(ff * TF, TF2), :],
                                     w2_buf.at[slot_, 0:TF2], sem_)
        dn_b = pltpu.make_async_copy(w2_hbm.at[e, pl.ds(ff * TF + TF2, TF2), :],
                                     w2_buf.at[slot_, TF2:TF], sem_)
        return gate, dn_a, up, dn_b

    def start_w(item, ff, slot_):
        gate, dn_a, up, dn_b = w_copies(item, ff, slot_)
        gate.start(priority=0)
        dn_a.start(priority=0)
        up.start(priority=1)
        dn_b.start(priority=1)

    @pl.when(jnp.logical_and(step == 0, nsub > 0))
    def _():
        for s in range(W_AHEAD):
            start_w(0, s, s)

    last_f = f == NF - 1
    nxt_item = w + (f + W_AHEAD) // NF
    nxt_f = (f + W_AHEAD) % NF

    @pl.when(jnp.logical_and(nxt_item < W_MAX, wn_ref[jnp.minimum(nxt_item, W_MAX - 1)] > 0))
    def _():
        start_w(nxt_item, nxt_f, (step + W_AHEAD) % W_SLOTS)

    @pl.when(nsub > 0)
    def _():
        for cp in w_copies(w, f, wslot):
            cp.wait()

    def x_copy(item, s):
        src = x_hbm.at[pl.ds(wb_ref[item] * MB + s * MB, MB)]
        return pltpu.make_async_copy(src, xbuf.at[pl.ds(s * MB, MB)], xsem)

    def start_x(item):
        def body(s, carry):
            x_copy(item, s).start()
            return carry
        lax.fori_loop(0, wn_ref[item], body, 0)

    @pl.when(jnp.logical_and(w == 0, f == 0))
    def _():
        start_x(0)

    @pl.when(f == 0)
    def _():
        def wait(s, carry):
            x_copy(w, s).wait()
            return carry
        lax.fori_loop(0, nsub, wait, 0)

        def body(s, carry):
            rs = pl.ds(pl.multiple_of(s * MB, MB), MB)
            xb16[rs, :] = xbuf[rs, :].astype(BF16)
            acc[rs, :] = jnp.broadcast_to(b2_ref[...], (MB, D_MODEL))
            return carry
        lax.fori_loop(0, nsub, body, 0)

    @pl.when(jnp.logical_and(f == 1, w + 1 < W_MAX))
    def _():
        start_x(w + 1)

    def y_copy(s):
        rs = pl.ds(pl.multiple_of(s * MB, MB), MB)
        return pltpu.make_async_copy(acc.at[rs], y_hbm.at[pl.ds(r0 + s * MB, MB)], ysem)

    def chunk(row, m):
        rs = pl.ds(row, m)
        xb = xb16[rs, :]
        gate = jnp.dot(xb, w1g_buf[wslot], preferred_element_type=F32) + b1g_ref[...]
        up = jnp.dot(xb, w1u_buf[wslot], preferred_element_type=F32) + b1u_ref[...]
        gate = jnp.minimum(gate, SWIGLU_LIMIT)
        up = jnp.clip(up, -SWIGLU_LIMIT, SWIGLU_LIMIT)
        glu = gate / (1.0 + jnp.exp(-SWIGLU_ALPHA * gate))
        act = (up + 1.0) * glu
        acc[rs, :] += jnp.dot(act, w2_buf[wslot], preferred_element_type=F32)

    def write_back(first_blk, n_blk):
        @pl.when(last_f)
        def _():
            for s in range(n_blk):
                y_copy(first_blk + s).start()

    nbig = nsub // BIG_BLKS

    def big(c, carry):
        chunk(pl.multiple_of(c * (BIG_BLKS * MB), BIG_BLKS * MB), BIG_BLKS * MB)
        write_back(c * BIG_BLKS, BIG_BLKS)
        return carry
    lax.fori_loop(0, nbig, big, 0)

    done = nbig * BIG_BLKS
    size = BIG_BLKS // 2
    while size >= 1:
        has = ((nsub - done) // size) % 2 == 1
        first = done + ((nsub - done) // (2 * size)) * (2 * size)

        @pl.when(has)
        def _(first=first, size=size):
            chunk(pl.multiple_of(first * MB, size * MB), size * MB)
            write_back(first, size)
        size //= 2

    @pl.when(last_f)
    def _():
        def wait(s, carry):
            y_copy(s).wait()
            return carry
        lax.fori_loop(0, nsub, wait, 0)

    @pl.when(jnp.logical_and(w == pl.num_programs(0) - 1, f == NF - 1))
    def _():
        acc[0:MB, :] = jnp.zeros((MB, D_MODEL), F32)

        def tail_copy(bk):
            return pltpu.make_async_copy(acc.at[0:MB], y_hbm.at[pl.ds(bk * MB, MB)], ysem)

        def start(bk, carry):
            tail_copy(bk).start()
            return carry
        lax.fori_loop(nvb_ref[0], N_BLOCKS, start, 0)

        def wait(bk, carry):
            tail_copy(bk).wait()
            return carry
        lax.fori_loop(nvb_ref[0], N_BLOCKS, wait, 0)


def _moe(x_sorted, w1, b1, w2, b2, work_e, work_blk, work_n, n_valid_blk, n_items):
    b1r = b1.reshape(N_EXPERTS, 1, 2 * D_FF)
    b2r = b2.reshape(N_EXPERTS, 1, D_MODEL)
    grid_spec = pltpu.PrefetchScalarGridSpec(
        num_scalar_prefetch=4,
        grid=(n_items, NF),
        in_specs=[
            pl.BlockSpec(memory_space=pl.ANY),
            pl.BlockSpec(memory_space=pl.ANY),
            pl.BlockSpec(memory_space=pl.ANY),
            pl.BlockSpec((None, 1, TF), lambda w, f, we, *_: (we[w], 0, f)),
            pl.BlockSpec((None, 1, TF), lambda w, f, we, *_: (we[w], 0, NF + f)),
            pl.BlockSpec((None, 1, D_MODEL), lambda w, f, we, *_: (we[w], 0, 0)),
        ],
        out_specs=pl.BlockSpec(memory_space=pl.ANY),
        scratch_shapes=[
            pltpu.VMEM((XROWS, D_MODEL), F32),
            pltpu.VMEM((XROWS, D_MODEL), BF16),
            pltpu.VMEM((XROWS, D_MODEL), F32),
            pltpu.VMEM((W_SLOTS, D_MODEL, TF), F32),
            pltpu.VMEM((W_SLOTS, D_MODEL, TF), F32),
            pltpu.VMEM((W_SLOTS, TF, D_MODEL), F32),
            pltpu.SemaphoreType.DMA(()),
            pltpu.SemaphoreType.DMA(()),
            pltpu.SemaphoreType.DMA((W_SLOTS,)),
        ],
    )
    return pl.pallas_call(
        _moe_kernel,
        grid_spec=grid_spec,
        out_shape=jax.ShapeDtypeStruct((P_ROWS, D_MODEL), F32),
        compiler_params=pltpu.CompilerParams(
            dimension_semantics=("arbitrary", "arbitrary"), vmem_limit_bytes=MOE_VMEM_LIMIT),
        name="moe_experts",
    )(work_e, work_blk, work_n, n_valid_blk, x_sorted, w1, w2, b1r, b1r, b2r)


TM7 = 256
N_TILES7 = T_LAT // TM7
TILES_PER_SEQ7 = SEQ // TM7
N_INLINE7 = TM7


def _final_kernel(pos_ref, x1_ref, tw_ref, gt2_ref, gf_ref, y_hbm, o_ref, ybuf, sem):
    i = pl.program_id(0)
    slot = i % 2

    def gather_token(base, slot_, t):
        for kk in range(TOP_K):
            p = pos_ref[base + t * TOP_K + kk]
            pltpu.make_async_copy(y_hbm.at[pl.ds(p, 1)], ybuf.at[slot_, kk, pl.ds(t, 1)],
                                  sem.at[slot_]).start(priority=kk % 2)

    def start_gather(tile, slot_, first_token):
        base = tile * (TM7 * TOP_K)

        def body(tg, carry):
            t0 = pl.multiple_of(tg * ROW_GROUP, ROW_GROUP)
            for j in range(ROW_GROUP):
                gather_token(base, slot_, t0 + j)
            return carry
        lax.fori_loop(first_token // ROW_GROUP, TM7 // ROW_GROUP, body, 0)

    @pl.when(i == 0)
    def _():
        start_gather(0, 0, 0)

    @pl.when(i + 1 < N_TILES7)
    def _():
        start_gather(i + 1, 1 - slot, N_INLINE7)

    for kk in range(TOP_K):
        pltpu.make_async_copy(y_hbm.at[pl.ds(0, TM7)], ybuf.at[slot, kk], sem.at[slot]).wait()

    nxt_base = jnp.minimum(i + 1, N_TILES7 - 1) * (TM7 * TOP_K)
    for t in range(N_INLINE7):
        gather_token(nxt_base, 1 - slot, t)

    row = i // TILES_PER_SEQ7
    tw = tw_ref[...]
    mix = jnp.zeros((TM7, D_MODEL), F32)
    for kk in range(TOP_K):
        mix += ybuf[slot, kk] * tw[:, kk:kk + 1]
    xo = x1_ref[...] + gt2_ref[pl.ds(row, 1), :] * mix
    o_ref[...] = _rms(xo) * gf_ref[...]

    @pl.when(i == N_TILES7 - 1)
    def _():
        for kk in range(TOP_K):
            pltpu.make_async_copy(y_hbm.at[pl.ds(0, N_INLINE7)],
                                  ybuf.at[1 - slot, kk, pl.ds(0, N_INLINE7)],
                                  sem.at[1 - slot]).wait()


def _final(dest, x1, top_w, gt2, g_final, y_sorted):
    grid_spec = pltpu.PrefetchScalarGridSpec(
        num_scalar_prefetch=1,
        grid=(N_TILES7,),
        in_specs=[
            pl.BlockSpec((TM7, D_MODEL), lambda i, *_: (i, 0)),
            pl.BlockSpec((TM7, LANE), lambda i, *_: (i, 0)),
            pl.BlockSpec((MOD_ROWS, D_MODEL), lambda i, *_: (0, 0)),
            pl.BlockSpec((1, D_MODEL), lambda i, *_: (0, 0)),
            pl.BlockSpec(memory_space=pl.ANY),
        ],
        out_specs=pl.BlockSpec((TM7, D_MODEL), lambda i, *_: (i, 0)),
        scratch_shapes=[
            pltpu.VMEM((2, TOP_K, TM7, D_MODEL), F32),
            pltpu.SemaphoreType.DMA((2,)),
        ],
    )
    return pl.pallas_call(
        _final_kernel,
        grid_spec=grid_spec,
        out_shape=jax.ShapeDtypeStruct((T_LAT, D_MODEL), F32),
        compiler_params=_params(1),
        name="combine_final_norm",
    )(dest, x1, top_w, gt2, g_final, y_sorted)


def _rope_tables():
    rows = SEQ // GRID_W
    row = jnp.repeat(jnp.arange(rows, dtype=F32), GRID_W)
    col = jnp.tile(jnp.arange(GRID_W, dtype=F32), rows)
    inv = ROPE_THETA ** (-jnp.arange(0, AXIS_DIM, 2, dtype=F32) / AXIS_DIM)
    ang = jnp.concatenate([row[:, None] * inv, col[:, None] * inv], axis=-1)
    cos = jnp.repeat(jnp.cos(ang), 2, axis=-1)
    sin = jnp.repeat(jnp.sin(ang), 2, axis=-1)
    sign = jnp.tile(jnp.array([-1.0, 1.0], F32), HEAD_DIM // 2)
    cos_t = jnp.concatenate([cos, jnp.ones((ROWS1, HEAD_DIM), F32)], axis=0)
    sin_t = jnp.concatenate([sin * sign, jnp.zeros((ROWS1, HEAD_DIM), F32)], axis=0)
    return cos_t, sin_t


def kernel(x, c, ctx, c_ctx, w_mod, b_mod, g_norm1, w_in, b_in, g_q, g_k, g_mlstm, w_out,
           g_norm2, w_router, b_router, w1, b1, w2, b2, g_final):
    x2 = x.reshape(T_LAT, D_MODEL)
    ctx2 = ctx.reshape(T_CTX, D_MODEL)
    cos_t, sin_t = _rope_tables()
    assert w_mod.shape[0] == 1
    l = 0
    c_rows = jnp.concatenate(
        [c, c_ctx[None, :], jnp.zeros((MOD_ROWS - BATCH - 1, D_MODEL), F32)], axis=0)
    mod = _modulation(c_rows, w_mod[l], b_mod[l][None, :])
    sh1, sc1, gt1, sh2, sc2, gt2 = [mod[:, k * D_MODEL:(k + 1) * D_MODEL]
                                    for k in range(N_ADALN)]
    pad = IN_COLS_PAD - IN_COLS
    w_main = w_in[l].astype(BF16)
    w_gate = jnp.pad(w_in[l][:, C_G:], ((0, 0), (0, pad))).astype(BF16)
    b_main = b_in[l][None, :C_G]
    b_gate = jnp.pad(b_in[l][C_G:], (0, pad))[None, :]
    q, k, v, mq, mkt, mv, mo, gates = _in_projection(
        x2, ctx2, sh1, sc1, g_norm1[l][None, :], w_main, b_main, w_gate, b_gate,
        g_q[l][None, :], g_k[l][None, :], cos_t, sin_t)
    a_lat = _attention(q, k, v)
    hf, hb = _mlstm(mq, mkt, mv, gates)
    wr_p = jnp.pad(w_router[l], ((0, 0), (0, LANE - N_EXPERTS)))
    wr_hi = wr_p.astype(BF16)
    wr_lo = (wr_p - wr_hi.astype(F32)).astype(BF16)
    br_p = jnp.pad(b_router[l], (0, LANE - N_EXPERTS), constant_values=NEG_BIG)[None, :]
    x1, h2, top_i, top_r, top_w, tile_cnt = _out_projection(
        a_lat, hf, hb, mo, x2, w_out[l].astype(BF16), g_mlstm[l][None, :],
        gt1, sh2, sc2, g_norm2[l][None, :], jnp.concatenate([wr_hi, wr_lo], axis=1), br_p)
    dest, counts, blk_start, nblk, n_valid_blk, work_e, work_blk, work_n, n_items = _routing_tables(
        top_i[:, :TOP_K], top_r[:, :TOP_K], tile_cnt[:, 0, :N_EXPERTS])
    x_sorted = _dispatch(h2, dest, counts, blk_start, nblk, n_valid_blk)
    y_sorted = _moe(x_sorted, w1[l], b1[l], w2[l], b2[l], work_e, work_blk, work_n, n_valid_blk,
                    n_items)
    out = _final(dest, x1, top_w, gt2, g_final[None, :], y_sorted)
    return out.reshape(BATCH, SEQ, D_MODEL)
```
